```python
import math
import jax, jax.numpy as jnp
from jax import lax
import numpy as np

D_MODEL = 1024
BATCH = 2
SEQ = 16384
DEPTH = 2
DEC_BATCH = 8
DEC_SEQ = 16
PAST_LEN = 2048

CHUNK = 64
N_MIXERS = 2
N_LAYERS_A = (DEPTH + 1) // 2
N_LAYERS_B = DEPTH // 2
A_HEADS = 16
A_HEAD_DIM = D_MODEL // A_HEADS
BAND_CHUNKS = 8
BAND_PAST = BAND_CHUNKS * CHUNK
REL_CLIP = 128
B_HEADS = 4
B_KEY_DIM = D_MODEL // 2 // B_HEADS
B_VAL_DIM = D_MODEL // B_HEADS
B_QK = B_HEADS * B_KEY_DIM
B_VD = B_HEADS * B_VAL_DIM
B_GATE_RANK = 16
B_GATE_NORM = 16.0
D_FF = -(-8 * D_MODEL // 768) * 256
ALPHA = (2.0 * DEPTH) ** 0.25
BETA = (8.0 * DEPTH) ** -0.25
LN_EPS = 1e-5
GN_EPS = 1e-6
NEG_INF = -1e30

kernel_name = "hybrid_stream_chunkattn_gla_step"


def _layer_norm(x, g, b):
    xf = x.astype(jnp.float32)
    mu = jnp.mean(xf, axis=-1, keepdims=True)
    var = jnp.mean(jnp.square(xf - mu), axis=-1, keepdims=True)
    return ((xf - mu) * lax.rsqrt(var + LN_EPS)).astype(x.dtype) * g + b


def _swiglu(x, w_in, w_out):
    gate, up = jnp.split(x @ w_in, 2, axis=-1)
    return (jax.nn.silu(gate) * up) @ w_out


def _rel_bias(rel_table, q_pos, k_pos):
    d = jnp.clip(q_pos[:, None] - k_pos[None, :], -REL_CLIP, REL_CLIP) + REL_CLIP
    return jnp.transpose(rel_table[d], (2, 0, 1)).astype(jnp.float32)


def _attend(q, k, v, bias, mask):
    s = jnp.einsum('bqhd,bkhd->bhqk', q.astype(jnp.float32), k.astype(jnp.float32))
    s = s * (A_HEAD_DIM ** -0.5) + bias[None]
    if mask is not None:
        s = jnp.where(mask[None, None], s, NEG_INF)
    p = jax.nn.softmax(s, axis=-1).astype(v.dtype)
    return jnp.einsum('bhqk,bkhd->bqhd', p, v)


def _split_heads(t, h, d):
    return t.reshape(t.shape[0], t.shape[1], h, d)


def mixer_a_prompt(x, w_in, rel_table, w_out):
    bsz, t_len, _ = x.shape
    q, k, v = jnp.split(x @ w_in, 3, axis=-1)
    q, k, v = (_split_heads(a, A_HEADS, A_HEAD_DIM) for a in (q, k, v))
    n_chunks = t_len // CHUNK
    band = BAND_PAST + CHUNK
    pad = ((0, 0), (BAND_PAST, 0), (0, 0), (0, 0))
    kp = jnp.pad(k, pad)
    vp = jnp.pad(v, pad)
    offs_q = jnp.arange(CHUNK)
    offs_k = jnp.arange(band) - BAND_PAST
    bias = _rel_bias(rel_table, offs_q, offs_k)

    def one_chunk(c):
        start = c * CHUNK
        qc = lax.dynamic_slice_in_dim(q, start, CHUNK, axis=1)
        kc = lax.dynamic_slice_in_dim(kp, start, band, axis=1)
        vc = lax.dynamic_slice_in_dim(vp, start, band, axis=1)
        valid = (start + offs_k) >= 0
        mask = jnp.broadcast_to(valid[None, :], (CHUNK, band))
        return _attend(qc, kc, vc, bias, mask)

    out = lax.map(one_chunk, jnp.arange(n_chunks))
    out = jnp.transpose(out, (1, 0, 2, 3, 4)).reshape(bsz, t_len, D_MODEL)
    keep = min(BAND_PAST, t_len)
    return out @ w_out, k[:, t_len - keep:], v[:, t_len - keep:]


def mixer_a_sample(x, cache_k, cache_v, w_in, rel_table, w_out):
    bsz, t_len, _ = x.shape
    n_cache = cache_k.shape[1]
    q, k, v = jnp.split(x @ w_in, 3, axis=-1)
    q, k, v = (_split_heads(a, A_HEADS, A_HEAD_DIM) for a in (q, k, v))
    kk = jnp.concatenate([cache_k.astype(k.dtype), k], axis=1)
    vv = jnp.concatenate([cache_v.astype(v.dtype), v], axis=1)
    bias = _rel_bias(rel_table, n_cache + jnp.arange(t_len), jnp.arange(n_cache + t_len))
    out = _attend(q, kk, vv, bias, None).reshape(bsz, t_len, D_MODEL)
    return out @ w_out, k, v


def _gla_project(x, w_in, w_gk_up, b_gk):
    q, k, v, r, gk_low = jnp.split(x @ w_in, [B_QK, 2 * B_QK, 2 * B_QK + B_VD, 2 * B_QK + 2 * B_VD], axis=-1)
    q = _split_heads(q, B_HEADS, B_KEY_DIM).astype(jnp.float32) * (B_KEY_DIM ** -0.5)
    k = _split_heads(k, B_HEADS, B_KEY_DIM).astype(jnp.float32)
    v = _split_heads(v, B_HEADS, B_VAL_DIM).astype(jnp.float32)
    g = jax.nn.log_sigmoid((gk_low @ w_gk_up + b_gk).astype(jnp.float32)) / B_GATE_NORM
    g = _split_heads(g, B_HEADS, B_KEY_DIM)
    return q, k, v, g, r


def _gla_chunk(s_prev, q, k, v, g):
    b = jnp.cumsum(g, axis=1)
    c_len = q.shape[1]
    causal = jnp.tril(jnp.ones((c_len, c_len), bool))
    expo = jnp.where(causal[None, :, :, None, None], b[:, :, None] - b[:, None, :], -jnp.inf)
    attn = jnp.einsum('bihd,bjhd,bijhd->bhij', q, k, jnp.exp(expo))
    o = jnp.einsum('bhij,bjhv->bihv', attn, v) + jnp.einsum('bihd,bhdv->bihv', q * jnp.exp(b), s_prev)
    b_last = b[:, -1]
    k_dec = k * jnp.exp(b_last[:, None] - b)
    s_new = s_prev * jnp.exp(b_last)[..., None] + jnp.einsum('bjhd,bjhv->bhdv', k_dec, v)
    return s_new, o


def _gla_output(o, r, gn_gain, w_out):
    o = o * lax.rsqrt(jnp.mean(jnp.square(o), axis=-1, keepdims=True) + GN_EPS)
    o = o.reshape(o.shape[0], o.shape[1], B_VD).astype(r.dtype) * gn_gain
    return (o * jax.nn.silu(r)) @ w_out


def mixer_b_prompt(x, w_in, w_gk_up, b_gk, gn_gain, w_out):
    bsz, t_len, _ = x.shape
    q, k, v, g, r = _gla_project(x, w_in, w_gk_up, b_gk)
    n_chunks = t_len // CHUNK

    def to_chunks(a):
        return jnp.transpose(a.reshape(bsz, n_chunks, CHUNK, a.shape[2], a.shape[3]), (1, 0, 2, 3, 4))

    def step(s, inp):
        qc, kc, vc, gc = inp
        return _gla_chunk(s, qc, kc, vc, gc)

    s0 = jnp.zeros((bsz, B_HEADS, B_KEY_DIM, B_VAL_DIM), jnp.float32)
    s_fin, o = lax.scan(step, s0, (to_chunks(q), to_chunks(k), to_chunks(v), to_chunks(g)))
    o = jnp.transpose(o, (1, 0, 2, 3, 4)).reshape(bsz, t_len, B_HEADS, B_VAL_DIM)
    return _gla_output(o, r, gn_gain, w_out), s_fin


def mixer_b_sample(x, state, w_in, w_gk_up, b_gk, gn_gain, w_out):
    q, k, v, g, r = _gla_project(x, w_in, w_gk_up, b_gk)
    s_new, o = _gla_chunk(state.astype(jnp.float32), q, k, v, g)
    return _gla_output(o, r, gn_gain, w_out), s_new


def setup_inputs(seed: int = 0) -> dict:
    key = jax.random.key(seed)
    ks = jax.random.split(key, 24)

    def nrm(k, shape, s):
        return jax.random.normal(k, shape, jnp.float32) * s

    a_cache_len = min(BAND_PAST, PAST_LEN)
    s_in = D_MODEL ** -0.5
    w_in_a = jnp.concatenate([nrm(ks[5], (N_LAYERS_A, D_MODEL, 2 * D_MODEL), s_in),
                              nrm(ks[6], (N_LAYERS_A, D_MODEL, D_MODEL), s_in * BETA)], axis=-1)
    w_in_b = jnp.concatenate([nrm(ks[9], (N_LAYERS_B, D_MODEL, 2 * B_QK), s_in),
                              nrm(ks[10], (N_LAYERS_B, D_MODEL, B_VD), s_in * BETA),
                              nrm(ks[11], (N_LAYERS_B, D_MODEL, B_VD), s_in),
                              nrm(ks[12], (N_LAYERS_B, D_MODEL, B_GATE_RANK), s_in)], axis=-1)
    return {
        "x_prompt": nrm(ks[0], (BATCH, SEQ, D_MODEL), 1.0),
        "x_sample": nrm(ks[1], (DEC_BATCH, DEC_SEQ, D_MODEL), 1.0),
        "cache_a_k": nrm(ks[2], (N_LAYERS_A, DEC_BATCH, a_cache_len, A_HEADS, A_HEAD_DIM), 1.0),
        "cache_a_v": nrm(ks[3], (N_LAYERS_A, DEC_BATCH, a_cache_len, A_HEADS, A_HEAD_DIM), BETA),
        "state_b": nrm(ks[4], (N_LAYERS_B, DEC_BATCH, B_HEADS, B_KEY_DIM, B_VAL_DIM), 0.5),
        "w_in_a": w_in_a,
        "rel_bias_a": nrm(ks[7], (N_LAYERS_A, 2 * REL_CLIP + 1, A_HEADS), 0.5),
        "w_out_a": nrm(ks[8], (N_LAYERS_A, D_MODEL, D_MODEL), s_in * BETA),
        "w_in_b": w_in_b,
        "w_gk_up_b": nrm(ks[13], (N_LAYERS_B, B_GATE_RANK, B_QK), B_GATE_RANK ** -0.5),
        "b_gk_b": nrm(ks[14], (N_LAYERS_B, B_QK), 0.1),
        "gn_gain_b": 1.0 + nrm(ks[15], (N_LAYERS_B, B_VD), 0.02),
        "w_out_b": nrm(ks[16], (N_LAYERS_B, B_VD, D_MODEL), (B_VD ** -0.5) * BETA),
        "w_ffn_in": nrm(ks[17], (DEPTH, D_MODEL, 2 * D_FF), s_in * BETA),
        "w_ffn_out": nrm(ks[18], (DEPTH, D_FF, D_MODEL), (D_FF ** -0.5) * BETA),
        "ln1_g": 1.0 + nrm(ks[19], (DEPTH, D_MODEL), 0.02),
        "ln1_b": nrm(ks[20], (DEPTH, D_MODEL), 0.02),
        "ln2_g": 1.0 + nrm(ks[21], (DEPTH, D_MODEL), 0.02),
        "ln2_b": nrm(ks[22], (DEPTH, D_MODEL), 0.02),
    }


def reference(x_prompt, x_sample, cache_a_k, cache_a_v, state_b,
              w_in_a, rel_bias_a, w_out_a,
              w_in_b, w_gk_up_b, b_gk_b, gn_gain_b, w_out_b,
              w_ffn_in, w_ffn_out, ln1_g, ln1_b, ln2_g, ln2_b):
    xp, xs = x_prompt, x_sample
    ak_p, av_p, sb_p, ak_s, av_s, sb_s = [], [], [], [], [], []
    for i in range(DEPTH):
        j = i // N_MIXERS
        if i % N_MIXERS == 0:
            mp, kp_new, vp_new = mixer_a_prompt(xp, w_in_a[j], rel_bias_a[j], w_out_a[j])
            ms, ks_new, vs_new = mixer_a_sample(xs, cache_a_k[j], cache_a_v[j], w_in_a[j], rel_bias_a[j], w_out_a[j])
            ak_p.append(kp_new); av_p.append(vp_new)
            ak_s.append(ks_new); av_s.append(vs_new)
        else:
            mp, sp_new = mixer_b_prompt(xp, w_in_b[j], w_gk_up_b[j], b_gk_b[j], gn_gain_b[j], w_out_b[j])
            ms, ss_new = mixer_b_sample(xs, state_b[j], w_in_b[j], w_gk_up_b[j], b_gk_b[j], gn_gain_b[j], w_out_b[j])
            sb_p.append(sp_new); sb_s.append(ss_new)
        xp = _layer_norm(ALPHA * xp + mp, ln1_g[i], ln1_b[i])
        xs = _layer_norm(ALPHA * xs + ms, ln1_g[i], ln1_b[i])
        xp = _layer_norm(ALPHA * xp + _swiglu(xp, w_ffn_in[i], w_ffn_out[i]), ln2_g[i], ln2_b[i])
        xs = _layer_norm(ALPHA * xs + _swiglu(xs, w_ffn_in[i], w_ffn_out[i]), ln2_g[i], ln2_b[i])
    return (xp, xs, jnp.stack(ak_p), jnp.stack(av_p), jnp.stack(sb_p),
            jnp.stack(ak_s), jnp.stack(av_s), jnp.stack(sb_s))
```

```python
import functools

import jax
import jax.numpy as jnp
from jax import lax
from jax.experimental import pallas as pl
from jax.experimental.pallas import tpu as pltpu

F32 = jnp.float32
BF16 = jnp.bfloat16

D_MODEL = 1024
CHUNK = 64
A_HEADS = 16
A_HEAD_DIM = D_MODEL // A_HEADS
BAND_CHUNKS = 8
BAND_PAST = BAND_CHUNKS * CHUNK
REL_CLIP = 128
B_HEADS = 4
B_KEY_DIM = D_MODEL // 2 // B_HEADS
B_VAL_DIM = D_MODEL // B_HEADS
B_QK = B_HEADS * B_KEY_DIM
B_VD = B_HEADS * B_VAL_DIM
B_GATE_RANK = 16
B_GATE_NORM = 16.0
D_FF = -(-8 * D_MODEL // 768) * 256
DEPTH = 2
ALPHA = (2.0 * DEPTH) ** 0.25
LN_EPS = 1e-5
GN_EPS = 1e-6
NEG_INF = -1e30

LANES = 128
SUB_BLOCK = 16
ATT_TILE = 4 * CHUNK
VMEM_LIMIT = 56 * 1024 * 1024


def _dot(a, b):
    return jnp.dot(a, b, preferred_element_type=F32)


def _dot_nt(a, b):
    return lax.dot_general(a, b, (((1,), (1,)), ((), ())), preferred_element_type=F32)


def _dot_tn(a, b):
    return lax.dot_general(a, b, (((0,), (0,)), ((), ())), preferred_element_type=F32)


def _params(*sem):
    return pltpu.CompilerParams(dimension_semantics=sem, vmem_limit_bytes=VMEM_LIMIT)


def _row_tile(m, pref):
    t = min(m, pref)
    assert m % t == 0
    return t


def _linear_kernel(x_ref, w_ref, *o_refs, n_chunk):
    xb = x_ref[...].astype(BF16)
    col = 0
    for o_ref in o_refs:
        nj = o_ref.shape[1]
        for c in range(0, nj, n_chunk):
            o_ref[:, c:c + n_chunk] = _dot(xb, w_ref[:, col + c:col + c + n_chunk]).astype(o_ref.dtype)
        col += nj


def _linear(x, w, widths, dtype, tm=512, name="linear"):
    m, k = x.shape
    tm = _row_tile(m, tm)
    n_chunk = 512
    assert all(wd % n_chunk == 0 for wd in widths) and sum(widths) == w.shape[1]
    return pl.pallas_call(
        functools.partial(_linear_kernel, n_chunk=n_chunk),
        grid=(m // tm,),
        in_specs=[pl.BlockSpec((tm, k), lambda i: (i, 0)),
                  pl.BlockSpec(w.shape, lambda i: (0, 0))],
        out_specs=[pl.BlockSpec((tm, wd), lambda i: (i, 0)) for wd in widths],
        out_shape=[jax.ShapeDtypeStruct((m, wd), dtype) for wd in widths],
        compiler_params=_params("parallel"),
        name=name,
    )(x, w)


def _res_ln(x, y, g, b):
    t = ALPHA * x + y
    mu = jnp.mean(t, axis=-1, keepdims=True)
    d = t - mu
    var = jnp.mean(d * d, axis=-1, keepdims=True)
    return d * lax.rsqrt(var + LN_EPS) * g + b


def _proj_ln_kernel(a_ref, w_ref, x_ref, g_ref, b_ref, o_ref):
    y = _dot(a_ref[...], w_ref[...])
    o_ref[...] = _res_ln(x_ref[...], y, g_ref[...], b_ref[...])


def _proj_ln(a, w, x, g, b, tm=512, name="proj_ln"):
    m, k = a.shape
    d = w.shape[1]
    tm = _row_tile(m, tm)
    return pl.pallas_call(
        _proj_ln_kernel,
        grid=(m // tm,),
        in_specs=[pl.BlockSpec((tm, k), lambda i: (i, 0)),
                  pl.BlockSpec((k, d), lambda i: (0, 0)),
                  pl.BlockSpec((tm, d), lambda i: (i, 0)),
                  pl.BlockSpec((1, d), lambda i: (0, 0)),
                  pl.BlockSpec((1, d), lambda i: (0, 0))],
        out_specs=pl.BlockSpec((tm, d), lambda i: (i, 0)),
        out_shape=jax.ShapeDtypeStruct((m, d), F32),
        compiler_params=_params("parallel"),
        name=name,
    )(a, w, x, g.reshape(1, d), b.reshape(1, d))


def _ffn_kernel(x_ref, wi_ref, wo_ref, g_ref, b_ref, o_ref, h_ref, *, ff_chunk):
    x = x_ref[...]
    xb = x.astype(BF16)
    d_ff = h_ref.shape[1]
    for c in range(0, d_ff, ff_chunk):
        gate = _dot(xb, wi_ref[:, c:c + ff_chunk])
        up = _dot(xb, wi_ref[:, d_ff + c:d_ff + c + ff_chunk])
        h_ref[:, c:c + ff_chunk] = (gate * jax.nn.sigmoid(gate) * up).astype(BF16)
    y = _dot(h_ref[...], wo_ref[...])
    o_ref[...] = _res_ln(x, y, g_ref[...], b_ref[...])


def _ffn_ln(x, wi, wo, g, b, tm=512, name="ffn_ln"):
    m, d = x.shape
    d_ff = wo.shape[0]
    tm = _row_tile(m, tm)
    ff_chunk = 256
    assert d_ff % ff_chunk == 0
    return pl.pallas_call(
        functools.partial(_ffn_kernel, ff_chunk=ff_chunk),
        grid=(m // tm,),
        in_specs=[pl.BlockSpec((tm, d), lambda i: (i, 0)),
                  pl.BlockSpec(wi.shape, lambda i: (0, 0), pipeline_mode=pl.Buffered(1)),
                  pl.BlockSpec(wo.shape, lambda i: (0, 0), pipeline_mode=pl.Buffered(1)),
                  pl.BlockSpec((1, d), lambda i: (0, 0)),
                  pl.BlockSpec((1, d), lambda i: (0, 0))],
        out_specs=pl.BlockSpec((tm, d), lambda i: (i, 0)),
        out_shape=jax.ShapeDtypeStruct((m, d), F32),
        scratch_shapes=[pltpu.VMEM((tm, d_ff), BF16)],
        compiler_params=_params("parallel"),
        name=name,
    )(x, wi, wo, g.reshape(1, d), b.reshape(1, d))


def _softmax_pv(s_parts, v_parts):
    m = s_parts[0].max(axis=-1, keepdims=True)
    for s in s_parts[1:]:
        m = jnp.maximum(m, s.max(axis=-1, keepdims=True))
    acc, l = None, None
    for s, v in zip(s_parts, v_parts):
        e = jnp.exp(s - m)
        ls = e.sum(axis=-1, keepdims=True)
        pv = _dot(e.astype(BF16), v)
        acc = pv if acc is None else acc + pv
        l = ls if l is None else l + ls
    return acc / l


def _attn_prompt_kernel(q_ref, ka_ref, kb_ref, kc_ref, va_ref, vb_ref, vc_ref, bias_ref, o_ref,
                        kwin_ref, vwin_ref):
    tile = pl.program_id(1)
    band = BAND_PAST + CHUNK
    for j, (k_ref, v_ref) in enumerate(((ka_ref, va_ref), (kb_ref, vb_ref), (kc_ref, vc_ref))):
        kwin_ref[j * ATT_TILE:(j + 1) * ATT_TILE, :] = k_ref[0]
        vwin_ref[j * ATT_TILE:(j + 1) * ATT_TILE, :] = v_ref[0]
    lane = lax.broadcasted_iota(jnp.int32, (CHUNK, LANES), 1)
    low_half = lane < A_HEAD_DIM
    win_col = lax.broadcasted_iota(jnp.int32, (1, band), 1)

    def chunk_body(c, carry):
        row0 = pl.multiple_of(c * CHUNK, CHUNK)
        valid = (tile - 2) * ATT_TILE + c * CHUNK + win_col >= 0
        for p in range(D_MODEL // LANES):
            cols = slice(p * LANES, (p + 1) * LANES)
            qp = q_ref[0, pl.ds(row0, CHUNK), cols]
            kp = kwin_ref[pl.ds(row0, band), cols]
            vp = vwin_ref[pl.ds(row0, band), cols]
            outs = []
            for hh in range(2):
                qm = jnp.where(low_half if hh == 0 else jnp.logical_not(low_half), qp, jnp.zeros_like(qp))
                s = _dot_nt(qm, kp) + bias_ref[2 * p + hh]
                s = jnp.where(valid, s, NEG_INF)
                outs.append(_softmax_pv([s], [vp]))
            o_ref[0, pl.ds(row0, CHUNK), cols] = jnp.where(low_half, outs[0], outs[1]).astype(o_ref.dtype)
        return carry

    lax.fori_loop(0, ATT_TILE // CHUNK, chunk_body, 0)


def _attn_prompt(q, k, v, bias):
    bsz, t_len, d = q.shape
    assert t_len % ATT_TILE == 0 and BAND_PAST == 2 * ATT_TILE
    blk = (1, ATT_TILE, d)

    def past(n):
        return lambda b, i: (b, jnp.maximum(i - n, 0), 0)

    kv_specs = [pl.BlockSpec(blk, past(2)), pl.BlockSpec(blk, past(1)), pl.BlockSpec(blk, past(0))]
    return pl.pallas_call(
        _attn_prompt_kernel,
        grid=(bsz, t_len // ATT_TILE),
        in_specs=[pl.BlockSpec(blk, lambda b, i: (b, i, 0))] + kv_specs + kv_specs
                 + [pl.BlockSpec(bias.shape, lambda b, i: (0, 0, 0))],
        out_specs=pl.BlockSpec(blk, lambda b, i: (b, i, 0)),
        out_shape=jax.ShapeDtypeStruct((bsz, t_len, d), BF16),
        scratch_shapes=[pltpu.VMEM((3 * ATT_TILE, d), BF16), pltpu.VMEM((3 * ATT_TILE, d), BF16)],
        compiler_params=_params("parallel", "arbitrary"),
        name="attn_prompt",
    )(q, k, k, k, v, v, v, bias)


def _attn_sample_kernel(q_ref, kn_ref, vn_ref, kc_ref, vc_ref, bc_ref, bn_ref, o_ref):
    t_len = q_ref.shape[1]
    lane = lax.broadcasted_iota(jnp.int32, (t_len, LANES), 1)
    low_half = lane < A_HEAD_DIM
    for p in range(D_MODEL // LANES):
        cols = slice(p * LANES, (p + 1) * LANES)
        qp = q_ref[0, :, cols].astype(BF16)
        kc = kc_ref[0, :, cols].astype(BF16)
        vc = vc_ref[0, :, cols].astype(BF16)
        kn = kn_ref[0, :, cols].astype(BF16)
        vn = vn_ref[0, :, cols].astype(BF16)
        outs = []
        for hh in range(2):
            qm = jnp.where(low_half if hh == 0 else jnp.logical_not(low_half), qp, jnp.zeros_like(qp))
            s_c = _dot_nt(qm, kc) + bc_ref[2 * p + hh]
            s_n = _dot_nt(qm, kn) + bn_ref[2 * p + hh]
            outs.append(_softmax_pv([s_c, s_n], [vc, vn]))
        o_ref[0, :, cols] = jnp.where(low_half, outs[0], outs[1]).astype(o_ref.dtype)


def _attn_sample(q, k_new, v_new, cache_k, cache_v, bias_c, bias_n):
    bsz, t_len, d = q.shape
    n_cache = cache_k.shape[1]
    new_spec = pl.BlockSpec((1, t_len, d), lambda b: (b, 0, 0))
    cache_spec = pl.BlockSpec((1, n_cache, d), lambda b: (b, 0, 0))
    return pl.pallas_call(
        _attn_sample_kernel,
        grid=(bsz,),
        in_specs=[new_spec, new_spec, new_spec, cache_spec, cache_spec,
                  pl.BlockSpec(bias_c.shape, lambda b: (0, 0, 0)),
                  pl.BlockSpec(bias_n.shape, lambda b: (0, 0, 0))],
        out_specs=new_spec,
        out_shape=jax.ShapeDtypeStruct((bsz, t_len, d), BF16),
        compiler_params=_params("parallel"),
        name="attn_sample",
    )(q, k_new, v_new, cache_k, cache_v, bias_c, bias_n)


def _log_sigmoid(z):
    return jnp.minimum(z, 0.0) - jnp.log(1.0 + jnp.exp(-jnp.abs(z)))


def _gla_proj_kernel(x_ref, w_ref, wlow_ref, wup_ref, bgk_ref, q_ref, k_ref, v_ref, r_ref, g_ref):
    xb = x_ref[...].astype(BF16)
    n_chunk = 512
    q_ref[...] = _dot(xb, w_ref[:, 0:B_QK]) * (B_KEY_DIM ** -0.5)
    k_ref[...] = _dot(xb, w_ref[:, B_QK:2 * B_QK])
    for c in range(0, B_VD, n_chunk):
        v_ref[:, c:c + n_chunk] = _dot(xb, w_ref[:, 2 * B_QK + c:2 * B_QK + c + n_chunk])
        r_ref[:, c:c + n_chunk] = _dot(xb, w_ref[:, 2 * B_QK + B_VD + c:2 * B_QK + B_VD + c + n_chunk])
    low = _dot(xb, wlow_ref[...])
    z = _dot(low.astype(BF16), wup_ref[...]) + bgk_ref[...]
    g_ref[...] = _log_sigmoid(z) / B_GATE_NORM


def _gla_proj(x, w_main, w_low, w_up, b_gk, tm=512, name="gla_proj"):
    m, d = x.shape
    tm = _row_tile(m, tm)
    widths = (B_QK, B_QK, B_VD, B_VD, B_QK)
    const = lambda i: (0, 0)
    return pl.pallas_call(
        _gla_proj_kernel,
        grid=(m // tm,),
        in_specs=[pl.BlockSpec((tm, d), lambda i: (i, 0)),
                  pl.BlockSpec(w_main.shape, const),
                  pl.BlockSpec(w_low.shape, const),
                  pl.BlockSpec(w_up.shape, const),
                  pl.BlockSpec((1, B_QK), const)],
        out_specs=[pl.BlockSpec((tm, wd), lambda i: (i, 0)) for wd in widths],
        out_shape=[jax.ShapeDtypeStruct((m, wd), F32) for wd in widths],
        compiler_params=_params("parallel"),
        name=name,
    )(x, w_main, w_low, w_up, b_gk.reshape(1, B_QK))


def _split3(x):
    hi = x.astype(BF16)
    r1 = x - hi.astype(F32)
    mid = r1.astype(BF16)
    lo = (r1 - mid.astype(F32)).astype(BF16)
    return hi, mid, lo


def _gla_kernel(q_ref, k_ref, g_ref, v_ref, r_ref, s0_ref, gain_ref, a_ref, st_ref, *, chunk, n_chunks):
    nsb = chunk // SUB_BLOCK

    @pl.when(pl.program_id(1) == 0)
    def _():
        st_ref[...] = s0_ref[...]

    row = lax.broadcasted_iota(jnp.int32, (chunk, chunk), 0)
    col = lax.broadcasted_iota(jnp.int32, (chunk, chunk), 1)
    blk_start = (row // SUB_BLOCK) * SUB_BLOCK
    tri_local = jnp.where((col <= row) & (col >= blk_start), 1.0, 0.0).astype(BF16)
    tri_before = jnp.where(col < blk_start, 1.0, 0.0).astype(BF16)
    rcol = lax.broadcasted_iota(jnp.int32, (SUB_BLOCK, chunk), 1)
    rrow = lax.broadcasted_iota(jnp.int32, (SUB_BLOCK, chunk), 0)

    def chunk_body(c, carry):
        row0 = pl.multiple_of(c * chunk, chunk)
        rows = pl.ds(row0, chunk)
        q = q_ref[0, rows, :]
        k = k_ref[0, rows, :]
        g_parts = _split3(g_ref[0, rows, :])
        bl = sum(_dot(tri_local, gp) for gp in g_parts)
        if nsb > 1:
            rr = sum(_dot(tri_before, gp) for gp in g_parts)
            b = bl + rr
        else:
            rr = None
            b = bl
        b_last = b[chunk - 1:chunk, :]
        qt = q * jnp.exp(bl)
        qhat = qt * jnp.exp(rr) if nsb > 1 else qt
        kdec = (k * jnp.exp(b_last - b)).astype(BF16)
        decay = jnp.exp(b_last)
        kb = k.astype(BF16)

        for h in range(B_HEADS):
            kcols = slice(h * B_KEY_DIM, (h + 1) * B_KEY_DIM)
            vcols = slice(h * B_VAL_DIM, (h + 1) * B_VAL_DIM)
            q_h, k_h, bl_h, b_h, qt_h = q[:, kcols], k[:, kcols], bl[:, kcols], b[:, kcols], qt[:, kcols]
            kb_h = kb[:, kcols]
            a_rows = []
            for i in range(nsb):
                sb = slice(i * SUB_BLOCK, (i + 1) * SUB_BLOCK)
                q_s, bl_s = q_h[sb], bl_h[sb]
                y = jnp.concatenate(
                    [q_s * jnp.exp(jnp.minimum(bl_s - bl_s[j:j + 1], 0.0)) for j in range(SUB_BLOCK)], axis=0)
                res = _dot_nt(y.astype(BF16), kb_h)
                a_i = jnp.zeros((SUB_BLOCK, chunk), F32)
                for j in range(SUB_BLOCK):
                    a_i = jnp.where(rcol == i * SUB_BLOCK + j, res[j * SUB_BLOCK:(j + 1) * SUB_BLOCK], a_i)
                a_i = jnp.where(rcol <= i * SUB_BLOCK + rrow, a_i, 0.0)
                if i > 0:
                    r_i = rr[i * SUB_BLOCK:i * SUB_BLOCK + 1, kcols]
                    kt = (k_h * jnp.exp(jnp.minimum(r_i - b_h, 0.0))).astype(BF16)
                    a_off = _dot_nt(qt_h[sb].astype(BF16), kt)
                    a_i = jnp.where(rcol < i * SUB_BLOCK, a_off, a_i)
                a_rows.append(a_i)
            a_mat = (jnp.concatenate(a_rows, axis=0) if nsb > 1 else a_rows[0]).astype(BF16)
            v_h = v_ref[0, rows, vcols].astype(BF16)
            st = st_ref[0, h]
            o = _dot(a_mat, v_h) + _dot_nt(qhat[:, kcols].astype(BF16), st.astype(BF16))
            st_ref[0, h] = st * decay[:, kcols] + _dot_tn(v_h, kdec[:, kcols])
            o = o * lax.rsqrt(jnp.mean(o * o, axis=-1, keepdims=True) + GN_EPS) * gain_ref[:, vcols]
            r_h = r_ref[0, rows, vcols]
            a_ref[0, rows, vcols] = (o * (r_h * jax.nn.sigmoid(r_h))).astype(a_ref.dtype)
        return carry

    lax.fori_loop(0, n_chunks, chunk_body, 0)


def _gla(q, k, g, v, r, s0t, gain, chunk, chunks_per_step, name="gla"):
    bsz, t_len, _ = q.shape
    rows = chunk * chunks_per_step
    assert t_len % rows == 0
    qk_spec = pl.BlockSpec((1, rows, B_QK), lambda b, s: (b, s, 0))
    v_spec = pl.BlockSpec((1, rows, B_VD), lambda b, s: (b, s, 0))
    st_spec = pl.BlockSpec((1, B_HEADS, B_VAL_DIM, B_KEY_DIM), lambda b, s: (b, 0, 0, 0))
    return pl.pallas_call(
        functools.partial(_gla_kernel, chunk=chunk, n_chunks=chunks_per_step),
        grid=(bsz, t_len // rows),
        in_specs=[qk_spec, qk_spec, qk_spec, v_spec, v_spec, st_spec,
                  pl.BlockSpec((1, B_VD), lambda b, s: (0, 0))],
        out_specs=[v_spec, st_spec],
        out_shape=[jax.ShapeDtypeStruct((bsz, t_len, B_VD), BF16),
                   jax.ShapeDtypeStruct(s0t.shape, F32)],
        compiler_params=_params("parallel", "arbitrary"),
        name=name,
    )(q, k, g, v, r, s0t, gain.reshape(1, B_VD))


def _rel_bias(rel_table, q_pos, k_pos):
    d = jnp.clip(q_pos[:, None] - k_pos[None, :], -REL_CLIP, REL_CLIP) + REL_CLIP
    return jnp.transpose(rel_table[d], (2, 0, 1)).astype(F32)


def kernel(x_prompt, x_sample, cache_a_k, cache_a_v, state_b, w_in_a, rel_bias_a, w_out_a, w_in_b, w_gk_up_b,
           b_gk_b, gn_gain_b, w_out_b, w_ffn_in, w_ffn_out, ln1_g, ln1_b, ln2_g, ln2_b):
    bsz, t_len, d = x_prompt.shape
    dbsz, dt_len, _ = x_sample.shape
    n_cache = cache_a_k.shape[2]
    keep = min(BAND_PAST, t_len)
    xp = x_prompt.reshape(bsz * t_len, d)
    xs = x_sample.reshape(dbsz * dt_len, d)

    qscale = jnp.concatenate([jnp.full((d,), A_HEAD_DIM ** -0.5, F32), jnp.ones((2 * d,), F32)])
    w_qkv = (w_in_a[0] * qscale).astype(BF16)
    w_oa = w_out_a[0].astype(BF16)
    offs_k = jnp.arange(BAND_PAST + CHUNK) - BAND_PAST
    bias_p = _rel_bias(rel_bias_a[0], jnp.arange(CHUNK), offs_k)
    bias_s = _rel_bias(rel_bias_a[0], n_cache + jnp.arange(dt_len), jnp.arange(n_cache + dt_len))

    q, k, v = _linear(xp, w_qkv, (d, d, d), BF16, name="qkv_prompt")
    x_tail = x_prompt[:, t_len - keep:].reshape(bsz * keep, d)
    k_tail, v_tail = _linear(x_tail, w_qkv[:, d:], (d, d), F32, name="kv_tail")
    att = _attn_prompt(q.reshape(bsz, t_len, d), k.reshape(bsz, t_len, d), v.reshape(bsz, t_len, d), bias_p)
    xp = _proj_ln(att.reshape(bsz * t_len, d), w_oa, xp, ln1_g[0], ln1_b[0], name="attn_out_prompt")

    qs, ks, vs = _linear(xs, w_qkv, (d, d, d), F32, tm=128, name="qkv_sample")
    att_s = _attn_sample(qs.reshape(dbsz, dt_len, d), ks.reshape(dbsz, dt_len, d), vs.reshape(dbsz, dt_len, d),
                         cache_a_k[0].reshape(dbsz, n_cache, d), cache_a_v[0].reshape(dbsz, n_cache, d),
                         bias_s[:, :, :n_cache], bias_s[:, :, n_cache:])
    xs = _proj_ln(att_s.reshape(dbsz * dt_len, d), w_oa, xs, ln1_g[0], ln1_b[0], tm=128, name="attn_out_sample")

    wi0, wo0 = w_ffn_in[0].astype(BF16), w_ffn_out[0].astype(BF16)
    xp = _ffn_ln(xp, wi0, wo0, ln2_g[0], ln2_b[0], name="ffn0_prompt")
    xs = _ffn_ln(xs, wi0, wo0, ln2_g[0], ln2_b[0], tm=128, name="ffn0_sample")

    n_main = 2 * B_QK + 2 * B_VD
    w_main = w_in_b[0][:, :n_main].astype(BF16)
    w_low = jnp.pad(w_in_b[0][:, n_main:], ((0, 0), (0, LANES - B_GATE_RANK))).astype(BF16)
    w_up = jnp.pad(w_gk_up_b[0], ((0, LANES - B_GATE_RANK), (0, 0))).astype(BF16)
    w_ob = w_out_b[0].astype(BF16)

    gq, gk, gv, gr, gg = _gla_proj(xp, w_main, w_low, w_up, b_gk_b[0], name="gla_proj_prompt")
    s0 = jnp.zeros((bsz, B_HEADS, B_VAL_DIM, B_KEY_DIM), F32)
    sh = lambda a: a.reshape(bsz, t_len, a.shape[-1])
    a_p, st_p = _gla(sh(gq), sh(gk), sh(gg), sh(gv), sh(gr), s0, gn_gain_b[0], CHUNK, 4, name="gla_prompt")
    xp = _proj_ln(a_p.reshape(bsz * t_len, B_VD), w_ob, xp, ln1_g[1], ln1_b[1], name="gla_out_prompt")

    sq, sk, sv, sr, sg = _gla_proj(xs, w_main, w_low, w_up, b_gk_b[0], tm=128, name="gla_proj_sample")
    shs = lambda a: a.reshape(dbsz, dt_len, a.shape[-1])
    a_s, st_s = _gla(shs(sq), shs(sk), shs(sg), shs(sv), shs(sr), jnp.swapaxes(state_b[0], -1, -2),
                     gn_gain_b[0], dt_len, 1, name="gla_sample")
    xs = _proj_ln(a_s.reshape(dbsz * dt_len, B_VD), w_ob, xs, ln1_g[1], ln1_b[1], tm=128, name="gla_out_sample")

    wi1, wo1 = w_ffn_in[1].astype(BF16), w_ffn_out[1].astype(BF16)
    xp = _ffn_ln(xp, wi1, wo1, ln2_g[1], ln2_b[1], name="ffn1_prompt")
    xs = _ffn_ln(xs, wi1, wo1, ln2_g[1], ln2_b[1], tm=128, name="ffn1_sample")

    heads = lambda a, n, t: a.reshape(1, n, t, A_HEADS, A_HEAD_DIM)
    return (xp.reshape(bsz, t_len, d), xs.reshape(dbsz, dt_len, d),
            heads(k_tail, bsz, keep), heads(v_tail, bsz, keep), jnp.swapaxes(st_p, -1, -2)[None],
            heads(ks, dbsz, dt_len), heads(vs, dbsz, dt_len), jnp.swapaxes(st_s, -1, -2)[None])
```

```python
import functools

import jax
import jax.numpy as jnp
from jax import lax
from jax.experimental import pallas as pl
from jax.experimental.pallas import tpu as pltpu

F32 = jnp.float32
BF16 = jnp.bfloat16

D_MODEL = 1024
CHUNK = 64
A_HEADS = 16
A_HEAD_DIM = D_MODEL // A_HEADS
BAND_CHUNKS = 8
BAND_PAST = BAND_CHUNKS * CHUNK
REL_CLIP = 128
B_HEADS = 4
B_KEY_DIM = D_MODEL // 2 // B_HEADS
B_VAL_DIM = D_MODEL // B_HEADS
B_QK = B_HEADS * B_KEY_DIM
B_VD = B_HEADS * B_VAL_DIM
B_GATE_RANK = 16
B_GATE_NORM = 16.0
D_FF = -(-8 * D_MODEL // 768) * 256
DEPTH = 2
ALPHA = (2.0 * DEPTH) ** 0.25
LN_EPS = 1e-5
GN_EPS = 1e-6
NEG_INF = -1e30

LANES = 128
SUB_BLOCK = 16
ATT_TILE = 4 * CHUNK
VMEM_LIMIT = 56 * 1024 * 1024


def _dot(a, b):
    return jnp.dot(a, b, preferred_element_type=F32)


def _dot_nt(a, b):
    return lax.dot_general(a, b, (((1,), (1,)), ((), ())), preferred_element_type=F32)


def _dot_tn(a, b):
    return lax.dot_general(a, b, (((0,), (0,)), ((), ())), preferred_element_type=F32)


def _params(*sem):
    return pltpu.CompilerParams(dimension_semantics=sem, vmem_limit_bytes=VMEM_LIMIT)


def _row_tile(m, pref):
    t = min(m, pref)
    assert m % t == 0
    return t


def _linear_kernel(x_ref, w_ref, *o_refs, n_chunk):
    xb = x_ref[...].astype(BF16)
    col = 0
    for o_ref in o_refs:
        nj = o_ref.shape[1]
        for c in range(0, nj, n_chunk):
            o_ref[:, c:c + n_chunk] = _dot(xb, w_ref[:, col + c:col + c + n_chunk]).astype(o_ref.dtype)
        col += nj


def _linear(x, w, widths, dtype, tm=512, name="linear"):
    m, k = x.shape
    tm = _row_tile(m, tm)
    n_chunk = 512
    assert all(wd % n_chunk == 0 for wd in widths) and sum(widths) == w.shape[1]
    return pl.pallas_call(
        functools.partial(_linear_kernel, n_chunk=n_chunk),
        grid=(m // tm,),
        in_specs=[pl.BlockSpec((tm, k), lambda i: (i, 0)),
                  pl.BlockSpec(w.shape, lambda i: (0, 0))],
        out_specs=[pl.BlockSpec((tm, wd), lambda i: (i, 0)) for wd in widths],
        out_shape=[jax.ShapeDtypeStruct((m, wd), dtype) for wd in widths],
        compiler_params=_params("parallel"),
        name=name,
    )(x, w)


def _res_ln(x, y, g, b):
    t = ALPHA * x + y
    mu = jnp.mean(t, axis=-1, keepdims=True)
    d = t - mu
    var = jnp.mean(d * d, axis=-1, keepdims=True)
    return d * lax.rsqrt(var + LN_EPS) * g + b


def _proj_ln_kernel(a_ref, w_ref, x_ref, g_ref, b_ref, o_ref):
    y = _dot(a_ref[...], w_ref[...])
    o_ref[...] = _res_ln(x_ref[...], y, g_ref[...], b_ref[...])


def _proj_ln(a, w, x, g, b, tm=512, name="proj_ln"):
    m, k = a.shape
    d = w.shape[1]
    tm = _row_tile(m, tm)
    return pl.pallas_call(
        _proj_ln_kernel,
        grid=(m // tm,),
        in_specs=[pl.BlockSpec((tm, k), lambda i: (i, 0)),
                  pl.BlockSpec((k, d), lambda i: (0, 0)),
                  pl.BlockSpec((tm, d), lambda i: (i, 0)),
                  pl.BlockSpec((1, d), lambda i: (0, 0)),
                  pl.BlockSpec((1, d), lambda i: (0, 0))],
        out_specs=pl.BlockSpec((tm, d), lambda i: (i, 0)),
        out_shape=jax.ShapeDtypeStruct((m, d), F32),
        compiler_params=_params("parallel"),
        name=name,
    )(a, w, x, g.reshape(1, d), b.reshape(1, d))


def _ffn_kernel(x_ref, wi_ref, wo_ref, g_ref, b_ref, o_ref, h_ref, *, ff_chunk):
    x = x_ref[...]
    xb = x.astype(BF16)
    d_ff = h_ref.shape[1]
    for c in range(0, d_ff, ff_chunk):
        gate = _dot(xb, wi_ref[:, c:c + ff_chunk])
        up = _dot(xb, wi_ref[:, d_ff + c:d_ff + c + ff_chunk])
        h_ref[:, c:c + ff_chunk] = (gate * jax.nn.sigmoid(gate) * up).astype(BF16)
    y = _dot(h_ref[...], wo_ref[...])
    o_ref[...] = _res_ln(x, y, g_ref[...], b_ref[...])


def _ffn_ln(x, wi, wo, g, b, tm=512, name="ffn_ln"):
    m, d = x.shape
    d_ff = wo.shape[0]
    tm = _row_tile(m, tm)
    ff_chunk = 256
    assert d_ff % ff_chunk == 0
    return pl.pallas_call(
        functools.partial(_ffn_kernel, ff_chunk=ff_chunk),
        grid=(m // tm,),
        in_specs=[pl.BlockSpec((tm, d), lambda i: (i, 0)),
                  pl.BlockSpec(wi.shape, lambda i: (0, 0), pipeline_mode=pl.Buffered(1)),
                  pl.BlockSpec(wo.shape, lambda i: (0, 0), pipeline_mode=pl.Buffered(1)),
                  pl.BlockSpec((1, d), lambda i: (0, 0)),
                  pl.BlockSpec((1, d), lambda i: (0, 0))],
        out_specs=pl.BlockSpec((tm, d), lambda i: (i, 0)),
        out_shape=jax.ShapeDtypeStruct((m, d), F32),
        scratch_shapes=[pltpu.VMEM((tm, d_ff), BF16)],
        compiler_params=_params("parallel"),
        name=name,
    )(x, wi, wo, g.reshape(1, d), b.reshape(1, d))


def _softmax_pv(s_parts, v_parts):
    m = s_parts[0].max(axis=-1, keepdims=True)
    for s in s_parts[1:]:
        m = jnp.maximum(m, s.max(axis=-1, keepdims=True))
    acc, l = None, None
    for s, v in zip(s_parts, v_parts):
        e = jnp.exp(s - m)
        ls = e.sum(axis=-1, keepdims=True)
        pv = _dot(e.astype(BF16), v)
        acc = pv if acc is None else acc + pv
        l = ls if l is None else l + ls
    return acc / l


def _attn_prompt_kernel(q_ref, ka_ref, kb_ref, kc_ref, va_ref, vb_ref, vc_ref, bias_ref, o_ref):
    lane = lax.broadcasted_iota(jnp.int32, (ATT_TILE, LANES), 1)
    low_half = lane < A_HEAD_DIM
    k_refs, v_refs = (ka_ref, kb_ref, kc_ref), (va_ref, vb_ref, vc_ref)
    for p in range(D_MODEL // LANES):
        cols = slice(p * LANES, (p + 1) * LANES)
        qp = q_ref[0, :, cols]
        zero = jnp.zeros_like(qp)
        q_pair = jnp.concatenate([jnp.where(low_half, qp, zero), jnp.where(low_half, zero, qp)], axis=0)
        s_parts = [_dot_nt(q_pair, k_ref[0, :, cols]) + bias_ref[0, p, :, j * ATT_TILE:(j + 1) * ATT_TILE]
                   for j, k_ref in enumerate(k_refs)]
        pv = _softmax_pv(s_parts, [v_ref[0, :, cols] for v_ref in v_refs])
        o_ref[0, :, cols] = jnp.where(low_half, pv[:ATT_TILE], pv[ATT_TILE:]).astype(o_ref.dtype)


def _band_bias(rel_table):
    n_heads = rel_table.shape[1]
    win = 3 * ATT_TILE
    t = rel_table.T.astype(F32)
    w = jnp.concatenate([jnp.broadcast_to(t[:, 2 * REL_CLIP:], (n_heads, win - REL_CLIP)),
                         jnp.flip(t[:, 1:2 * REL_CLIP], axis=1),
                         jnp.broadcast_to(t[:, :1], (n_heads, REL_CLIP))], axis=1)
    length = ATT_TILE + win - 1
    assert w.shape[1] == length
    z = jnp.pad(w, ((0, 0), (0, 1)))
    skew = jnp.tile(z, (1, ATT_TILE))[:, :ATT_TILE * length].reshape(n_heads, ATT_TILE, length)
    toeplitz = skew[:, :, ATT_TILE - 1:ATT_TILE - 1 + win]
    qc = lax.broadcasted_iota(jnp.int32, (ATT_TILE, win), 0) // CHUNK
    kc = lax.broadcasted_iota(jnp.int32, (ATT_TILE, win), 1) // CHUNK
    in_band = (kc >= qc) & (kc <= qc + BAND_CHUNKS)
    variants = []
    for v in range(3):
        valid = in_band & (kc >= BAND_CHUNKS - v * (ATT_TILE // CHUNK))
        variants.append(jnp.where(valid[None], toeplitz, NEG_INF))
    return jnp.stack(variants).reshape(3, n_heads // 2, 2 * ATT_TILE, win)


def _attn_prompt(q, k, v, bias):
    bsz, t_len, d = q.shape
    assert t_len % ATT_TILE == 0 and BAND_PAST == 2 * ATT_TILE
    blk = (1, ATT_TILE, d)

    def past(n):
        return lambda b, i: (b, jnp.maximum(i - n, 0), 0)

    kv_specs = [pl.BlockSpec(blk, past(2)), pl.BlockSpec(blk, past(1)), pl.BlockSpec(blk, past(0))]
    bias_spec = pl.BlockSpec((1,) + bias.shape[1:], lambda b, i: (jnp.minimum(i, 2), 0, 0, 0),
                             pipeline_mode=pl.Buffered(1))
    return pl.pallas_call(
        _attn_prompt_kernel,
        grid=(bsz, t_len // ATT_TILE),
        in_specs=[pl.BlockSpec(blk, lambda b, i: (b, i, 0))] + kv_specs + kv_specs + [bias_spec],
        out_specs=pl.BlockSpec(blk, lambda b, i: (b, i, 0)),
        out_shape=jax.ShapeDtypeStruct((bsz, t_len, d), BF16),
        compiler_params=_params("parallel", "arbitrary"),
        name="attn_prompt",
    )(q, k, k, k, v, v, v, bias)


def _attn_sample_kernel(q_ref, kn_ref, vn_ref, kc_ref, vc_ref, bc_ref, bn_ref, o_ref):
    t_len = q_ref.shape[1]
    lane = lax.broadcasted_iota(jnp.int32, (t_len, LANES), 1)
    low_half = lane < A_HEAD_DIM
    for p in range(D_MODEL // LANES):
        cols = slice(p * LANES, (p + 1) * LANES)
        qp = q_ref[0, :, cols].astype(BF16)
        kc = kc_ref[0, :, cols].astype(BF16)
        vc = vc_ref[0, :, cols].astype(BF16)
        kn = kn_ref[0, :, cols].astype(BF16)
        vn = vn_ref[0, :, cols].astype(BF16)
        outs = []
        for hh in range(2):
            qm = jnp.where(low_half if hh == 0 else jnp.logical_not(low_half), qp, jnp.zeros_like(qp))
            s_c = _dot_nt(qm, kc) + bc_ref[2 * p + hh]
            s_n = _dot_nt(qm, kn) + bn_ref[2 * p + hh]
            outs.append(_softmax_pv([s_c, s_n], [vc, vn]))
        o_ref[0, :, cols] = jnp.where(low_half, outs[0], outs[1]).astype(o_ref.dtype)


def _attn_sample(q, k_new, v_new, cache_k, cache_v, bias_c, bias_n):
    bsz, t_len, d = q.shape
    n_cache = cache_k.shape[1]
    new_spec = pl.BlockSpec((1, t_len, d), lambda b: (b, 0, 0))
    cache_spec = pl.BlockSpec((1, n_cache, d), lambda b: (b, 0, 0))
    return pl.pallas_call(
        _attn_sample_kernel,
        grid=(bsz,),
        in_specs=[new_spec, new_spec, new_spec, cache_spec, cache_spec,
                  pl.BlockSpec(bias_c.shape, lambda b: (0, 0, 0)),
                  pl.BlockSpec(bias_n.shape, lambda b: (0, 0, 0))],
        out_specs=new_spec,
        out_shape=jax.ShapeDtypeStruct((bsz, t_len, d), BF16),
        compiler_params=_params("parallel"),
        name="attn_sample",
    )(q, k_new, v_new, cache_k, cache_v, bias_c, bias_n)


def _log_sigmoid(z):
    return jnp.minimum(z, 0.0) - jnp.log(1.0 + jnp.exp(-jnp.abs(z)))


def _gla_proj_kernel(x_ref, w_ref, wlow_ref, wup_ref, bgk_ref, q_ref, k_ref, v_ref, r_ref, g_ref):
    xb = x_ref[...].astype(BF16)
    n_chunk = 512
    q_ref[...] = _dot(xb, w_ref[:, 0:B_QK]) * (B_KEY_DIM ** -0.5)
    k_ref[...] = _dot(xb, w_ref[:, B_QK:2 * B_QK])
    for c in range(0, B_VD, n_chunk):
        v_ref[:, c:c + n_chunk] = _dot(xb, w_ref[:, 2 * B_QK + c:2 * B_QK + c + n_chunk])
        r_ref[:, c:c + n_chunk] = _dot(xb, w_ref[:, 2 * B_QK + B_VD + c:2 * B_QK + B_VD + c + n_chunk])
    low = _dot(xb, wlow_ref[...])
    z = _dot(low.astype(BF16), wup_ref[...]) + bgk_ref[...]
    g_ref[...] = _log_sigmoid(z) / B_GATE_NORM


def _gla_proj(x, w_main, w_low, w_up, b_gk, tm=512, name="gla_proj"):
    m, d = x.shape
    tm = _row_tile(m, tm)
    widths = (B_QK, B_QK, B_VD, B_VD, B_QK)
    const = lambda i: (0, 0)
    return pl.pallas_call(
        _gla_proj_kernel,
        grid=(m // tm,),
        in_specs=[pl.BlockSpec((tm, d), lambda i: (i, 0)),
                  pl.BlockSpec(w_main.shape, const),
                  pl.BlockSpec(w_low.shape, const),
                  pl.BlockSpec(w_up.shape, const),
                  pl.BlockSpec((1, B_QK), const)],
        out_specs=[pl.BlockSpec((tm, wd), lambda i: (i, 0)) for wd in widths],
        out_shape=[jax.ShapeDtypeStruct((m, wd), F32) for wd in widths],
        compiler_params=_params("parallel"),
        name=name,
    )(x, w_main, w_low, w_up, b_gk.reshape(1, B_QK))


def _split3(x):
    hi = x.astype(BF16)
    r1 = x - hi.astype(F32)
    mid = r1.astype(BF16)
    lo = (r1 - mid.astype(F32)).astype(BF16)
    return hi, mid, lo


def _gla_kernel(q_ref, k_ref, g_ref, v_ref, r_ref, s0_ref, gain_ref, a_ref, st_ref, *, chunk, n_chunks):
    nsb = chunk // SUB_BLOCK

    @pl.when(pl.program_id(1) == 0)
    def _():
        st_ref[...] = s0_ref[...]

    row = lax.broadcasted_iota(jnp.int32, (chunk, chunk), 0)
    col = lax.broadcasted_iota(jnp.int32, (chunk, chunk), 1)
    blk_start = (row // SUB_BLOCK) * SUB_BLOCK
    tri_local = jnp.where((col <= row) & (col >= blk_start), 1.0, 0.0).astype(BF16)
    tri_before = jnp.where(col < blk_start, 1.0, 0.0).astype(BF16)
    rcol = lax.broadcasted_iota(jnp.int32, (SUB_BLOCK, chunk), 1)
    rrow = lax.broadcasted_iota(jnp.int32, (SUB_BLOCK, chunk), 0)

    def chunk_body(c, carry):
        row0 = pl.multiple_of(c * chunk, chunk)
        rows = pl.ds(row0, chunk)
        q = q_ref[0, rows, :]
        k = k_ref[0, rows, :]
        g_parts = _split3(g_ref[0, rows, :])
        bl = sum(_dot(tri_local, gp) for gp in g_parts)
        if nsb > 1:
            rr = sum(_dot(tri_before, gp) for gp in g_parts)
            b = bl + rr
        else:
            rr = None
            b = bl
        b_last = b[chunk - 1:chunk, :]
        qt = q * jnp.exp(bl)
        qhat = qt * jnp.exp(rr) if nsb > 1 else qt
        kdec = (k * jnp.exp(b_last - b)).astype(BF16)
        decay = jnp.exp(b_last)
        kb = k.astype(BF16)

        for h in range(B_HEADS):
            kcols = slice(h * B_KEY_DIM, (h + 1) * B_KEY_DIM)
            vcols = slice(h * B_VAL_DIM, (h + 1) * B_VAL_DIM)
            q_h, k_h, bl_h, b_h, qt_h = q[:, kcols], k[:, kcols], bl[:, kcols], b[:, kcols], qt[:, kcols]
            kb_h = kb[:, kcols]
            a_rows = []
            for i in range(nsb):
                sb = slice(i * SUB_BLOCK, (i + 1) * SUB_BLOCK)
                q_s, bl_s = q_h[sb], bl_h[sb]
                y = jnp.concatenate(
                    [q_s * jnp.exp(jnp.minimum(bl_s - bl_s[j:j + 1], 0.0)) for j in range(SUB_BLOCK)], axis=0)
                res = _dot_nt(y.astype(BF16), kb_h)
                a_i = jnp.zeros((SUB_BLOCK, chunk), F32)
                for j in range(SUB_BLOCK):
                    a_i = jnp.where(rcol == i * SUB_BLOCK + j, res[j * SUB_BLOCK:(j + 1) * SUB_BLOCK], a_i)
                a_i = jnp.where(rcol <= i * SUB_BLOCK + rrow, a_i, 0.0)
                if i > 0:
                    r_i = rr[i * SUB_BLOCK:i * SUB_BLOCK + 1, kcols]
                    kt = (k_h * jnp.exp(jnp.minimum(r_i - b_h, 0.0))).astype(BF16)
                    a_off = _dot_nt(qt_h[sb].astype(BF16), kt)
                    a_i = jnp.where(rcol < i * SUB_BLOCK, a_off, a_i)
                a_rows.append(a_i)
            a_mat = (jnp.concatenate(a_rows, axis=0) if nsb > 1 else a_rows[0]).astype(BF16)
            v_h = v_ref[0, rows, vcols].astype(BF16)
            st = st_ref[0, h]
            o = _dot(a_mat, v_h) + _dot_nt(qhat[:, kcols].astype(BF16), st.astype(BF16))
            st_ref[0, h] = st * decay[:, kcols] + _dot_tn(v_h, kdec[:, kcols])
            o = o * lax.rsqrt(jnp.mean(o * o, axis=-1, keepdims=True) + GN_EPS) * gain_ref[:, vcols]
            r_h = r_ref[0, rows, vcols]
            a_ref[0, rows, vcols] = (o * (r_h * jax.nn.sigmoid(r_h))).astype(a_ref.dtype)
        return carry

    lax.fori_loop(0, n_chunks, chunk_body, 0)


def _gla(q, k, g, v, r, s0t, gain, chunk, chunks_per_step, name="gla"):
    bsz, t_len, _ = q.shape
    rows = chunk * chunks_per_step
    assert t_len % rows == 0
    qk_spec = pl.BlockSpec((1, rows, B_QK), lambda b, s: (b, s, 0))
    v_spec = pl.BlockSpec((1, rows, B_VD), lambda b, s: (b, s, 0))
    st_spec = pl.BlockSpec((1, B_HEADS, B_VAL_DIM, B_KEY_DIM), lambda b, s: (b, 0, 0, 0))
    return pl.pallas_call(
        functools.partial(_gla_kernel, chunk=chunk, n_chunks=chunks_per_step),
        grid=(bsz, t_len // rows),
        in_specs=[qk_spec, qk_spec, qk_spec, v_spec, v_spec, st_spec,
                  pl.BlockSpec((1, B_VD), lambda b, s: (0, 0))],
        out_specs=[v_spec, st_spec],
        out_shape=[jax.ShapeDtypeStruct((bsz, t_len, B_VD), BF16),
                   jax.ShapeDtypeStruct(s0t.shape, F32)],
        compiler_params=_params("parallel", "arbitrary"),
        name=name,
    )(q, k, g, v, r, s0t, gain.reshape(1, B_VD))


def _rel_bias(rel_table, q_pos, k_pos):
    d = jnp.clip(q_pos[:, None] - k_pos[None, :], -REL_CLIP, REL_CLIP) + REL_CLIP
    return jnp.transpose(rel_table[d], (2, 0, 1)).astype(F32)


def kernel(x_prompt, x_sample, cache_a_k, cache_a_v, state_b, w_in_a, rel_bias_a, w_out_a, w_in_b, w_gk_up_b,
           b_gk_b, gn_gain_b, w_out_b, w_ffn_in, w_ffn_out, ln1_g, ln1_b, ln2_g, ln2_b):
    bsz, t_len, d = x_prompt.shape
    dbsz, dt_len, _ = x_sample.shape
    n_cache = cache_a_k.shape[2]
    keep = min(BAND_PAST, t_len)
    xp = x_prompt.reshape(bsz * t_len, d)
    xs = x_sample.reshape(dbsz * dt_len, d)

    qscale = jnp.concatenate([jnp.full((d,), A_HEAD_DIM ** -0.5, F32), jnp.ones((2 * d,), F32)])
    w_qkv = (w_in_a[0] * qscale).astype(BF16)
    w_oa = w_out_a[0].astype(BF16)
    bias_p = _band_bias(rel_bias_a[0])
    bias_s = _rel_bias(rel_bias_a[0], n_cache + jnp.arange(dt_len), jnp.arange(n_cache + dt_len))

    q, k, v = _linear(xp, w_qkv, (d, d, d), BF16, name="qkv_prompt")
    x_tail = x_prompt[:, t_len - keep:].reshape(bsz * keep, d)
    k_tail, v_tail = _linear(x_tail, w_qkv[:, d:], (d, d), F32, name="kv_tail")
    att = _attn_prompt(q.reshape(bsz, t_len, d), k.reshape(bsz, t_len, d), v.reshape(bsz, t_len, d), bias_p)
    xp = _proj_ln(att.reshape(bsz * t_len, d), w_oa, xp, ln1_g[0], ln1_b[0], name="attn_out_prompt")

    qs, ks, vs = _linear(xs, w_qkv, (d, d, d), F32, tm=128, name="qkv_sample")
    att_s = _attn_sample(qs.reshape(dbsz, dt_len, d), ks.reshape(dbsz, dt_len, d), vs.reshape(dbsz, dt_len, d),
                         cache_a_k[0].reshape(dbsz, n_cache, d), cache_a_v[0].reshape(dbsz, n_cache, d),
                         bias_s[:, :, :n_cache], bias_s[:, :, n_cache:])
    xs = _proj_ln(att_s.reshape(dbsz * dt_len, d), w_oa, xs, ln1_g[0], ln1_b[0], tm=128, name="attn_out_sample")

    wi0, wo0 = w_ffn_in[0].astype(BF16), w_ffn_out[0].astype(BF16)
    xp = _ffn_ln(xp, wi0, wo0, ln2_g[0], ln2_b[0], name="ffn0_prompt")
    xs = _ffn_ln(xs, wi0, wo0, ln2_g[0], ln2_b[0], tm=128, name="ffn0_sample")

    n_main = 2 * B_QK + 2 * B_VD
    w_main = w_in_b[0][:, :n_main].astype(BF16)
    w_low = jnp.pad(w_in_b[0][:, n_main:], ((0, 0), (0, LANES - B_GATE_RANK))).astype(BF16)
    w_up = jnp.pad(w_gk_up_b[0], ((0, LANES - B_GATE_RANK), (0, 0))).astype(BF16)
    w_ob = w_out_b[0].astype(BF16)

    gq, gk, gv, gr, gg = _gla_proj(xp, w_main, w_low, w_up, b_gk_b[0], name="gla_proj_prompt")
    s0 = jnp.zeros((bsz, B_HEADS, B_VAL_DIM, B_KEY_DIM), F32)
    sh = lambda a: a.reshape(bsz, t_len, a.shape[-1])
    a_p, st_p = _gla(sh(gq), sh(gk), sh(gg), sh(gv), sh(gr), s0, gn_gain_b[0], CHUNK, 4, name="gla_prompt")
    xp = _proj_ln(a_p.reshape(bsz * t_len, B_VD), w_ob, xp, ln1_g[1], ln1_b[1], name="gla_out_prompt")

    sq, sk, sv, sr, sg = _gla_proj(xs, w_main, w_low, w_up, b_gk_b[0], tm=128, name="gla_proj_sample")
    shs = lambda a: a.reshape(dbsz, dt_len, a.shape[-1])
    a_s, st_s = _gla(shs(sq), shs(sk), shs(sg), shs(sv), shs(sr), jnp.swapaxes(state_b[0], -1, -2),
                     gn_gain_b[0], dt_len, 1, name="gla_sample")
    xs = _proj_ln(a_s.reshape(dbsz * dt_len, B_VD), w_ob, xs, ln1_g[1], ln1_b[1], tm=128, name="gla_out_sample")

    wi1, wo1 = w_ffn_in[1].astype(BF16), w_ffn_out[1].astype(BF16)
    xp = _ffn_ln(xp, wi1, wo1, ln2_g[1], ln2_b[1], name="ffn1_prompt")
    xs = _ffn_ln(xs, wi1, wo1, ln2_g[1], ln2_b[1], tm=128, name="ffn1_sample")

    heads = lambda a, n, t: a.reshape(1, n, t, A_HEADS, A_HEAD_DIM)
    return (xp.reshape(bsz, t_len, d), xs.reshape(dbsz, dt_len, d),
            heads(k_tail, bsz, keep), heads(v_tail, bsz, keep), jnp.swapaxes(st_p, -1, -2)[None],
            heads(ks, dbsz, dt_len), heads(vs, dbsz, dt_len), jnp.swapaxes(st_s, -1, -2)[None])
```

```python
import functools

import jax
import jax.numpy as jnp
from jax import lax
from jax.experimental import pallas as pl
from jax.experimental.pallas import tpu as pltpu

F32 = jnp.float32
BF16 = jnp.bfloat16

D_MODEL = 1024
CHUNK = 64
A_HEADS = 16
A_HEAD_DIM = D_MODEL // A_HEADS
BAND_CHUNKS = 8
BAND_PAST = BAND_CHUNKS * CHUNK
REL_CLIP = 128
B_HEADS = 4
B_KEY_DIM = D_MODEL // 2 // B_HEADS
B_VAL_DIM = D_MODEL // B_HEADS
B_QK = B_HEADS * B_KEY_DIM
B_VD = B_HEADS * B_VAL_DIM
B_GATE_RANK = 16
B_GATE_NORM = 16.0
D_FF = -(-8 * D_MODEL // 768) * 256
DEPTH = 2
ALPHA = (2.0 * DEPTH) ** 0.25
LN_EPS = 1e-5
GN_EPS = 1e-6
NEG_INF = -1e30

LANES = 128
SUB_BLOCK = 16
ATT_TILE = 4 * CHUNK
GLA_MAX_LOG_DECAY = 60.0
VMEM_LIMIT = 56 * 1024 * 1024


def _dot(a, b):
    return jnp.dot(a, b, preferred_element_type=F32)


def _dot_nt(a, b):
    return lax.dot_general(a, b, (((1,), (1,)), ((), ())), preferred_element_type=F32)


def _dot_tn(a, b):
    return lax.dot_general(a, b, (((0,), (0,)), ((), ())), preferred_element_type=F32)


def _params(*sem):
    return pltpu.CompilerParams(dimension_semantics=sem, vmem_limit_bytes=VMEM_LIMIT)


def _row_tile(m, pref):
    t = min(m, pref)
    assert m % t == 0
    return t


def _linear_kernel(x_ref, w_ref, *o_refs, n_chunk):
    xb = x_ref[...].astype(BF16)
    col = 0
    for o_ref in o_refs:
        nj = o_ref.shape[1]
        for c in range(0, nj, n_chunk):
            o_ref[:, c:c + n_chunk] = _dot(xb, w_ref[:, col + c:col + c + n_chunk]).astype(o_ref.dtype)
        col += nj


def _linear(x, w, widths, dtype, tm=512, name="linear"):
    m, k = x.shape
    tm = _row_tile(m, tm)
    n_chunk = 512
    assert all(wd % n_chunk == 0 for wd in widths) and sum(widths) == w.shape[1]
    return pl.pallas_call(
        functools.partial(_linear_kernel, n_chunk=n_chunk),
        grid=(m // tm,),
        in_specs=[pl.BlockSpec((tm, k), lambda i: (i, 0)),
                  pl.BlockSpec(w.shape, lambda i: (0, 0))],
        out_specs=[pl.BlockSpec((tm, wd), lambda i: (i, 0)) for wd in widths],
        out_shape=[jax.ShapeDtypeStruct((m, wd), dtype) for wd in widths],
        compiler_params=_params("parallel"),
        name=name,
    )(x, w)


def _res_ln(x, y, g, b):
    t = ALPHA * x + y
    mu = jnp.mean(t, axis=-1, keepdims=True)
    d = t - mu
    var = jnp.mean(d * d, axis=-1, keepdims=True)
    return d * lax.rsqrt(var + LN_EPS) * g + b


def _layer_tail_kernel(a_ref, wp_ref, x_ref, g1_ref, b1_ref, wi_ref, wo_ref, g2_ref, b2_ref, o_ref, h_ref,
                       *, ff_chunk):
    x1 = _res_ln(x_ref[...], _dot(a_ref[...], wp_ref[...]), g1_ref[...], b1_ref[...])
    xb = x1.astype(BF16)
    d_ff = h_ref.shape[1]
    for c in range(0, d_ff, ff_chunk):
        gate = _dot(xb, wi_ref[:, c:c + ff_chunk])
        up = _dot(xb, wi_ref[:, d_ff + c:d_ff + c + ff_chunk])
        h_ref[:, c:c + ff_chunk] = (gate * jax.nn.sigmoid(gate) * up).astype(BF16)
    y = _dot(h_ref[...], wo_ref[...])
    o_ref[...] = _res_ln(x1, y, g2_ref[...], b2_ref[...])


def _layer_tail(a, wp, x, ln1, wi, wo, ln2, tm=512, name="layer_tail"):
    m, k = a.shape
    d = x.shape[1]
    d_ff = wo.shape[0]
    tm = _row_tile(m, tm)
    ff_chunk = 256
    assert d_ff % ff_chunk == 0
    rows = lambda width: pl.BlockSpec((tm, width), lambda i: (i, 0))
    resident = lambda w: pl.BlockSpec(w.shape, lambda i: (0, 0), pipeline_mode=pl.Buffered(1))
    vec = pl.BlockSpec((1, d), lambda i: (0, 0))
    return pl.pallas_call(
        functools.partial(_layer_tail_kernel, ff_chunk=ff_chunk),
        grid=(m // tm,),
        in_specs=[rows(k), resident(wp), rows(d), vec, vec, resident(wi), resident(wo), vec, vec],
        out_specs=rows(d),
        out_shape=jax.ShapeDtypeStruct((m, d), F32),
        scratch_shapes=[pltpu.VMEM((tm, d_ff), BF16)],
        compiler_params=_params("parallel"),
        name=name,
    )(a, wp, x, ln1[0].reshape(1, d), ln1[1].reshape(1, d), wi, wo, ln2[0].reshape(1, d), ln2[1].reshape(1, d))


def _softmax_pv(s_parts, v_parts):
    m = s_parts[0].max(axis=-1, keepdims=True)
    for s in s_parts[1:]:
        m = jnp.maximum(m, s.max(axis=-1, keepdims=True))
    acc, l = None, None
    for s, v in zip(s_parts, v_parts):
        e = jnp.exp(s - m)
        ls = e.sum(axis=-1, keepdims=True)
        pv = _dot(e.astype(BF16), v)
        acc = pv if acc is None else acc + pv
        l = ls if l is None else l + ls
    return acc / l


def _attn_prompt_kernel(q_ref, ka_ref, kb_ref, kc_ref, va_ref, vb_ref, vc_ref, w_ref, o_ref, bias_ref):
    tile = pl.program_id(1)
    win = 3 * ATT_TILE

    @pl.when(tile <= 2)
    def _():
        qc = lax.broadcasted_iota(jnp.int32, (ATT_TILE, win), 0) // CHUNK
        kc = lax.broadcasted_iota(jnp.int32, (ATT_TILE, win), 1) // CHUNK
        valid = (kc >= qc) & (kc <= qc + BAND_CHUNKS) & (kc >= BAND_CHUNKS - tile * (ATT_TILE // CHUNK))
        for h in range(A_HEADS):
            w_rows = jnp.broadcast_to(w_ref[h:h + 1, :], (ATT_TILE, w_ref.shape[1]))
            toeplitz = pltpu.roll(w_rows, 0, 1, stride=1, stride_axis=0)[:, :win]
            bias_ref[h // 2, (h % 2) * ATT_TILE:(h % 2 + 1) * ATT_TILE, :] = jnp.where(valid, toeplitz, NEG_INF)

    lane = lax.broadcasted_iota(jnp.int32, (ATT_TILE, LANES), 1)
    low_half = lane < A_HEAD_DIM
    k_refs, v_refs = (ka_ref, kb_ref, kc_ref), (va_ref, vb_ref, vc_ref)
    for p in range(D_MODEL // LANES):
        cols = slice(p * LANES, (p + 1) * LANES)
        qp = q_ref[0, :, cols]
        zero = jnp.zeros_like(qp)
        q_pair = jnp.concatenate([jnp.where(low_half, qp, zero), jnp.where(low_half, zero, qp)], axis=0)
        s_parts = [_dot_nt(q_pair, k_ref[0, :, cols]) + bias_ref[p, :, j * ATT_TILE:(j + 1) * ATT_TILE]
                   for j, k_ref in enumerate(k_refs)]
        pv = _softmax_pv(s_parts, [v_ref[0, :, cols] for v_ref in v_refs])
        o_ref[0, :, cols] = jnp.where(low_half, pv[:ATT_TILE], pv[ATT_TILE:]).astype(o_ref.dtype)


def _band_bias_vectors(rel_table):
    n_heads = rel_table.shape[1]
    win = 3 * ATT_TILE
    t = rel_table.T.astype(F32)
    u = jnp.concatenate([jnp.broadcast_to(t[:, 2 * REL_CLIP:], (n_heads, win - REL_CLIP)),
                         jnp.flip(t[:, 1:2 * REL_CLIP], axis=1),
                         jnp.broadcast_to(t[:, :1], (n_heads, REL_CLIP))], axis=1)
    assert u.shape[1] == ATT_TILE + win - 1
    return jnp.roll(jnp.pad(u, ((0, 0), (0, 1))), -(ATT_TILE - 1), axis=1)


def _attn_prompt(q, k, v, bias_vectors):
    bsz, t_len, d = q.shape
    assert t_len % ATT_TILE == 0 and BAND_PAST == 2 * ATT_TILE
    blk = (1, ATT_TILE, d)

    def past(n):
        return lambda b, i: (b, jnp.maximum(i - n, 0), 0)

    kv_specs = [pl.BlockSpec(blk, past(2)), pl.BlockSpec(blk, past(1)), pl.BlockSpec(blk, past(0))]
    return pl.pallas_call(
        _attn_prompt_kernel,
        grid=(bsz, t_len // ATT_TILE),
        in_specs=[pl.BlockSpec(blk, lambda b, i: (b, i, 0))] + kv_specs + kv_specs
                 + [pl.BlockSpec(bias_vectors.shape, lambda b, i: (0, 0))],
        out_specs=pl.BlockSpec(blk, lambda b, i: (b, i, 0)),
        out_shape=jax.ShapeDtypeStruct((bsz, t_len, d), BF16),
        scratch_shapes=[pltpu.VMEM((A_HEADS // 2, 2 * ATT_TILE, 3 * ATT_TILE), F32)],
        compiler_params=_params("parallel", "arbitrary"),
        name="attn_prompt",
    )(q, k, k, k, v, v, v, bias_vectors)


def _attn_sample_kernel(q_ref, kn_ref, vn_ref, kc_ref, vc_ref, bc_ref, bn_ref, o_ref):
    t_len = q_ref.shape[1]
    lane = lax.broadcasted_iota(jnp.int32, (t_len, LANES), 1)
    low_half = lane < A_HEAD_DIM
    for p in range(D_MODEL // LANES):
        cols = slice(p * LANES, (p + 1) * LANES)
        qp = q_ref[0, :, cols].astype(BF16)
        kc = kc_ref[0, :, cols].astype(BF16)
        vc = vc_ref[0, :, cols].astype(BF16)
        kn = kn_ref[0, :, cols].astype(BF16)
        vn = vn_ref[0, :, cols].astype(BF16)
        outs = []
        for hh in range(2):
            qm = jnp.where(low_half if hh == 0 else jnp.logical_not(low_half), qp, jnp.zeros_like(qp))
            s_c = _dot_nt(qm, kc) + bc_ref[2 * p + hh]
            s_n = _dot_nt(qm, kn) + bn_ref[2 * p + hh]
            outs.append(_softmax_pv([s_c, s_n], [vc, vn]))
        o_ref[0, :, cols] = jnp.where(low_half, outs[0], outs[1]).astype(o_ref.dtype)


def _attn_sample(q, k_new, v_new, cache_k, cache_v, bias_c, bias_n):
    bsz, t_len, d = q.shape
    n_cache = cache_k.shape[1]
    new_spec = pl.BlockSpec((1, t_len, d), lambda b: (b, 0, 0))
    cache_spec = pl.BlockSpec((1, n_cache, d), lambda b: (b, 0, 0))
    return pl.pallas_call(
        _attn_sample_kernel,
        grid=(bsz,),
        in_specs=[new_spec, new_spec, new_spec, cache_spec, cache_spec,
                  pl.BlockSpec(bias_c.shape, lambda b: (0, 0, 0)),
                  pl.BlockSpec(bias_n.shape, lambda b: (0, 0, 0))],
        out_specs=new_spec,
        out_shape=jax.ShapeDtypeStruct((bsz, t_len, d), BF16),
        compiler_params=_params("parallel"),
        name="attn_sample",
    )(q, k_new, v_new, cache_k, cache_v, bias_c, bias_n)


def _log_sigmoid(z):
    return jnp.minimum(z, 0.0) - jnp.log(1.0 + jnp.exp(-jnp.abs(z)))


def _gla_proj_kernel(x_ref, w_ref, wlow_ref, wup_ref, bgk_ref, q_ref, k_ref, v_ref, r_ref, g_ref):
    xb = x_ref[...].astype(BF16)
    n_chunk = 512
    q_ref[...] = _dot(xb, w_ref[:, 0:B_QK]) * (B_KEY_DIM ** -0.5)
    k_ref[...] = _dot(xb, w_ref[:, B_QK:2 * B_QK])
    for c in range(0, B_VD, n_chunk):
        v_ref[:, c:c + n_chunk] = _dot(xb, w_ref[:, 2 * B_QK + c:2 * B_QK + c + n_chunk])
        r_ref[:, c:c + n_chunk] = _dot(xb, w_ref[:, 2 * B_QK + B_VD + c:2 * B_QK + B_VD + c + n_chunk])
    low = _dot(xb, wlow_ref[...])
    z = _dot(low.astype(BF16), wup_ref[...]) + bgk_ref[...]
    g_ref[...] = _log_sigmoid(z) / B_GATE_NORM


def _gla_proj(x, w_main, w_low, w_up, b_gk, tm=512, name="gla_proj"):
    m, d = x.shape
    tm = _row_tile(m, tm)
    widths = (B_QK, B_QK, B_VD, B_VD, B_QK)
    const = lambda i: (0, 0)
    return pl.pallas_call(
        _gla_proj_kernel,
        grid=(m // tm,),
        in_specs=[pl.BlockSpec((tm, d), lambda i: (i, 0)),
                  pl.BlockSpec(w_main.shape, const),
                  pl.BlockSpec(w_low.shape, const),
                  pl.BlockSpec(w_up.shape, const),
                  pl.BlockSpec((1, B_QK), const)],
        out_specs=[pl.BlockSpec((tm, wd), lambda i: (i, 0)) for wd in widths],
        out_shape=[jax.ShapeDtypeStruct((m, wd), F32) for wd in widths],
        compiler_params=_params("parallel"),
        name=name,
    )(x, w_main, w_low, w_up, b_gk.reshape(1, B_QK))


def _split3(x):
    hi = x.astype(BF16)
    r1 = x - hi.astype(F32)
    mid = r1.astype(BF16)
    lo = (r1 - mid.astype(F32)).astype(BF16)
    return hi, mid, lo


def _gla_finish(o, h, rows, r_ref, gain_ref, a_ref):
    vcols = slice(h * B_VAL_DIM, (h + 1) * B_VAL_DIM)
    o = o * lax.rsqrt(jnp.mean(o * o, axis=-1, keepdims=True) + GN_EPS) * gain_ref[:, vcols]
    r_h = r_ref[0, rows, vcols]
    a_ref[0, rows, vcols] = (o * (r_h * jax.nn.sigmoid(r_h))).astype(a_ref.dtype)


def _gla_step_as_one_chunk(b, q_ref, k_ref, v_ref, r_ref, gain_ref, a_ref, st_ref):
    n = b.shape[0]
    rows = slice(0, n)
    q, k = q_ref[0], k_ref[0]
    b_last = b[n - 1:n, :]
    qhat = (q * jnp.exp(b)).astype(BF16)
    kinv = (k * jnp.exp(-b)).astype(BF16)
    kdec = (k * jnp.exp(b_last - b)).astype(BF16)
    decay = jnp.exp(b_last)
    causal = lax.broadcasted_iota(jnp.int32, (n, n), 1) <= lax.broadcasted_iota(jnp.int32, (n, n), 0)
    for h in range(B_HEADS):
        kcols = slice(h * B_KEY_DIM, (h + 1) * B_KEY_DIM)
        vcols = slice(h * B_VAL_DIM, (h + 1) * B_VAL_DIM)
        a_mat = jnp.where(causal, _dot_nt(qhat[:, kcols], kinv[:, kcols]), 0.0).astype(BF16)
        v_h = v_ref[0, :, vcols].astype(BF16)
        st = st_ref[0, h]
        o = _dot(a_mat, v_h) + _dot_nt(qhat[:, kcols], st.astype(BF16))
        st_ref[0, h] = st * decay[:, kcols] + _dot_tn(v_h, kdec[:, kcols])
        _gla_finish(o, h, rows, r_ref, gain_ref, a_ref)


def _gla_kernel(q_ref, k_ref, g_ref, v_ref, r_ref, s0_ref, gain_ref, a_ref, st_ref, *, chunk, n_chunks,
                whole_step):
    @pl.when(pl.program_id(1) == 0)
    def _():
        st_ref[...] = s0_ref[...]

    def by_sub_blocks():
        _gla_chunks_by_sub_blocks(q_ref, k_ref, g_ref, v_ref, r_ref, gain_ref, a_ref, st_ref,
                                  chunk=chunk, n_chunks=n_chunks)

    if not whole_step:
        by_sub_blocks()
        return
    n = chunk * n_chunks
    tri = (lax.broadcasted_iota(jnp.int32, (n, n), 1) <= lax.broadcasted_iota(jnp.int32, (n, n), 0))
    tri = jnp.where(tri, 1.0, 0.0).astype(BF16)
    b = sum(_dot(tri, gp) for gp in _split3(g_ref[0]))
    in_range = jnp.min(b[n - 1:n, :]) >= -GLA_MAX_LOG_DECAY

    @pl.when(in_range)
    def _():
        _gla_step_as_one_chunk(b, q_ref, k_ref, v_ref, r_ref, gain_ref, a_ref, st_ref)

    pl.when(jnp.logical_not(in_range))(by_sub_blocks)


def _gla_chunks_by_sub_blocks(q_ref, k_ref, g_ref, v_ref, r_ref, gain_ref, a_ref, st_ref, *, chunk, n_chunks):
    nsb = chunk // SUB_BLOCK
    row = lax.broadcasted_iota(jnp.int32, (chunk, chunk), 0)
    col = lax.broadcasted_iota(jnp.int32, (chunk, chunk), 1)
    blk_start = (row // SUB_BLOCK) * SUB_BLOCK
    tri_local = jnp.where((col <= row) & (col >= blk_start), 1.0, 0.0).astype(BF16)
    tri_before = jnp.where(col < blk_start, 1.0, 0.0).astype(BF16)
    rcol = lax.broadcasted_iota(jnp.int32, (SUB_BLOCK, chunk), 1)
    rrow = lax.broadcasted_iota(jnp.int32, (SUB_BLOCK, chunk), 0)

    def chunk_body(c, carry):
        row0 = pl.multiple_of(c * chunk, chunk)
        rows = pl.ds(row0, chunk)
        q = q_ref[0, rows, :]
        k = k_ref[0, rows, :]
        g_parts = _split3(g_ref[0, rows, :])
        bl = sum(_dot(tri_local, gp) for gp in g_parts)
        if nsb > 1:
            rr = sum(_dot(tri_before, gp) for gp in g_parts)
            b = bl + rr
        else:
            rr = None
            b = bl
        b_last = b[chunk - 1:chunk, :]
        qt = q * jnp.exp(bl)
        qhat = qt * jnp.exp(rr) if nsb > 1 else qt
        kdec = (k * jnp.exp(b_last - b)).astype(BF16)
        decay = jnp.exp(b_last)
        kb = k.astype(BF16)

        for h in range(B_HEADS):
            kcols = slice(h * B_KEY_DIM, (h + 1) * B_KEY_DIM)
            vcols = slice(h * B_VAL_DIM, (h + 1) * B_VAL_DIM)
            q_h, k_h, bl_h, b_h, qt_h = q[:, kcols], k[:, kcols], bl[:, kcols], b[:, kcols], qt[:, kcols]
            kb_h = kb[:, kcols]
            a_rows = []
            for i in range(nsb):
                sb = slice(i * SUB_BLOCK, (i + 1) * SUB_BLOCK)
                q_s, bl_s = q_h[sb], bl_h[sb]
                y = jnp.concatenate(
                    [q_s * jnp.exp(jnp.minimum(bl_s - bl_s[j:j + 1], 0.0)) for j in range(SUB_BLOCK)], axis=0)
                res = _dot_nt(y.astype(BF16), kb_h)
                a_i = jnp.zeros((SUB_BLOCK, chunk), F32)
                for j in range(SUB_BLOCK):
                    a_i = jnp.where(rcol == i * SUB_BLOCK + j, res[j * SUB_BLOCK:(j + 1) * SUB_BLOCK], a_i)
                a_i = jnp.where(rcol <= i * SUB_BLOCK + rrow, a_i, 0.0)
                if i > 0:
                    r_i = rr[i * SUB_BLOCK:i * SUB_BLOCK + 1, kcols]
                    kt = (k_h * jnp.exp(jnp.minimum(r_i - b_h, 0.0))).astype(BF16)
                    a_off = _dot_nt(qt_h[sb].astype(BF16), kt)
                    a_i = jnp.where(rcol < i * SUB_BLOCK, a_off, a_i)
                a_rows.append(a_i)
            a_mat = (jnp.concatenate(a_rows, axis=0) if nsb > 1 else a_rows[0]).astype(BF16)
            v_h = v_ref[0, rows, vcols].astype(BF16)
            st = st_ref[0, h]
            o = _dot(a_mat, v_h) + _dot_nt(qhat[:, kcols].astype(BF16), st.astype(BF16))
            st_ref[0, h] = st * decay[:, kcols] + _dot_tn(v_h, kdec[:, kcols])
            _gla_finish(o, h, rows, r_ref, gain_ref, a_ref)
        return carry

    lax.fori_loop(0, n_chunks, chunk_body, 0)


def _gla(q, k, g, v, r, s0t, gain, chunk, chunks_per_step, whole_step, name="gla"):
    bsz, t_len, _ = q.shape
    rows = chunk * chunks_per_step
    assert t_len % rows == 0
    qk_spec = pl.BlockSpec((1, rows, B_QK), lambda b, s: (b, s, 0))
    v_spec = pl.BlockSpec((1, rows, B_VD), lambda b, s: (b, s, 0))
    st_spec = pl.BlockSpec((1, B_HEADS, B_VAL_DIM, B_KEY_DIM), lambda b, s: (b, 0, 0, 0))
    return pl.pallas_call(
        functools.partial(_gla_kernel, chunk=chunk, n_chunks=chunks_per_step, whole_step=whole_step),
        grid=(bsz, t_len // rows),
        in_specs=[qk_spec, qk_spec, qk_spec, v_spec, v_spec, st_spec,
                  pl.BlockSpec((1, B_VD), lambda b, s: (0, 0))],
        out_specs=[v_spec, st_spec],
        out_shape=[jax.ShapeDtypeStruct((bsz, t_len, B_VD), BF16),
                   jax.ShapeDtypeStruct(s0t.shape, F32)],
        compiler_params=_params("parallel", "arbitrary"),
        name=name,
    )(q, k, g, v, r, s0t, gain.reshape(1, B_VD))


def _rel_bias(rel_table, q_pos, k_pos):
    d = jnp.clip(q_pos[:, None] - k_pos[None, :], -REL_CLIP, REL_CLIP) + REL_CLIP
    return jnp.transpose(rel_table[d], (2, 0, 1)).astype(F32)


def kernel(x_prompt, x_sample, cache_a_k, cache_a_v, state_b, w_in_a, rel_bias_a, w_out_a, w_in_b, w_gk_up_b,
           b_gk_b, gn_gain_b, w_out_b, w_ffn_in, w_ffn_out, ln1_g, ln1_b, ln2_g, ln2_b):
    bsz, t_len, d = x_prompt.shape
    dbsz, dt_len, _ = x_sample.shape
    n_cache = cache_a_k.shape[2]
    keep = min(BAND_PAST, t_len)
    xp = x_prompt.reshape(bsz * t_len, d)
    xs = x_sample.reshape(dbsz * dt_len, d)

    qscale = jnp.concatenate([jnp.full((d,), A_HEAD_DIM ** -0.5, F32), jnp.ones((2 * d,), F32)])
    w_qkv = (w_in_a[0] * qscale).astype(BF16)
    w_oa = w_out_a[0].astype(BF16)
    bias_p = _band_bias_vectors(rel_bias_a[0])
    bias_s = _rel_bias(rel_bias_a[0], n_cache + jnp.arange(dt_len), jnp.arange(n_cache + dt_len))

    q, k, v = _linear(xp, w_qkv, (d, d, d), BF16, name="qkv_prompt")
    x_tail = x_prompt[:, t_len - keep:].reshape(bsz * keep, d)
    k_tail, v_tail = _linear(x_tail, w_qkv[:, d:], (d, d), F32, name="kv_tail")
    att = _attn_prompt(q.reshape(bsz, t_len, d), k.reshape(bsz, t_len, d), v.reshape(bsz, t_len, d), bias_p)

    qs, ks, vs = _linear(xs, w_qkv, (d, d, d), F32, tm=128, name="qkv_sample")
    att_s = _attn_sample(qs.reshape(dbsz, dt_len, d), ks.reshape(dbsz, dt_len, d), vs.reshape(dbsz, dt_len, d),
                         cache_a_k[0].reshape(dbsz, n_cache, d), cache_a_v[0].reshape(dbsz, n_cache, d),
                         bias_s[:, :, :n_cache], bias_s[:, :, n_cache:])

    wi0, wo0 = w_ffn_in[0].astype(BF16), w_ffn_out[0].astype(BF16)
    ln1, ln2 = (ln1_g[0], ln1_b[0]), (ln2_g[0], ln2_b[0])
    xp = _layer_tail(att.reshape(bsz * t_len, d), w_oa, xp, ln1, wi0, wo0, ln2, name="tail0_prompt")
    xs = _layer_tail(att_s.reshape(dbsz * dt_len, d), w_oa, xs, ln1, wi0, wo0, ln2, tm=128, name="tail0_sample")

    n_main = 2 * B_QK + 2 * B_VD
    w_main = w_in_b[0][:, :n_main].astype(BF16)
    w_low = jnp.pad(w_in_b[0][:, n_main:], ((0, 0), (0, LANES - B_GATE_RANK))).astype(BF16)
    w_up = jnp.pad(w_gk_up_b[0], ((0, LANES - B_GATE_RANK), (0, 0))).astype(BF16)
    w_ob = w_out_b[0].astype(BF16)

    gq, gk, gv, gr, gg = _gla_proj(xp, w_main, w_low, w_up, b_gk_b[0], name="gla_proj_prompt")
    s0 = jnp.zeros((bsz, B_HEADS, B_VAL_DIM, B_KEY_DIM), F32)
    sh = lambda a: a.reshape(bsz, t_len, a.shape[-1])
    a_p, st_p = _gla(sh(gq), sh(gk), sh(gg), sh(gv), sh(gr), s0, gn_gain_b[0], CHUNK, 4, True,
                     name="gla_prompt")

    sq, sk, sv, sr, sg = _gla_proj(xs, w_main, w_low, w_up, b_gk_b[0], tm=128, name="gla_proj_sample")
    shs = lambda a: a.reshape(dbsz, dt_len, a.shape[-1])
    a_s, st_s = _gla(shs(sq), shs(sk), shs(sg), shs(sv), shs(sr), jnp.swapaxes(state_b[0], -1, -2),
                     gn_gain_b[0], dt_len, 1, False, name="gla_sample")

    wi1, wo1 = w_ffn_in[1].astype(BF16), w_ffn_out[1].astype(BF16)
    ln1, ln2 = (ln1_g[1], ln1_b[1]), (ln2_g[1], ln2_b[1])
    xp = _layer_tail(a_p.reshape(bsz * t_len, B_VD), w_ob, xp, ln1, wi1, wo1, ln2, name="tail1_prompt")
    xs = _layer_tail(a_s.reshape(dbsz * dt_len, B_VD), w_ob, xs, ln1, wi1, wo1, ln2, tm=128, name="tail1_sample")

    heads = lambda a, n, t: a.reshape(1, n, t, A_HEADS, A_HEAD_DIM)
    return (xp.reshape(bsz, t_len, d), xs.reshape(dbsz, dt_len, d),
            heads(k_tail, bsz, keep), heads(v_tail, bsz, keep), jnp.swapaxes(st_p, -1, -2)[None],
            heads(ks, dbsz, dt_len), heads(vs, dbsz, dt_len), jnp.swapaxes(st_s, -1, -2)[None])
```

```python
import functools

import jax
import jax.numpy as jnp
from jax import lax
from jax.experimental import pallas as pl
from jax.experimental.pallas import tpu as pltpu

F32 = jnp.float32
BF16 = jnp.bfloat16

D_MODEL = 1024
CHUNK = 64
A_HEADS = 16
A_HEAD_DIM = D_MODEL // A_HEADS
BAND_CHUNKS = 8
BAND_PAST = BAND_CHUNKS * CHUNK
REL_CLIP = 128
B_HEADS = 4
B_KEY_DIM = D_MODEL // 2 // B_HEADS
B_VAL_DIM = D_MODEL // B_HEADS
B_QK = B_HEADS * B_KEY_DIM
B_VD = B_HEADS * B_VAL_DIM
B_GATE_RANK = 16
B_GATE_NORM = 16.0
D_FF = -(-8 * D_MODEL // 768) * 256
DEPTH = 2
ALPHA = (2.0 * DEPTH) ** 0.25
LN_EPS = 1e-5
GN_EPS = 1e-6
NEG_INF = -1e30

LANES = 128
SUB_BLOCK = 16
ATT_TILE = 4 * CHUNK
GLA_MAX_LOG_DECAY = 60.0
VMEM_LIMIT = 56 * 1024 * 1024


def _dot(a, b):
    return jnp.dot(a, b, preferred_element_type=F32)


def _dot_nt(a, b):
    return lax.dot_general(a, b, (((1,), (1,)), ((), ())), preferred_element_type=F32)


def _dot_tn(a, b):
    return lax.dot_general(a, b, (((0,), (0,)), ((), ())), preferred_element_type=F32)


def _params(*sem):
    return pltpu.CompilerParams(dimension_semantics=sem, vmem_limit_bytes=VMEM_LIMIT)


def _row_tile(m, pref):
    t = min(m, pref)
    assert m % t == 0
    return t


def _linear_kernel(x_ref, w_ref, *o_refs, n_chunk):
    xb = x_ref[...].astype(BF16)
    col = 0
    for o_ref in o_refs:
        nj = o_ref.shape[1]
        for c in range(0, nj, n_chunk):
            o_ref[:, c:c + n_chunk] = _dot(xb, w_ref[:, col + c:col + c + n_chunk]).astype(o_ref.dtype)
        col += nj


def _linear(x, w, widths, dtype, tm=512, name="linear"):
    m, k = x.shape
    tm = _row_tile(m, tm)
    n_chunk = 512
    assert all(wd % n_chunk == 0 for wd in widths) and sum(widths) == w.shape[1]
    return pl.pallas_call(
        functools.partial(_linear_kernel, n_chunk=n_chunk),
        grid=(m // tm,),
        in_specs=[pl.BlockSpec((tm, k), lambda i: (i, 0)),
                  pl.BlockSpec(w.shape, lambda i: (0, 0))],
        out_specs=[pl.BlockSpec((tm, wd), lambda i: (i, 0)) for wd in widths],
        out_shape=[jax.ShapeDtypeStruct((m, wd), dtype) for wd in widths],
        compiler_params=_params("parallel"),
        name=name,
    )(x, w)


def _res_ln(x, y, g, b):
    t = ALPHA * x + y
    mu = jnp.mean(t, axis=-1, keepdims=True)
    d = t - mu
    var = jnp.mean(d * d, axis=-1, keepdims=True)
    return d * lax.rsqrt(var + LN_EPS) * g + b


def _layer_tail_kernel(a_ref, wp_ref, x_ref, g1_ref, b1_ref, wi_ref, wo_ref, g2_ref, b2_ref, o_ref, h_ref,
                       *, ff_chunk):
    x1 = _res_ln(x_ref[...], _dot(a_ref[...], wp_ref[...]), g1_ref[...], b1_ref[...])
    xb = x1.astype(BF16)
    d_ff = h_ref.shape[1]
    for c in range(0, d_ff, ff_chunk):
        gate = _dot(xb, wi_ref[:, c:c + ff_chunk])
        up = _dot(xb, wi_ref[:, d_ff + c:d_ff + c + ff_chunk])
        h_ref[:, c:c + ff_chunk] = (gate * jax.nn.sigmoid(gate) * up).astype(BF16)
    y = _dot(h_ref[...], wo_ref[...])
    o_ref[...] = _res_ln(x1, y, g2_ref[...], b2_ref[...])


def _layer_tail(a, wp, x, ln1, wi, wo, ln2, tm=1024, name="layer_tail"):
    m, k = a.shape
    d = x.shape[1]
    d_ff = wo.shape[0]
    tm = _row_tile(m, tm)
    ff_chunk = 256
    assert d_ff % ff_chunk == 0
    rows = lambda width: pl.BlockSpec((tm, width), lambda i: (i, 0))
    resident = lambda w: pl.BlockSpec(w.shape, lambda i: (0, 0), pipeline_mode=pl.Buffered(1))
    vec = pl.BlockSpec((1, d), lambda i: (0, 0))
    return pl.pallas_call(
        functools.partial(_layer_tail_kernel, ff_chunk=ff_chunk),
        grid=(m // tm,),
        in_specs=[rows(k), resident(wp), rows(d), vec, vec, resident(wi), resident(wo), vec, vec],
        out_specs=rows(d),
        out_shape=jax.ShapeDtypeStruct((m, d), F32),
        scratch_shapes=[pltpu.VMEM((tm, d_ff), BF16)],
        compiler_params=_params("parallel"),
        name=name,
    )(a, wp, x, ln1[0].reshape(1, d), ln1[1].reshape(1, d), wi, wo, ln2[0].reshape(1, d), ln2[1].reshape(1, d))


def _softmax_pv(s_parts, v_parts):
    m = s_parts[0].max(axis=-1, keepdims=True)
    for s in s_parts[1:]:
        m = jnp.maximum(m, s.max(axis=-1, keepdims=True))
    acc, l = None, None
    for s, v in zip(s_parts, v_parts):
        e = jnp.exp(s - m)
        ls = e.sum(axis=-1, keepdims=True)
        pv = _dot(e.astype(BF16), v)
        acc = pv if acc is None else acc + pv
        l = ls if l is None else l + ls
    return acc / l


def _attn_prompt_kernel(q_ref, ka_ref, kb_ref, kc_ref, va_ref, vb_ref, vc_ref, w_ref, o_ref,
                        bias_ref, vext_ref):
    tile = pl.program_id(1)
    win = 3 * ATT_TILE

    @pl.when(tile <= 2)
    def _():
        qc = lax.broadcasted_iota(jnp.int32, (ATT_TILE, win), 0) // CHUNK
        kc = lax.broadcasted_iota(jnp.int32, (ATT_TILE, win), 1) // CHUNK
        valid = (kc >= qc) & (kc <= qc + BAND_CHUNKS) & (kc >= BAND_CHUNKS - tile * (ATT_TILE // CHUNK))
        for h in range(A_HEADS):
            w_rows = jnp.broadcast_to(w_ref[h:h + 1, :], (ATT_TILE, w_ref.shape[1]))
            toeplitz = pltpu.roll(w_rows, 0, 1, stride=1, stride_axis=0)[:, :win]
            bias_ref[h // 2, (h % 2) * ATT_TILE:(h % 2 + 1) * ATT_TILE, :] = jnp.where(valid, toeplitz, NEG_INF)

    n_groups = D_MODEL // LANES

    @pl.when(tile == 0)
    def _():
        vext_ref[:, :, :, LANES:] = jnp.ones((3, n_groups, ATT_TILE, LANES), BF16)

    k_refs, v_refs = (ka_ref, kb_ref, kc_ref), (va_ref, vb_ref, vc_ref)
    for j, v_ref in enumerate(v_refs):
        for p in range(n_groups):
            vext_ref[j, p, :, :LANES] = v_ref[0, :, p * LANES:(p + 1) * LANES]

    lane = lax.broadcasted_iota(jnp.int32, (ATT_TILE, LANES), 1)
    low_half = lane < A_HEAD_DIM
    for p in range(n_groups):
        cols = slice(p * LANES, (p + 1) * LANES)
        qp = q_ref[0, :, cols]
        zero = jnp.zeros_like(qp)
        q_pair = jnp.concatenate([jnp.where(low_half, qp, zero), jnp.where(low_half, zero, qp)], axis=0)
        s_parts = [_dot_nt(q_pair, k_ref[0, :, cols]) + bias_ref[p, :, j * ATT_TILE:(j + 1) * ATT_TILE]
                   for j, k_ref in enumerate(k_refs)]
        m = functools.reduce(jnp.maximum, [s.max(axis=-1, keepdims=True) for s in s_parts])
        pv = sum(_dot(jnp.exp(s - m).astype(BF16), vext_ref[j, p]) for j, s in enumerate(s_parts))
        pv = pv[:, :LANES] / pv[:, LANES:]
        o_ref[0, :, cols] = jnp.where(low_half, pv[:ATT_TILE], pv[ATT_TILE:]).astype(o_ref.dtype)


def _band_bias_vectors(rel_table):
    n_heads = rel_table.shape[1]
    win = 3 * ATT_TILE
    t = rel_table.T.astype(F32)
    u = jnp.concatenate([jnp.broadcast_to(t[:, 2 * REL_CLIP:], (n_heads, win - REL_CLIP)),
                         jnp.flip(t[:, 1:2 * REL_CLIP], axis=1),
                         jnp.broadcast_to(t[:, :1], (n_heads, REL_CLIP))], axis=1)
    assert u.shape[1] == ATT_TILE + win - 1
    return jnp.roll(jnp.pad(u, ((0, 0), (0, 1))), -(ATT_TILE - 1), axis=1)


def _attn_prompt(q, k, v, bias_vectors):
    bsz, t_len, d = q.shape
    assert t_len % ATT_TILE == 0 and BAND_PAST == 2 * ATT_TILE
    blk = (1, ATT_TILE, d)

    def past(n):
        return lambda b, i: (b, jnp.maximum(i - n, 0), 0)

    kv_specs = [pl.BlockSpec(blk, past(2)), pl.BlockSpec(blk, past(1)), pl.BlockSpec(blk, past(0))]
    return pl.pallas_call(
        _attn_prompt_kernel,
        grid=(bsz, t_len // ATT_TILE),
        in_specs=[pl.BlockSpec(blk, lambda b, i: (b, i, 0))] + kv_specs + kv_specs
                 + [pl.BlockSpec(bias_vectors.shape, lambda b, i: (0, 0))],
        out_specs=pl.BlockSpec(blk, lambda b, i: (b, i, 0)),
        out_shape=jax.ShapeDtypeStruct((bsz, t_len, d), BF16),
        scratch_shapes=[pltpu.VMEM((A_HEADS // 2, 2 * ATT_TILE, 3 * ATT_TILE), F32),
                        pltpu.VMEM((3, d // LANES, ATT_TILE, 2 * LANES), BF16)],
        compiler_params=_params("parallel", "arbitrary"),
        name="attn_prompt",
    )(q, k, k, k, v, v, v, bias_vectors)


def _attn_sample_kernel(q_ref, kn_ref, vn_ref, kc_ref, vc_ref, bc_ref, bn_ref, o_ref):
    t_len = q_ref.shape[1]
    lane = lax.broadcasted_iota(jnp.int32, (t_len, LANES), 1)
    low_half = lane < A_HEAD_DIM
    for p in range(D_MODEL // LANES):
        cols = slice(p * LANES, (p + 1) * LANES)
        qp = q_ref[0, :, cols].astype(BF16)
        kc = kc_ref[0, :, cols].astype(BF16)
        vc = vc_ref[0, :, cols].astype(BF16)
        kn = kn_ref[0, :, cols].astype(BF16)
        vn = vn_ref[0, :, cols].astype(BF16)
        outs = []
        for hh in range(2):
            qm = jnp.where(low_half if hh == 0 else jnp.logical_not(low_half), qp, jnp.zeros_like(qp))
            s_c = _dot_nt(qm, kc) + bc_ref[2 * p + hh]
            s_n = _dot_nt(qm, kn) + bn_ref[2 * p + hh]
            outs.append(_softmax_pv([s_c, s_n], [vc, vn]))
        o_ref[0, :, cols] = jnp.where(low_half, outs[0], outs[1]).astype(o_ref.dtype)


def _attn_sample(q, k_new, v_new, cache_k, cache_v, bias_c, bias_n):
    bsz, t_len, d = q.shape
    n_cache = cache_k.shape[1]
    new_spec = pl.BlockSpec((1, t_len, d), lambda b: (b, 0, 0))
    cache_spec = pl.BlockSpec((1, n_cache, d), lambda b: (b, 0, 0))
    return pl.pallas_call(
        _attn_sample_kernel,
        grid=(bsz,),
        in_specs=[new_spec, new_spec, new_spec, cache_spec, cache_spec,
                  pl.BlockSpec(bias_c.shape, lambda b: (0, 0, 0)),
                  pl.BlockSpec(bias_n.shape, lambda b: (0, 0, 0))],
        out_specs=new_spec,
        out_shape=jax.ShapeDtypeStruct((bsz, t_len, d), BF16),
        compiler_params=_params("parallel"),
        name="attn_sample",
    )(q, k_new, v_new, cache_k, cache_v, bias_c, bias_n)


def _log_sigmoid(z):
    return jnp.minimum(z, 0.0) - jnp.log(1.0 + jnp.exp(-jnp.abs(z)))


def _gla_project(x_ref, w_ref, wlow_ref, wup_ref, bgk_ref, q_ref, k_ref, g_ref, v_ref, r_ref):
    xb = x_ref[0].astype(BF16)
    n_chunk = 512
    q_ref[0] = _dot(xb, w_ref[:, 0:B_QK]) * (B_KEY_DIM ** -0.5)
    k_ref[0] = _dot(xb, w_ref[:, B_QK:2 * B_QK])
    for c in range(0, B_VD, n_chunk):
        v_ref[0, :, c:c + n_chunk] = _dot(xb, w_ref[:, 2 * B_QK + c:2 * B_QK + c + n_chunk])
        r_ref[0, :, c:c + n_chunk] = _dot(xb, w_ref[:, 2 * B_QK + B_VD + c:2 * B_QK + B_VD + c + n_chunk])
    low = _dot(xb, wlow_ref[...])
    z = _dot(low.astype(BF16), wup_ref[...]) + bgk_ref[...]
    g_ref[0] = _log_sigmoid(z) / B_GATE_NORM


def _split3(x):
    hi = x.astype(BF16)
    r1 = x - hi.astype(F32)
    mid = r1.astype(BF16)
    lo = (r1 - mid.astype(F32)).astype(BF16)
    return hi, mid, lo


def _gla_finish(o, h, rows, r_ref, gain_ref, a_ref):
    vcols = slice(h * B_VAL_DIM, (h + 1) * B_VAL_DIM)
    o = o * lax.rsqrt(jnp.mean(o * o, axis=-1, keepdims=True) + GN_EPS) * gain_ref[:, vcols]
    r_h = r_ref[0, rows, vcols]
    a_ref[0, rows, vcols] = (o * (r_h * jax.nn.sigmoid(r_h))).astype(a_ref.dtype)


def _gla_step_as_one_chunk(b, q_ref, k_ref, v_ref, r_ref, gain_ref, a_ref, st_ref):
    n = b.shape[0]
    rows = slice(0, n)
    q, k = q_ref[0], k_ref[0]
    b_last = b[n - 1:n, :]
    qhat = (q * jnp.exp(b)).astype(BF16)
    kinv = (k * jnp.exp(-b)).astype(BF16)
    kdec = (k * jnp.exp(b_last - b)).astype(BF16)
    decay = jnp.exp(b_last)
    causal = lax.broadcasted_iota(jnp.int32, (n, n), 1) <= lax.broadcasted_iota(jnp.int32, (n, n), 0)
    for h in range(B_HEADS):
        kcols = slice(h * B_KEY_DIM, (h + 1) * B_KEY_DIM)
        vcols = slice(h * B_VAL_DIM, (h + 1) * B_VAL_DIM)
        a_mat = jnp.where(causal, _dot_nt(qhat[:, kcols], kinv[:, kcols]), 0.0).astype(BF16)
        v_h = v_ref[0, :, vcols].astype(BF16)
        st = st_ref[0, h]
        o = _dot(a_mat, v_h) + _dot_nt(qhat[:, kcols], st.astype(BF16))
        st_ref[0, h] = st * decay[:, kcols] + _dot_tn(v_h, kdec[:, kcols])
        _gla_finish(o, h, rows, r_ref, gain_ref, a_ref)


def _gla_kernel(x_ref, w_ref, wlow_ref, wup_ref, bgk_ref, s0_ref, gain_ref, a_ref, st_ref,
                q_ref, k_ref, g_ref, v_ref, r_ref, *, chunk, n_chunks, whole_step):
    @pl.when(pl.program_id(1) == 0)
    def _():
        st_ref[...] = s0_ref[...]

    _gla_project(x_ref, w_ref, wlow_ref, wup_ref, bgk_ref, q_ref, k_ref, g_ref, v_ref, r_ref)

    def by_sub_blocks():
        _gla_chunks_by_sub_blocks(q_ref, k_ref, g_ref, v_ref, r_ref, gain_ref, a_ref, st_ref,
                                  chunk=chunk, n_chunks=n_chunks)

    if not whole_step:
        by_sub_blocks()
        return
    n = chunk * n_chunks
    tri = (lax.broadcasted_iota(jnp.int32, (n, n), 1) <= lax.broadcasted_iota(jnp.int32, (n, n), 0))
    tri = jnp.where(tri, 1.0, 0.0).astype(BF16)
    b = sum(_dot(tri, gp) for gp in _split3(g_ref[0]))
    in_range = jnp.min(b[n - 1:n, :]) >= -GLA_MAX_LOG_DECAY

    @pl.when(in_range)
    def _():
        _gla_step_as_one_chunk(b, q_ref, k_ref, v_ref, r_ref, gain_ref, a_ref, st_ref)

    pl.when(jnp.logical_not(in_range))(by_sub_blocks)


def _gla_chunks_by_sub_blocks(q_ref, k_ref, g_ref, v_ref, r_ref, gain_ref, a_ref, st_ref, *, chunk, n_chunks):
    nsb = chunk // SUB_BLOCK
    row = lax.broadcasted_iota(jnp.int32, (chunk, chunk), 0)
    col = lax.broadcasted_iota(jnp.int32, (chunk, chunk), 1)
    blk_start = (row // SUB_BLOCK) * SUB_BLOCK
    tri_local = jnp.where((col <= row) & (col >= blk_start), 1.0, 0.0).astype(BF16)
    tri_before = jnp.where(col < blk_start, 1.0, 0.0).astype(BF16)
    rcol = lax.broadcasted_iota(jnp.int32, (SUB_BLOCK, chunk), 1)
    rrow = lax.broadcasted_iota(jnp.int32, (SUB_BLOCK, chunk), 0)

    def chunk_body(c, carry):
        row0 = pl.multiple_of(c * chunk, chunk)
        rows = pl.ds(row0, chunk)
        q = q_ref[0, rows, :]
        k = k_ref[0, rows, :]
        g_parts = _split3(g_ref[0, rows, :])
        bl = sum(_dot(tri_local, gp) for gp in g_parts)
        if nsb > 1:
            rr = sum(_dot(tri_before, gp) for gp in g_parts)
            b = bl + rr
        else:
            rr = None
            b = bl
        b_last = b[chunk - 1:chunk, :]
        qt = q * jnp.exp(bl)
        qhat = qt * jnp.exp(rr) if nsb > 1 else qt
        kdec = (k * jnp.exp(b_last - b)).astype(BF16)
        decay = jnp.exp(b_last)
        kb = k.astype(BF16)

        for h in range(B_HEADS):
            kcols = slice(h * B_KEY_DIM, (h + 1) * B_KEY_DIM)
            vcols = slice(h * B_VAL_DIM, (h + 1) * B_VAL_DIM)
            q_h, k_h, bl_h, b_h, qt_h = q[:, kcols], k[:, kcols], bl[:, kcols], b[:, kcols], qt[:, kcols]
            kb_h = kb[:, kcols]
            a_rows = []
            for i in range(nsb):
                sb = slice(i * SUB_BLOCK, (i + 1) * SUB_BLOCK)
                q_s, bl_s = q_h[sb], bl_h[sb]
                y = jnp.concatenate(
                    [q_s * jnp.exp(jnp.minimum(bl_s - bl_s[j:j + 1], 0.0)) for j in range(SUB_BLOCK)], axis=0)
                res = _dot_nt(y.astype(BF16), kb_h)
                a_i = jnp.zeros((SUB_BLOCK, chunk), F32)
                for j in range(SUB_BLOCK):
                    a_i = jnp.where(rcol == i * SUB_BLOCK + j, res[j * SUB_BLOCK:(j + 1) * SUB_BLOCK], a_i)
                a_i = jnp.where(rcol <= i * SUB_BLOCK + rrow, a_i, 0.0)
                if i > 0:
                    r_i = rr[i * SUB_BLOCK:i * SUB_BLOCK + 1, kcols]
                    kt = (k_h * jnp.exp(jnp.minimum(r_i - b_h, 0.0))).astype(BF16)
                    a_off = _dot_nt(qt_h[sb].astype(BF16), kt)
                    a_i = jnp.where(rcol < i * SUB_BLOCK, a_off, a_i)
                a_rows.append(a_i)
            a_mat = (jnp.concatenate(a_rows, axis=0) if nsb > 1 else a_rows[0]).astype(BF16)
            v_h = v_ref[0, rows, vcols].astype(BF16)
            st = st_ref[0, h]
            o = _dot(a_mat, v_h) + _dot_nt(qhat[:, kcols].astype(BF16), st.astype(BF16))
            st_ref[0, h] = st * decay[:, kcols] + _dot_tn(v_h, kdec[:, kcols])
            _gla_finish(o, h, rows, r_ref, gain_ref, a_ref)
        return carry

    lax.fori_loop(0, n_chunks, chunk_body, 0)


def _gla(x, w_main, w_low, w_up, b_gk, s0t, gain, chunk, chunks_per_step, whole_step, name="gla"):
    bsz, t_len, d = x.shape
    rows = chunk * chunks_per_step
    assert t_len % rows == 0
    const = lambda b, s: (0, 0)
    resident = lambda w: pl.BlockSpec(w.shape, const, pipeline_mode=pl.Buffered(1))
    st_spec = pl.BlockSpec((1, B_HEADS, B_VAL_DIM, B_KEY_DIM), lambda b, s: (b, 0, 0, 0))
    out_spec = pl.BlockSpec((1, rows, B_VD), lambda b, s: (b, s, 0))
    return pl.pallas_call(
        functools.partial(_gla_kernel, chunk=chunk, n_chunks=chunks_per_step, whole_step=whole_step),
        grid=(bsz, t_len // rows),
        in_specs=[pl.BlockSpec((1, rows, d), lambda b, s: (b, s, 0)),
                  resident(w_main), resident(w_low), resident(w_up), pl.BlockSpec((1, B_QK), const),
                  st_spec, pl.BlockSpec((1, B_VD), const)],
        out_specs=[out_spec, st_spec],
        out_shape=[jax.ShapeDtypeStruct((bsz, t_len, B_VD), BF16),
                   jax.ShapeDtypeStruct(s0t.shape, F32)],
        scratch_shapes=[pltpu.VMEM((1, rows, B_QK), F32)] * 3 + [pltpu.VMEM((1, rows, B_VD), F32)] * 2,
        compiler_params=_params("parallel", "arbitrary"),
        name=name,
    )(x, w_main, w_low, w_up, b_gk.reshape(1, B_QK), s0t, gain.reshape(1, B_VD))


def _rel_bias(rel_table, q_pos, k_pos):
    d = jnp.clip(q_pos[:, None] - k_pos[None, :], -REL_CLIP, REL_CLIP) + REL_CLIP
    return jnp.transpose(rel_table[d], (2, 0, 1)).astype(F32)


def kernel(x_prompt, x_sample, cache_a_k, cache_a_v, state_b, w_in_a, rel_bias_a, w_out_a, w_in_b, w_gk_up_b,
           b_gk_b, gn_gain_b, w_out_b, w_ffn_in, w_ffn_out, ln1_g, ln1_b, ln2_g, ln2_b):
    bsz, t_len, d = x_prompt.shape
    dbsz, dt_len, _ = x_sample.shape
    n_cache = cache_a_k.shape[2]
    keep = min(BAND_PAST, t_len)
    xp = x_prompt.reshape(bsz * t_len, d)
    xs = x_sample.reshape(dbsz * dt_len, d)

    qscale = jnp.concatenate([jnp.full((d,), A_HEAD_DIM ** -0.5, F32), jnp.ones((2 * d,), F32)])
    w_qkv = (w_in_a[0] * qscale).astype(BF16)
    w_oa = w_out_a[0].astype(BF16)
    bias_p = _band_bias_vectors(rel_bias_a[0])
    bias_s = _rel_bias(rel_bias_a[0], n_cache + jnp.arange(dt_len), jnp.arange(n_cache + dt_len))

    q, k, v = _linear(xp, w_qkv, (d, d, d), BF16, name="qkv_prompt")
    x_tail = x_prompt[:, t_len - keep:].reshape(bsz * keep, d)
    k_tail, v_tail = _linear(x_tail, w_qkv[:, d:], (d, d), F32, name="kv_tail")
    att = _attn_prompt(q.reshape(bsz, t_len, d), k.reshape(bsz, t_len, d), v.reshape(bsz, t_len, d), bias_p)

    qs, ks, vs = _linear(xs, w_qkv, (d, d, d), F32, tm=128, name="qkv_sample")
    att_s = _attn_sample(qs.reshape(dbsz, dt_len, d), ks.reshape(dbsz, dt_len, d), vs.reshape(dbsz, dt_len, d),
                         cache_a_k[0].reshape(dbsz, n_cache, d), cache_a_v[0].reshape(dbsz, n_cache, d),
                         bias_s[:, :, :n_cache], bias_s[:, :, n_cache:])

    wi0, wo0 = w_ffn_in[0].astype(BF16), w_ffn_out[0].astype(BF16)
    ln1, ln2 = (ln1_g[0], ln1_b[0]), (ln2_g[0], ln2_b[0])
    xp = _layer_tail(att.reshape(bsz * t_len, d), w_oa, xp, ln1, wi0, wo0, ln2, name="tail0_prompt")
    xs = _layer_tail(att_s.reshape(dbsz * dt_len, d), w_oa, xs, ln1, wi0, wo0, ln2, tm=128, name="tail0_sample")

    n_main = 2 * B_QK + 2 * B_VD
    w_main = w_in_b[0][:, :n_main].astype(BF16)
    w_low = jnp.pad(w_in_b[0][:, n_main:], ((0, 0), (0, LANES - B_GATE_RANK))).astype(BF16)
    w_up = jnp.pad(w_gk_up_b[0], ((0, LANES - B_GATE_RANK), (0, 0))).astype(BF16)
    w_ob = w_out_b[0].astype(BF16)

    s0 = jnp.zeros((bsz, B_HEADS, B_VAL_DIM, B_KEY_DIM), F32)
    a_p, st_p = _gla(xp.reshape(bsz, t_len, d), w_main, w_low, w_up, b_gk_b[0], s0, gn_gain_b[0],
                     CHUNK, 4, True, name="gla_prompt")
    a_s, st_s = _gla(xs.reshape(dbsz, dt_len, d), w_main, w_low, w_up, b_gk_b[0],
                     jnp.swapaxes(state_b[0], -1, -2), gn_gain_b[0], dt_len, 1, False, name="gla_sample")

    wi1, wo1 = w_ffn_in[1].astype(BF16), w_ffn_out[1].astype(BF16)
    ln1, ln2 = (ln1_g[1], ln1_b[1]), (ln2_g[1], ln2_b[1])
    xp = _layer_tail(a_p.reshape(bsz * t_len, B_VD), w_ob, xp, ln1, wi1, wo1, ln2, name="tail1_prompt")
    xs = _layer_tail(a_s.reshape(dbsz * dt_len, B_VD), w_ob, xs, ln1, wi1, wo1, ln2, tm=128, name="tail1_sample")

    heads = lambda a, n, t: a.reshape(1, n, t, A_HEADS, A_HEAD_DIM)
    return (xp.reshape(bsz, t_len, d), xs.reshape(dbsz, dt_len, d),
            heads(k_tail, bsz, keep), heads(v_tail, bsz, keep), jnp.swapaxes(st_p, -1, -2)[None],
            heads(ks, dbsz, dt_len), heads(vs, dbsz, dt_len), jnp.swapaxes(st_s, -1, -2)[None])
```

```python
import functools

import jax
import jax.numpy as jnp
from jax import lax
from jax.experimental import pallas as pl
from jax.experimental.pallas import tpu as pltpu

F32 = jnp.float32
BF16 = jnp.bfloat16

D_MODEL = 1024
CHUNK = 64
A_HEADS = 16
A_HEAD_DIM = D_MODEL // A_HEADS
BAND_CHUNKS = 8
BAND_PAST = BAND_CHUNKS * CHUNK
REL_CLIP = 128
B_HEADS = 4
B_KEY_DIM = D_MODEL // 2 // B_HEADS
B_VAL_DIM = D_MODEL // B_HEADS
B_QK = B_HEADS * B_KEY_DIM
B_VD = B_HEADS * B_VAL_DIM
B_GATE_RANK = 16
B_GATE_NORM = 16.0
D_FF = -(-8 * D_MODEL // 768) * 256
DEPTH = 2
ALPHA = (2.0 * DEPTH) ** 0.25
LN_EPS = 1e-5
GN_EPS = 1e-6
NEG_INF = -1e30

LANES = 128
SUB_BLOCK = 16
ATT_TILE = 4 * CHUNK
GLA_STEP_ROWS = 1024
GLA_WIDE_CHUNK = 256
GLA_MAX_LOG_DECAY = 60.0
VMEM_LIMIT = 56 * 1024 * 1024


def _dot(a, b):
    return jnp.dot(a, b, preferred_element_type=F32)


def _dot_nt(a, b):
    return lax.dot_general(a, b, (((1,), (1,)), ((), ())), preferred_element_type=F32)


def _dot_tn(a, b):
    return lax.dot_general(a, b, (((0,), (0,)), ((), ())), preferred_element_type=F32)


def _params(*sem):
    return pltpu.CompilerParams(dimension_semantics=sem, vmem_limit_bytes=VMEM_LIMIT)


def _row_tile(m, pref):
    t = min(m, pref)
    assert m % t == 0
    return t


def _linear_kernel(x_ref, w_ref, *o_refs, n_chunk):
    xb = x_ref[...].astype(BF16)
    col = 0
    for o_ref in o_refs:
        nj = o_ref.shape[1]
        for c in range(0, nj, n_chunk):
            o_ref[:, c:c + n_chunk] = _dot(xb, w_ref[:, col + c:col + c + n_chunk]).astype(o_ref.dtype)
        col += nj


def _linear(x, w, widths, dtype, tm=512, name="linear"):
    m, k = x.shape
    tm = _row_tile(m, tm)
    n_chunk = 512
    assert all(wd % n_chunk == 0 for wd in widths) and sum(widths) == w.shape[1]
    return pl.pallas_call(
        functools.partial(_linear_kernel, n_chunk=n_chunk),
        grid=(m // tm,),
        in_specs=[pl.BlockSpec((tm, k), lambda i: (i, 0)),
                  pl.BlockSpec(w.shape, lambda i: (0, 0))],
        out_specs=[pl.BlockSpec((tm, wd), lambda i: (i, 0)) for wd in widths],
        out_shape=[jax.ShapeDtypeStruct((m, wd), dtype) for wd in widths],
        compiler_params=_params("parallel"),
        name=name,
    )(x, w)


def _res_ln(x, y, g, b):
    t = ALPHA * x + y
    mu = jnp.mean(t, axis=-1, keepdims=True)
    d = t - mu
    var = jnp.mean(d * d, axis=-1, keepdims=True)
    return d * lax.rsqrt(var + LN_EPS) * g + b


def _layer_tail_kernel(a_ref, wp_ref, x_ref, g1_ref, b1_ref, wi_ref, wo_ref, g2_ref, b2_ref, o_ref, h_ref,
                       *, ff_chunk):
    x1 = _res_ln(x_ref[...], _dot(a_ref[...], wp_ref[...]), g1_ref[...], b1_ref[...])
    xb = x1.astype(BF16)
    d_ff = h_ref.shape[1]
    for c in range(0, d_ff, ff_chunk):
        gate = _dot(xb, wi_ref[:, c:c + ff_chunk])
        up = _dot(xb, wi_ref[:, d_ff + c:d_ff + c + ff_chunk])
        h_ref[:, c:c + ff_chunk] = (gate * jax.nn.sigmoid(gate) * up).astype(BF16)
    y = _dot(h_ref[...], wo_ref[...])
    o_ref[...] = _res_ln(x1, y, g2_ref[...], b2_ref[...])


def _layer_tail(a, wp, x, ln1, wi, wo, ln2, tm=1024, name="layer_tail"):
    m, k = a.shape
    d = x.shape[1]
    d_ff = wo.shape[0]
    tm = _row_tile(m, tm)
    ff_chunk = 256
    assert d_ff % ff_chunk == 0
    rows = lambda width: pl.BlockSpec((tm, width), lambda i: (i, 0))
    resident = lambda w: pl.BlockSpec(w.shape, lambda i: (0, 0), pipeline_mode=pl.Buffered(1))
    vec = pl.BlockSpec((1, d), lambda i: (0, 0))
    return pl.pallas_call(
        functools.partial(_layer_tail_kernel, ff_chunk=ff_chunk),
        grid=(m // tm,),
        in_specs=[rows(k), resident(wp), rows(d), vec, vec, resident(wi), resident(wo), vec, vec],
        out_specs=rows(d),
        out_shape=jax.ShapeDtypeStruct((m, d), F32),
        scratch_shapes=[pltpu.VMEM((tm, d_ff), BF16)],
        compiler_params=_params("parallel"),
        name=name,
    )(a, wp, x, ln1[0].reshape(1, d), ln1[1].reshape(1, d), wi, wo, ln2[0].reshape(1, d), ln2[1].reshape(1, d))


def _softmax_pv(s_parts, v_parts):
    m = s_parts[0].max(axis=-1, keepdims=True)
    for s in s_parts[1:]:
        m = jnp.maximum(m, s.max(axis=-1, keepdims=True))
    acc, l = None, None
    for s, v in zip(s_parts, v_parts):
        e = jnp.exp(s - m)
        ls = e.sum(axis=-1, keepdims=True)
        pv = _dot(e.astype(BF16), v)
        acc = pv if acc is None else acc + pv
        l = ls if l is None else l + ls
    return acc / l


def _attn_prompt_kernel(q_ref, ka_ref, kb_ref, kc_ref, va_ref, vb_ref, vc_ref, w_ref, o_ref,
                        bias_ref, vext_ref):
    tile = pl.program_id(1)
    win = 3 * ATT_TILE

    @pl.when(tile <= 2)
    def _():
        qc = lax.broadcasted_iota(jnp.int32, (ATT_TILE, win), 0) // CHUNK
        kc = lax.broadcasted_iota(jnp.int32, (ATT_TILE, win), 1) // CHUNK
        valid = (kc >= qc) & (kc <= qc + BAND_CHUNKS) & (kc >= BAND_CHUNKS - tile * (ATT_TILE // CHUNK))
        for h in range(A_HEADS):
            w_rows = jnp.broadcast_to(w_ref[h:h + 1, :], (ATT_TILE, w_ref.shape[1]))
            toeplitz = pltpu.roll(w_rows, 0, 1, stride=1, stride_axis=0)[:, :win]
            bias_ref[h // 2, (h % 2) * ATT_TILE:(h % 2 + 1) * ATT_TILE, :] = jnp.where(valid, toeplitz, NEG_INF)

    n_groups = D_MODEL // LANES

    @pl.when(tile == 0)
    def _():
        vext_ref[:, :, :, LANES:] = jnp.ones((3, n_groups, ATT_TILE, LANES), BF16)

    k_refs, v_refs = (ka_ref, kb_ref, kc_ref), (va_ref, vb_ref, vc_ref)
    for j, v_ref in enumerate(v_refs):
        for p in range(n_groups):
            vext_ref[j, p, :, :LANES] = v_ref[0, :, p * LANES:(p + 1) * LANES]

    lane = lax.broadcasted_iota(jnp.int32, (ATT_TILE, LANES), 1)
    low_half = lane < A_HEAD_DIM
    for p in range(n_groups):
        cols = slice(p * LANES, (p + 1) * LANES)
        qp = q_ref[0, :, cols]
        zero = jnp.zeros_like(qp)
        q_pair = jnp.concatenate([jnp.where(low_half, qp, zero), jnp.where(low_half, zero, qp)], axis=0)
        s_parts = [_dot_nt(q_pair, k_ref[0, :, cols]) + bias_ref[p, :, j * ATT_TILE:(j + 1) * ATT_TILE]
                   for j, k_ref in enumerate(k_refs)]
        m = functools.reduce(jnp.maximum, [s.max(axis=-1, keepdims=True) for s in s_parts])
        pv = sum(_dot(jnp.exp(s - m).astype(BF16), vext_ref[j, p]) for j, s in enumerate(s_parts))
        pv = pv[:, :LANES] / pv[:, LANES:]
        o_ref[0, :, cols] = jnp.where(low_half, pv[:ATT_TILE], pv[ATT_TILE:]).astype(o_ref.dtype)


def _band_bias_vectors(rel_table):
    n_heads = rel_table.shape[1]
    win = 3 * ATT_TILE
    t = rel_table.T.astype(F32)
    u = jnp.concatenate([jnp.broadcast_to(t[:, 2 * REL_CLIP:], (n_heads, win - REL_CLIP)),
                         jnp.flip(t[:, 1:2 * REL_CLIP], axis=1),
                         jnp.broadcast_to(t[:, :1], (n_heads, REL_CLIP))], axis=1)
    assert u.shape[1] == ATT_TILE + win - 1
    return jnp.roll(jnp.pad(u, ((0, 0), (0, 1))), -(ATT_TILE - 1), axis=1)


def _attn_prompt(q, k, v, bias_vectors):
    bsz, t_len, d = q.shape
    assert t_len % ATT_TILE == 0 and BAND_PAST == 2 * ATT_TILE
    blk = (1, ATT_TILE, d)

    def past(n):
        return lambda b, i: (b, jnp.maximum(i - n, 0), 0)

    kv_specs = [pl.BlockSpec(blk, past(2)), pl.BlockSpec(blk, past(1)), pl.BlockSpec(blk, past(0))]
    return pl.pallas_call(
        _attn_prompt_kernel,
        grid=(bsz, t_len // ATT_TILE),
        in_specs=[pl.BlockSpec(blk, lambda b, i: (b, i, 0))] + kv_specs + kv_specs
                 + [pl.BlockSpec(bias_vectors.shape, lambda b, i: (0, 0))],
        out_specs=pl.BlockSpec(blk, lambda b, i: (b, i, 0)),
        out_shape=jax.ShapeDtypeStruct((bsz, t_len, d), BF16),
        scratch_shapes=[pltpu.VMEM((A_HEADS // 2, 2 * ATT_TILE, 3 * ATT_TILE), F32),
                        pltpu.VMEM((3, d // LANES, ATT_TILE, 2 * LANES), BF16)],
        compiler_params=_params("parallel", "arbitrary"),
        name="attn_prompt",
    )(q, k, k, k, v, v, v, bias_vectors)


def _attn_sample_kernel(q_ref, kn_ref, vn_ref, kc_ref, vc_ref, bc_ref, bn_ref, o_ref):
    t_len = q_ref.shape[1]
    lane = lax.broadcasted_iota(jnp.int32, (t_len, LANES), 1)
    low_half = lane < A_HEAD_DIM
    for p in range(D_MODEL // LANES):
        cols = slice(p * LANES, (p + 1) * LANES)
        qp = q_ref[0, :, cols].astype(BF16)
        kc = kc_ref[0, :, cols]
        vc = vc_ref[0, :, cols]
        kn = kn_ref[0, :, cols].astype(BF16)
        vn = vn_ref[0, :, cols].astype(BF16)
        outs = []
        for hh in range(2):
            qm = jnp.where(low_half if hh == 0 else jnp.logical_not(low_half), qp, jnp.zeros_like(qp))
            s_c = _dot_nt(qm, kc) + bc_ref[2 * p + hh]
            s_n = _dot_nt(qm, kn) + bn_ref[2 * p + hh]
            outs.append(_softmax_pv([s_c, s_n], [vc, vn]))
        o_ref[0, :, cols] = jnp.where(low_half, outs[0], outs[1]).astype(o_ref.dtype)


def _attn_sample(q, k_new, v_new, cache_k, cache_v, bias_c, bias_n):
    bsz, t_len, d = q.shape
    n_cache = cache_k.shape[1]
    new_spec = pl.BlockSpec((1, t_len, d), lambda b: (b, 0, 0))
    cache_spec = pl.BlockSpec((1, n_cache, d), lambda b: (b, 0, 0))
    return pl.pallas_call(
        _attn_sample_kernel,
        grid=(bsz,),
        in_specs=[new_spec, new_spec, new_spec, cache_spec, cache_spec,
                  pl.BlockSpec(bias_c.shape, lambda b: (0, 0, 0)),
                  pl.BlockSpec(bias_n.shape, lambda b: (0, 0, 0))],
        out_specs=new_spec,
        out_shape=jax.ShapeDtypeStruct((bsz, t_len, d), BF16),
        compiler_params=_params("parallel"),
        name="attn_sample",
    )(q, k_new, v_new, cache_k, cache_v, bias_c, bias_n)


def _log_sigmoid(z):
    return jnp.minimum(z, 0.0) - jnp.log(1.0 + jnp.exp(-jnp.abs(z)))


def _gla_project(rows, x_ref, w_ref, wlow_ref, wup_ref, bgk_ref, q_ref, k_ref, g_ref, v_ref, r_ref):
    xb = x_ref[0, rows, :].astype(BF16)
    n_chunk = 512
    q_ref[0, rows, :] = _dot(xb, w_ref[:, 0:B_QK]) * (B_KEY_DIM ** -0.5)
    k_ref[0, rows, :] = _dot(xb, w_ref[:, B_QK:2 * B_QK])
    for c in range(0, B_VD, n_chunk):
        v_ref[0, rows, c:c + n_chunk] = _dot(xb, w_ref[:, 2 * B_QK + c:2 * B_QK + c + n_chunk])
        r_ref[0, rows, c:c + n_chunk] = _dot(xb, w_ref[:, 2 * B_QK + B_VD + c:2 * B_QK + B_VD + c + n_chunk])
    low = _dot(xb, wlow_ref[...])
    z = _dot(low.astype(BF16), wup_ref[...]) + bgk_ref[...]
    g_ref[0, rows, :] = _log_sigmoid(z) / B_GATE_NORM


def _split3(x):
    hi = x.astype(BF16)
    r1 = x - hi.astype(F32)
    mid = r1.astype(BF16)
    lo = (r1 - mid.astype(F32)).astype(BF16)
    return hi, mid, lo


def _gla_finish(o, h, rows, r_ref, gain_ref, a_ref):
    vcols = slice(h * B_VAL_DIM, (h + 1) * B_VAL_DIM)
    o = o * lax.rsqrt(jnp.mean(o * o, axis=-1, keepdims=True) + GN_EPS) * gain_ref[:, vcols]
    r_h = r_ref[0, rows, vcols]
    a_ref[0, rows, vcols] = (o * (r_h * jax.nn.sigmoid(r_h))).astype(a_ref.dtype)


def _gla_rows_as_one_chunk(rows, b, q_ref, k_ref, v_ref, r_ref, gain_ref, a_ref, st_ref):
    n = b.shape[0]
    q, k = q_ref[0, rows, :], k_ref[0, rows, :]
    b_last = b[n - 1:n, :]
    qhat = (q * jnp.exp(b)).astype(BF16)
    kinv = (k * jnp.exp(-b)).astype(BF16)
    kdec = (k * jnp.exp(b_last - b)).astype(BF16)
    decay = jnp.exp(b_last)
    causal = lax.broadcasted_iota(jnp.int32, (n, n), 1) <= lax.broadcasted_iota(jnp.int32, (n, n), 0)
    for h in range(B_HEADS):
        kcols = slice(h * B_KEY_DIM, (h + 1) * B_KEY_DIM)
        vcols = slice(h * B_VAL_DIM, (h + 1) * B_VAL_DIM)
        a_mat = jnp.where(causal, _dot_nt(qhat[:, kcols], kinv[:, kcols]), 0.0).astype(BF16)
        v_h = v_ref[0, rows, vcols].astype(BF16)
        st = st_ref[0, h]
        o = _dot(a_mat, v_h) + _dot_nt(qhat[:, kcols], st.astype(BF16))
        st_ref[0, h] = st * decay[:, kcols] + _dot_tn(v_h, kdec[:, kcols])
        _gla_finish(o, h, rows, r_ref, gain_ref, a_ref)


def _gla_kernel(x_ref, w_ref, wlow_ref, wup_ref, bgk_ref, s0_ref, gain_ref, a_ref, st_ref,
                q_ref, k_ref, g_ref, v_ref, r_ref, st0_ref, *, chunk, n_chunks, wide_chunk):
    @pl.when(pl.program_id(1) == 0)
    def _():
        st_ref[...] = s0_ref[...]

    n = chunk * n_chunks
    project = functools.partial(_gla_project, x_ref=x_ref, w_ref=w_ref, wlow_ref=wlow_ref, wup_ref=wup_ref,
                                bgk_ref=bgk_ref, q_ref=q_ref, k_ref=k_ref, g_ref=g_ref, v_ref=v_ref, r_ref=r_ref)

    def by_sub_blocks():
        _gla_chunks_by_sub_blocks(q_ref, k_ref, g_ref, v_ref, r_ref, gain_ref, a_ref, st_ref,
                                  chunk=chunk, n_chunks=n_chunks)

    if not wide_chunk:
        project(slice(0, n))
        by_sub_blocks()
        return
    project(slice(0, n))
    st0_ref[...] = st_ref[...]
    tri = (lax.broadcasted_iota(jnp.int32, (wide_chunk, wide_chunk), 1)
           <= lax.broadcasted_iota(jnp.int32, (wide_chunk, wide_chunk), 0))
    tri = jnp.where(tri, 1.0, 0.0).astype(BF16)
    min_log_decay = None
    for r0 in range(0, n, wide_chunk):
        rows = slice(r0, r0 + wide_chunk)
        b = sum(_dot(tri, gp) for gp in _split3(g_ref[0, rows, :]))
        _gla_rows_as_one_chunk(rows, b, q_ref, k_ref, v_ref, r_ref, gain_ref, a_ref, st_ref)
        total = b[wide_chunk - 1:wide_chunk, :]
        min_log_decay = total if min_log_decay is None else jnp.minimum(min_log_decay, total)

    @pl.when(jnp.logical_not(jnp.min(min_log_decay) >= -GLA_MAX_LOG_DECAY))
    def _():
        st_ref[...] = st0_ref[...]
        by_sub_blocks()


def _gla_chunks_by_sub_blocks(q_ref, k_ref, g_ref, v_ref, r_ref, gain_ref, a_ref, st_ref, *, chunk, n_chunks):
    nsb = chunk // SUB_BLOCK
    row = lax.broadcasted_iota(jnp.int32, (chunk, chunk), 0)
    col = lax.broadcasted_iota(jnp.int32, (chunk, chunk), 1)
    blk_start = (row // SUB_BLOCK) * SUB_BLOCK
    tri_local = jnp.where((col <= row) & (col >= blk_start), 1.0, 0.0).astype(BF16)
    tri_before = jnp.where(col < blk_start, 1.0, 0.0).astype(BF16)
    rcol = lax.broadcasted_iota(jnp.int32, (SUB_BLOCK, chunk), 1)
    rrow = lax.broadcasted_iota(jnp.int32, (SUB_BLOCK, chunk), 0)

    def chunk_body(c, carry):
        row0 = pl.multiple_of(c * chunk, chunk)
        rows = pl.ds(row0, chunk)
        q = q_ref[0, rows, :]
        k = k_ref[0, rows, :]
        g_parts = _split3(g_ref[0, rows, :])
        bl = sum(_dot(tri_local, gp) for gp in g_parts)
        if nsb > 1:
            rr = sum(_dot(tri_before, gp) for gp in g_parts)
            b = bl + rr
        else:
            rr = None
            b = bl
        b_last = b[chunk - 1:chunk, :]
        qt = q * jnp.exp(bl)
        qhat = qt * jnp.exp(rr) if nsb > 1 else qt
        kdec = (k * jnp.exp(b_last - b)).astype(BF16)
        decay = jnp.exp(b_last)
        kb = k.astype(BF16)

        for h in range(B_HEADS):
            kcols = slice(h * B_KEY_DIM, (h + 1) * B_KEY_DIM)
            vcols = slice(h * B_VAL_DIM, (h + 1) * B_VAL_DIM)
            q_h, k_h, bl_h, b_h, qt_h = q[:, kcols], k[:, kcols], bl[:, kcols], b[:, kcols], qt[:, kcols]
            kb_h = kb[:, kcols]
            a_rows = []
            for i in range(nsb):
                sb = slice(i * SUB_BLOCK, (i + 1) * SUB_BLOCK)
                q_s, bl_s = q_h[sb], bl_h[sb]
                y = jnp.concatenate(
                    [q_s * jnp.exp(jnp.minimum(bl_s - bl_s[j:j + 1], 0.0)) for j in range(SUB_BLOCK)], axis=0)
                res = _dot_nt(y.astype(BF16), kb_h)
                a_i = jnp.zeros((SUB_BLOCK, chunk), F32)
                for j in range(SUB_BLOCK):
                    a_i = jnp.where(rcol == i * SUB_BLOCK + j, res[j * SUB_BLOCK:(j + 1) * SUB_BLOCK], a_i)
                a_i = jnp.where(rcol <= i * SUB_BLOCK + rrow, a_i, 0.0)
                if i > 0:
                    r_i = rr[i * SUB_BLOCK:i * SUB_BLOCK + 1, kcols]
                    kt = (k_h * jnp.exp(jnp.minimum(r_i - b_h, 0.0))).astype(BF16)
                    a_off = _dot_nt(qt_h[sb].astype(BF16), kt)
                    a_i = jnp.where(rcol < i * SUB_BLOCK, a_off, a_i)
                a_rows.append(a_i)
            a_mat = (jnp.concatenate(a_rows, axis=0) if nsb > 1 else a_rows[0]).astype(BF16)
            v_h = v_ref[0, rows, vcols].astype(BF16)
            st = st_ref[0, h]
            o = _dot(a_mat, v_h) + _dot_nt(qhat[:, kcols].astype(BF16), st.astype(BF16))
            st_ref[0, h] = st * decay[:, kcols] + _dot_tn(v_h, kdec[:, kcols])
            _gla_finish(o, h, rows, r_ref, gain_ref, a_ref)
        return carry

    lax.fori_loop(0, n_chunks, chunk_body, 0)


def _gla(x, w_main, w_low, w_up, b_gk, s0t, gain, chunk, chunks_per_step, wide_chunk, name="gla"):
    bsz, t_len, d = x.shape
    rows = chunk * chunks_per_step
    assert t_len % rows == 0
    const = lambda b, s: (0, 0)
    resident = lambda w: pl.BlockSpec(w.shape, const, pipeline_mode=pl.Buffered(1))
    st_spec = pl.BlockSpec((1, B_HEADS, B_VAL_DIM, B_KEY_DIM), lambda b, s: (b, 0, 0, 0))
    out_spec = pl.BlockSpec((1, rows, B_VD), lambda b, s: (b, s, 0))
    return pl.pallas_call(
        functools.partial(_gla_kernel, chunk=chunk, n_chunks=chunks_per_step, wide_chunk=wide_chunk),
        grid=(bsz, t_len // rows),
        in_specs=[pl.BlockSpec((1, rows, d), lambda b, s: (b, s, 0)),
                  resident(w_main), resident(w_low), resident(w_up), pl.BlockSpec((1, B_QK), const),
                  st_spec, pl.BlockSpec((1, B_VD), const)],
        out_specs=[out_spec, st_spec],
        out_shape=[jax.ShapeDtypeStruct((bsz, t_len, B_VD), BF16),
                   jax.ShapeDtypeStruct(s0t.shape, F32)],
        scratch_shapes=[pltpu.VMEM((1, rows, B_QK), F32)] * 3 + [pltpu.VMEM((1, rows, B_VD), F32)] * 2
                       + [pltpu.VMEM((1, B_HEADS, B_VAL_DIM, B_KEY_DIM), F32)],
        compiler_params=_params("parallel", "arbitrary"),
        name=name,
    )(x, w_main, w_low, w_up, b_gk.reshape(1, B_QK), s0t, gain.reshape(1, B_VD))


def _rel_bias(rel_table, q_pos, k_pos):
    d = jnp.clip(q_pos[:, None] - k_pos[None, :], -REL_CLIP, REL_CLIP) + REL_CLIP
    return jnp.transpose(rel_table[d], (2, 0, 1)).astype(F32)


def kernel(x_prompt, x_sample, cache_a_k, cache_a_v, state_b, w_in_a, rel_bias_a, w_out_a, w_in_b, w_gk_up_b,
           b_gk_b, gn_gain_b, w_out_b, w_ffn_in, w_ffn_out, ln1_g, ln1_b, ln2_g, ln2_b):
    bsz, t_len, d = x_prompt.shape
    dbsz, dt_len, _ = x_sample.shape
    n_cache = cache_a_k.shape[2]
    keep = min(BAND_PAST, t_len)
    xp = x_prompt.reshape(bsz * t_len, d)
    xs = x_sample.reshape(dbsz * dt_len, d)

    qscale = jnp.concatenate([jnp.full((d,), A_HEAD_DIM ** -0.5, F32), jnp.ones((2 * d,), F32)])
    w_qkv = (w_in_a[0] * qscale).astype(BF16)
    w_oa = w_out_a[0].astype(BF16)
    bias_p = _band_bias_vectors(rel_bias_a[0])
    bias_s = _rel_bias(rel_bias_a[0], n_cache + jnp.arange(dt_len), jnp.arange(n_cache + dt_len))

    q, k, v = _linear(xp, w_qkv, (d, d, d), BF16, name="qkv_prompt")
    x_tail = x_prompt[:, t_len - keep:].reshape(bsz * keep, d)
    k_tail, v_tail = _linear(x_tail, w_qkv[:, d:], (d, d), F32, name="kv_tail")
    att = _attn_prompt(q.reshape(bsz, t_len, d), k.reshape(bsz, t_len, d), v.reshape(bsz, t_len, d), bias_p)

    qs, ks, vs = _linear(xs, w_qkv, (d, d, d), F32, tm=128, name="qkv_sample")
    att_s = _attn_sample(qs.reshape(dbsz, dt_len, d), ks.reshape(dbsz, dt_len, d), vs.reshape(dbsz, dt_len, d),
                         cache_a_k[0].astype(BF16).reshape(dbsz, n_cache, d),
                         cache_a_v[0].astype(BF16).reshape(dbsz, n_cache, d),
                         bias_s[:, :, :n_cache], bias_s[:, :, n_cache:])

    wi0, wo0 = w_ffn_in[0].astype(BF16), w_ffn_out[0].astype(BF16)
    ln1, ln2 = (ln1_g[0], ln1_b[0]), (ln2_g[0], ln2_b[0])
    xp = _layer_tail(att.reshape(bsz * t_len, d), w_oa, xp, ln1, wi0, wo0, ln2, name="tail0_prompt")
    xs = _layer_tail(att_s.reshape(dbsz * dt_len, d), w_oa, xs, ln1, wi0, wo0, ln2, tm=128, name="tail0_sample")

    n_main = 2 * B_QK + 2 * B_VD
    w_main = w_in_b[0][:, :n_main].astype(BF16)
    w_low = jnp.pad(w_in_b[0][:, n_main:], ((0, 0), (0, LANES - B_GATE_RANK))).astype(BF16)
    w_up = jnp.pad(w_gk_up_b[0], ((0, LANES - B_GATE_RANK), (0, 0))).astype(BF16)
    w_ob = w_out_b[0].astype(BF16)

    s0 = jnp.zeros((bsz, B_HEADS, B_VAL_DIM, B_KEY_DIM), F32)
    a_p, st_p = _gla(xp.reshape(bsz, t_len, d), w_main, w_low, w_up, b_gk_b[0], s0, gn_gain_b[0],
                     CHUNK, min(GLA_STEP_ROWS, t_len) // CHUNK, GLA_WIDE_CHUNK, name="gla_prompt")
    a_s, st_s = _gla(xs.reshape(dbsz, dt_len, d), w_main, w_low, w_up, b_gk_b[0],
                     jnp.swapaxes(state_b[0], -1, -2), gn_gain_b[0], dt_len, 1, None, name="gla_sample")

    wi1, wo1 = w_ffn_in[1].astype(BF16), w_ffn_out[1].astype(BF16)
    ln1, ln2 = (ln1_g[1], ln1_b[1]), (ln2_g[1], ln2_b[1])
    xp = _layer_tail(a_p.reshape(bsz * t_len, B_VD), w_ob, xp, ln1, wi1, wo1, ln2, name="tail1_prompt")
    xs = _layer_tail(a_s.reshape(dbsz * dt_len, B_VD), w_ob, xs, ln1, wi1, wo1, ln2, tm=128, name="tail1_sample")

    heads = lambda a, n, t: a.reshape(1, n, t, A_HEADS, A_HEAD_DIM)
    return (xp.reshape(bsz, t_len, d), xs.reshape(dbsz, dt_len, d),
            heads(k_tail, bsz, keep), heads(v_tail, bsz, keep), jnp.swapaxes(st_p, -1, -2)[None],
            heads(ks, dbsz, dt_len), heads(vs, dbsz, dt_len), jnp.swapaxes(st_s, -1, -2)[None])
```

```python
import functools

import jax
import jax.numpy as jnp
from jax import lax
from jax.experimental import pallas as pl
from jax.experimental.pallas import tpu as pltpu

F32 = jnp.float32
BF16 = jnp.bfloat16

D_MODEL = 1024
CHUNK = 64
A_HEADS = 16
A_HEAD_DIM = D_MODEL // A_HEADS
BAND_CHUNKS = 8
BAND_PAST = BAND_CHUNKS * CHUNK
REL_CLIP = 128
B_HEADS = 4
B_KEY_DIM = D_MODEL // 2 // B_HEADS
B_VAL_DIM = D_MODEL // B_HEADS
B_QK = B_HEADS * B_KEY_DIM
B_VD = B_HEADS * B_VAL_DIM
B_GATE_RANK = 16
B_GATE_NORM = 16.0
D_FF = -(-8 * D_MODEL // 768) * 256
DEPTH = 2
ALPHA = (2.0 * DEPTH) ** 0.25
LN_EPS = 1e-5
GN_EPS = 1e-6
NEG_INF = -1e30

LANES = 128
SUB_BLOCK = 16
ATT_TILE = 4 * CHUNK
GLA_STEP_ROWS = 1024
GLA_WIDE_CHUNK = 256
GLA_MAX_LOG_DECAY = 60.0
VMEM_LIMIT = 56 * 1024 * 1024


def _dot(a, b):
    return jnp.dot(a, b, preferred_element_type=F32)


def _dot_nt(a, b):
    return lax.dot_general(a, b, (((1,), (1,)), ((), ())), preferred_element_type=F32)


def _dot_tn(a, b):
    return lax.dot_general(a, b, (((0,), (0,)), ((), ())), preferred_element_type=F32)


def _params(*sem):
    return pltpu.CompilerParams(dimension_semantics=sem, vmem_limit_bytes=VMEM_LIMIT)


def _row_tile(m, pref):
    t = min(m, pref)
    assert m % t == 0
    return t


def _linear_kernel(x_ref, w_ref, *o_refs, n_chunk):
    xb = x_ref[...].astype(BF16)
    col = 0
    for o_ref in o_refs:
        nj = o_ref.shape[1]
        for c in range(0, nj, n_chunk):
            o_ref[:, c:c + n_chunk] = _dot(xb, w_ref[:, col + c:col + c + n_chunk]).astype(o_ref.dtype)
        col += nj


def _linear(x, w, widths, dtype, tm=512, name="linear"):
    m, k = x.shape
    tm = _row_tile(m, tm)
    n_chunk = 512
    assert all(wd % n_chunk == 0 for wd in widths) and sum(widths) == w.shape[1]
    return pl.pallas_call(
        functools.partial(_linear_kernel, n_chunk=n_chunk),
        grid=(m // tm,),
        in_specs=[pl.BlockSpec((tm, k), lambda i: (i, 0)),
                  pl.BlockSpec(w.shape, lambda i: (0, 0))],
        out_specs=[pl.BlockSpec((tm, wd), lambda i: (i, 0)) for wd in widths],
        out_shape=[jax.ShapeDtypeStruct((m, wd), dtype) for wd in widths],
        compiler_params=_params("parallel"),
        name=name,
    )(x, w)


def _res_ln(x, y, g, b):
    t = ALPHA * x + y
    mu = jnp.mean(t, axis=-1, keepdims=True)
    d = t - mu
    var = jnp.mean(d * d, axis=-1, keepdims=True)
    return d * lax.rsqrt(var + LN_EPS) * g + b


def _layer_tail_kernel(a_ref, wp_ref, x_ref, g1_ref, b1_ref, wi_ref, wo_ref, g2_ref, b2_ref, o_ref, *h_refs,
                       ff_chunk):
    n = len(h_refs)
    tm = x_ref.shape[0]
    d_ff = h_refs[0].shape[1]
    rows = [slice(i * (tm // n), (i + 1) * (tm // n)) for i in range(n)]
    x1, y = [None] * n, [None] * n

    def ln1(i):
        x1[i] = _res_ln(x_ref[rows[i], :], _dot(a_ref[rows[i], :], wp_ref[...]), g1_ref[...], b1_ref[...])

    def ffn(i):
        xb = x1[i].astype(BF16)
        for c in range(0, d_ff, ff_chunk):
            gate = _dot(xb, wi_ref[:, c:c + ff_chunk])
            up = _dot(xb, wi_ref[:, d_ff + c:d_ff + c + ff_chunk])
            h_refs[i][:, c:c + ff_chunk] = (gate * jax.nn.sigmoid(gate) * up).astype(BF16)
        y[i] = _dot(h_refs[i][...], wo_ref[...])

    def ln2(i):
        o_ref[rows[i], :] = _res_ln(x1[i], y[i], g2_ref[...], b2_ref[...])

    ln1(0)
    for i in range(n):
        if i + 1 < n:
            ln1(i + 1)
        ffn(i)
        if i > 0:
            ln2(i - 1)
    ln2(n - 1)


def _layer_tail(a, wp, x, ln1, wi, wo, ln2, tm=1024, name="layer_tail"):
    m, k = a.shape
    d = x.shape[1]
    d_ff = wo.shape[0]
    tm = _row_tile(m, tm)
    ff_chunk = 256
    row_groups = 4 if tm % 1024 == 0 else 1
    assert d_ff % ff_chunk == 0
    rows = lambda width: pl.BlockSpec((tm, width), lambda i: (i, 0))
    resident = lambda w: pl.BlockSpec(w.shape, lambda i: (0, 0), pipeline_mode=pl.Buffered(1))
    vec = pl.BlockSpec((1, d), lambda i: (0, 0))
    return pl.pallas_call(
        functools.partial(_layer_tail_kernel, ff_chunk=ff_chunk),
        grid=(m // tm,),
        in_specs=[rows(k), resident(wp), rows(d), vec, vec, resident(wi), resident(wo), vec, vec],
        out_specs=rows(d),
        out_shape=jax.ShapeDtypeStruct((m, d), F32),
        scratch_shapes=[pltpu.VMEM((tm // row_groups, d_ff), BF16)] * row_groups,
        compiler_params=_params("parallel"),
        name=name,
    )(a, wp, x, ln1[0].reshape(1, d), ln1[1].reshape(1, d), wi, wo, ln2[0].reshape(1, d), ln2[1].reshape(1, d))


def _softmax_pv(s_parts, v_parts):
    m = s_parts[0].max(axis=-1, keepdims=True)
    for s in s_parts[1:]:
        m = jnp.maximum(m, s.max(axis=-1, keepdims=True))
    acc, l = None, None
    for s, v in zip(s_parts, v_parts):
        e = jnp.exp(s - m)
        ls = e.sum(axis=-1, keepdims=True)
        pv = _dot(e.astype(BF16), v)
        acc = pv if acc is None else acc + pv
        l = ls if l is None else l + ls
    return acc / l


def _attn_prompt_kernel(q_ref, ka_ref, kb_ref, kc_ref, va_ref, vb_ref, vc_ref, w_ref, o_ref,
                        bias_ref, vext_ref):
    tile = pl.program_id(1)
    win = 3 * ATT_TILE

    @pl.when(tile <= 2)
    def _():
        qc = lax.broadcasted_iota(jnp.int32, (ATT_TILE, win), 0) // CHUNK
        kc = lax.broadcasted_iota(jnp.int32, (ATT_TILE, win), 1) // CHUNK
        valid = (kc >= qc) & (kc <= qc + BAND_CHUNKS) & (kc >= BAND_CHUNKS - tile * (ATT_TILE // CHUNK))
        for h in range(A_HEADS):
            w_rows = jnp.broadcast_to(w_ref[h:h + 1, :], (ATT_TILE, w_ref.shape[1]))
            toeplitz = pltpu.roll(w_rows, 0, 1, stride=1, stride_axis=0)[:, :win]
            bias_ref[h // 2, (h % 2) * ATT_TILE:(h % 2 + 1) * ATT_TILE, :] = jnp.where(valid, toeplitz, NEG_INF)

    n_groups = D_MODEL // LANES

    @pl.when(tile == 0)
    def _():
        vext_ref[:, :, :, LANES:] = jnp.ones((3, n_groups, ATT_TILE, LANES), BF16)

    k_refs, v_refs = (ka_ref, kb_ref, kc_ref), (va_ref, vb_ref, vc_ref)
    for j, v_ref in enumerate(v_refs):
        for p in range(n_groups):
            vext_ref[j, p, :, :LANES] = v_ref[0, :, p * LANES:(p + 1) * LANES]

    lane = lax.broadcasted_iota(jnp.int32, (ATT_TILE, LANES), 1)
    low_half = lane < A_HEAD_DIM
    for p in range(n_groups):
        cols = slice(p * LANES, (p + 1) * LANES)
        qp = q_ref[0, :, cols]
        zero = jnp.zeros_like(qp)
        q_pair = jnp.concatenate([jnp.where(low_half, qp, zero), jnp.where(low_half, zero, qp)], axis=0)
        s_parts = [_dot_nt(q_pair, k_ref[0, :, cols]) + bias_ref[p, :, j * ATT_TILE:(j + 1) * ATT_TILE]
                   for j, k_ref in enumerate(k_refs)]
        m = functools.reduce(jnp.maximum, [s.max(axis=-1, keepdims=True) for s in s_parts])
        pv = sum(_dot(jnp.exp(s - m).astype(BF16), vext_ref[j, p]) for j, s in enumerate(s_parts))
        pv = pv[:, :LANES] / pv[:, LANES:]
        o_ref[0, :, cols] = jnp.where(low_half, pv[:ATT_TILE], pv[ATT_TILE:]).astype(o_ref.dtype)


def _band_bias_vectors(rel_table):
    n_heads = rel_table.shape[1]
    win = 3 * ATT_TILE
    t = rel_table.T.astype(F32)
    u = jnp.concatenate([jnp.broadcast_to(t[:, 2 * REL_CLIP:], (n_heads, win - REL_CLIP)),
                         jnp.flip(t[:, 1:2 * REL_CLIP], axis=1),
                         jnp.broadcast_to(t[:, :1], (n_heads, REL_CLIP))], axis=1)
    assert u.shape[1] == ATT_TILE + win - 1
    return jnp.roll(jnp.pad(u, ((0, 0), (0, 1))), -(ATT_TILE - 1), axis=1)


def _attn_prompt(q, k, v, bias_vectors):
    bsz, t_len, d = q.shape
    assert t_len % ATT_TILE == 0 and BAND_PAST == 2 * ATT_TILE
    blk = (1, ATT_TILE, d)

    def past(n):
        return lambda b, i: (b, jnp.maximum(i - n, 0), 0)

    kv_specs = [pl.BlockSpec(blk, past(2)), pl.BlockSpec(blk, past(1)), pl.BlockSpec(blk, past(0))]
    return pl.pallas_call(
        _attn_prompt_kernel,
        grid=(bsz, t_len // ATT_TILE),
        in_specs=[pl.BlockSpec(blk, lambda b, i: (b, i, 0))] + kv_specs + kv_specs
                 + [pl.BlockSpec(bias_vectors.shape, lambda b, i: (0, 0))],
        out_specs=pl.BlockSpec(blk, lambda b, i: (b, i, 0)),
        out_shape=jax.ShapeDtypeStruct((bsz, t_len, d), BF16),
        scratch_shapes=[pltpu.VMEM((A_HEADS // 2, 2 * ATT_TILE, 3 * ATT_TILE), F32),
                        pltpu.VMEM((3, d // LANES, ATT_TILE, 2 * LANES), BF16)],
        compiler_params=_params("parallel", "arbitrary"),
        name="attn_prompt",
    )(q, k, k, k, v, v, v, bias_vectors)


def _attn_sample_kernel(q_ref, kn_ref, vn_ref, kc_ref, vc_ref, w_ref, o_ref):
    t_len = q_ref.shape[1]
    n_cache = kc_ref.shape[1]
    lane = lax.broadcasted_iota(jnp.int32, (t_len, LANES), 1)
    low_half = lane < A_HEAD_DIM
    bias = [pltpu.roll(jnp.broadcast_to(w_ref[h:h + 1, :], (t_len, w_ref.shape[1])), 0, 1, stride=1, stride_axis=0)
            for h in range(A_HEADS)]
    for p in range(D_MODEL // LANES):
        cols = slice(p * LANES, (p + 1) * LANES)
        qp = q_ref[0, :, cols].astype(BF16)
        kc = kc_ref[0, :, cols]
        vc = vc_ref[0, :, cols]
        kn = kn_ref[0, :, cols].astype(BF16)
        vn = vn_ref[0, :, cols].astype(BF16)
        outs = []
        for hh in range(2):
            qm = jnp.where(low_half if hh == 0 else jnp.logical_not(low_half), qp, jnp.zeros_like(qp))
            s_c = _dot_nt(qm, kc) + bias[2 * p + hh][:, :n_cache]
            s_n = _dot_nt(qm, kn) + bias[2 * p + hh][:, n_cache:n_cache + t_len]
            outs.append(_softmax_pv([s_c, s_n], [vc, vn]))
        o_ref[0, :, cols] = jnp.where(low_half, outs[0], outs[1]).astype(o_ref.dtype)


def _sample_bias_vectors(rel_table, n_cache, t_len):
    assert t_len - 1 <= REL_CLIP <= n_cache + t_len
    n_heads = rel_table.shape[1]
    t = rel_table.T.astype(F32)
    u = jnp.concatenate([jnp.broadcast_to(t[:, 2 * REL_CLIP:], (n_heads, n_cache + t_len - REL_CLIP)),
                         jnp.flip(t[:, REL_CLIP - t_len + 1:2 * REL_CLIP], axis=1)], axis=1)
    assert u.shape[1] == n_cache + 2 * t_len - 1
    width = -(-u.shape[1] // LANES) * LANES
    return jnp.roll(jnp.pad(u, ((0, 0), (0, width - u.shape[1]))), -(t_len - 1), axis=1)


def _attn_sample(q, k_new, v_new, cache_k, cache_v, bias_vectors):
    bsz, t_len, d = q.shape
    n_cache = cache_k.shape[1]
    new_spec = pl.BlockSpec((1, t_len, d), lambda b: (b, 0, 0))
    cache_spec = pl.BlockSpec((1, n_cache, d), lambda b: (b, 0, 0))
    return pl.pallas_call(
        _attn_sample_kernel,
        grid=(bsz,),
        in_specs=[new_spec, new_spec, new_spec, cache_spec, cache_spec,
                  pl.BlockSpec(bias_vectors.shape, lambda b: (0, 0))],
        out_specs=new_spec,
        out_shape=jax.ShapeDtypeStruct((bsz, t_len, d), BF16),
        compiler_params=_params("parallel"),
        name="attn_sample",
    )(q, k_new, v_new, cache_k, cache_v, bias_vectors)


def _log_sigmoid(z):
    return jnp.minimum(z, 0.0) - jnp.log(1.0 + jnp.exp(-jnp.abs(z)))


def _gla_project(rows, x_ref, w_ref, wlow_ref, wup_ref, bgk_ref, q_ref, k_ref, g_ref, v_ref, r_ref):
    xb = x_ref[0, rows, :].astype(BF16)
    n_chunk = 512
    q_ref[0, rows, :] = _dot(xb, w_ref[:, 0:B_QK]) * (B_KEY_DIM ** -0.5)
    k_ref[0, rows, :] = _dot(xb, w_ref[:, B_QK:2 * B_QK])
    for c in range(0, B_VD, n_chunk):
        v_ref[0, rows, c:c + n_chunk] = _dot(xb, w_ref[:, 2 * B_QK + c:2 * B_QK + c + n_chunk])
        r_ref[0, rows, c:c + n_chunk] = _dot(xb, w_ref[:, 2 * B_QK + B_VD + c:2 * B_QK + B_VD + c + n_chunk])
    low = _dot(xb, wlow_ref[...])
    z = _dot(low.astype(BF16), wup_ref[...]) + bgk_ref[...]
    g_ref[0, rows, :] = _log_sigmoid(z) / B_GATE_NORM


def _split3(x):
    hi = x.astype(BF16)
    r1 = x - hi.astype(F32)
    mid = r1.astype(BF16)
    lo = (r1 - mid.astype(F32)).astype(BF16)
    return hi, mid, lo


def _gla_finish(o, h, rows, r_ref, gain_ref, a_ref):
    vcols = slice(h * B_VAL_DIM, (h + 1) * B_VAL_DIM)
    o = o * lax.rsqrt(jnp.mean(o * o, axis=-1, keepdims=True) + GN_EPS) * gain_ref[:, vcols]
    r_h = r_ref[0, rows, vcols]
    a_ref[0, rows, vcols] = (o * (r_h * jax.nn.sigmoid(r_h))).astype(a_ref.dtype)


def _gla_rows_as_one_chunk(rows, b, q_ref, k_ref, v_ref, r_ref, gain_ref, a_ref, st_ref):
    n = b.shape[0]
    q, k = q_ref[0, rows, :], k_ref[0, rows, :]
    b_last = b[n - 1:n, :]
    qhat = (q * jnp.exp(b)).astype(BF16)
    kinv = (k * jnp.exp(-b)).astype(BF16)
    kdec = (k * jnp.exp(b_last - b)).astype(BF16)
    decay = jnp.exp(b_last)
    causal = lax.broadcasted_iota(jnp.int32, (n, n), 1) <= lax.broadcasted_iota(jnp.int32, (n, n), 0)
    for h in range(B_HEADS):
        kcols = slice(h * B_KEY_DIM, (h + 1) * B_KEY_DIM)
        vcols = slice(h * B_VAL_DIM, (h + 1) * B_VAL_DIM)
        a_mat = jnp.where(causal, _dot_nt(qhat[:, kcols], kinv[:, kcols]), 0.0).astype(BF16)
        v_h = v_ref[0, rows, vcols].astype(BF16)
        st = st_ref[0, h]
        o = _dot(a_mat, v_h) + _dot_nt(qhat[:, kcols], st.astype(BF16))
        st_ref[0, h] = st * decay[:, kcols] + _dot_tn(v_h, kdec[:, kcols])
        _gla_finish(o, h, rows, r_ref, gain_ref, a_ref)


def _gla_kernel(x_ref, w_ref, wlow_ref, wup_ref, bgk_ref, s0_ref, gain_ref, a_ref, st_ref,
                q_ref, k_ref, g_ref, v_ref, r_ref, st0_ref, *, chunk, n_chunks, wide_chunk):
    @pl.when(pl.program_id(1) == 0)
    def _():
        st_ref[...] = s0_ref[...]

    n = chunk * n_chunks
    project = functools.partial(_gla_project, x_ref=x_ref, w_ref=w_ref, wlow_ref=wlow_ref, wup_ref=wup_ref,
                                bgk_ref=bgk_ref, q_ref=q_ref, k_ref=k_ref, g_ref=g_ref, v_ref=v_ref, r_ref=r_ref)

    def by_sub_blocks():
        _gla_chunks_by_sub_blocks(q_ref, k_ref, g_ref, v_ref, r_ref, gain_ref, a_ref, st_ref,
                                  chunk=chunk, n_chunks=n_chunks)

    if not wide_chunk:
        project(slice(0, n))
        by_sub_blocks()
        return
    project(slice(0, n))
    st0_ref[...] = st_ref[...]
    tri = (lax.broadcasted_iota(jnp.int32, (wide_chunk, wide_chunk), 1)
           <= lax.broadcasted_iota(jnp.int32, (wide_chunk, wide_chunk), 0))
    tri = jnp.where(tri, 1.0, 0.0).astype(BF16)
    min_log_decay = None
    for r0 in range(0, n, wide_chunk):
        rows = slice(r0, r0 + wide_chunk)
        b = sum(_dot(tri, gp) for gp in _split3(g_ref[0, rows, :]))
        _gla_rows_as_one_chunk(rows, b, q_ref, k_ref, v_ref, r_ref, gain_ref, a_ref, st_ref)
        total = b[wide_chunk - 1:wide_chunk, :]
        min_log_decay = total if min_log_decay is None else jnp.minimum(min_log_decay, total)

    @pl.when(jnp.logical_not(jnp.min(min_log_decay) >= -GLA_MAX_LOG_DECAY))
    def _():
        st_ref[...] = st0_ref[...]
        by_sub_blocks()


def _gla_chunks_by_sub_blocks(q_ref, k_ref, g_ref, v_ref, r_ref, gain_ref, a_ref, st_ref, *, chunk, n_chunks):
    nsb = chunk // SUB_BLOCK
    row = lax.broadcasted_iota(jnp.int32, (chunk, chunk), 0)
    col = lax.broadcasted_iota(jnp.int32, (chunk, chunk), 1)
    blk_start = (row // SUB_BLOCK) * SUB_BLOCK
    tri_local = jnp.where((col <= row) & (col >= blk_start), 1.0, 0.0).astype(BF16)
    tri_before = jnp.where(col < blk_start, 1.0, 0.0).astype(BF16)
    rcol = lax.broadcasted_iota(jnp.int32, (SUB_BLOCK, chunk), 1)
    rrow = lax.broadcasted_iota(jnp.int32, (SUB_BLOCK, chunk), 0)

    def chunk_body(c, carry):
        row0 = pl.multiple_of(c * chunk, chunk)
        rows = pl.ds(row0, chunk)
        q = q_ref[0, rows, :]
        k = k_ref[0, rows, :]
        g_parts = _split3(g_ref[0, rows, :])
        bl = sum(_dot(tri_local, gp) for gp in g_parts)
        if nsb > 1:
            rr = sum(_dot(tri_before, gp) for gp in g_parts)
            b = bl + rr
        else:
            rr = None
            b = bl
        b_last = b[chunk - 1:chunk, :]
        qt = q * jnp.exp(bl)
        qhat = qt * jnp.exp(rr) if nsb > 1 else qt
        kdec = (k * jnp.exp(b_last - b)).astype(BF16)
        decay = jnp.exp(b_last)
        kb = k.astype(BF16)

        for h in range(B_HEADS):
            kcols = slice(h * B_KEY_DIM, (h + 1) * B_KEY_DIM)
            vcols = slice(h * B_VAL_DIM, (h + 1) * B_VAL_DIM)
            q_h, k_h, bl_h, b_h, qt_h = q[:, kcols], k[:, kcols], bl[:, kcols], b[:, kcols], qt[:, kcols]
            kb_h = kb[:, kcols]
            a_rows = []
            for i in range(nsb):
                sb = slice(i * SUB_BLOCK, (i + 1) * SUB_BLOCK)
                q_s, bl_s = q_h[sb], bl_h[sb]
                y = jnp.concatenate(
                    [q_s * jnp.exp(jnp.minimum(bl_s - bl_s[j:j + 1], 0.0)) for j in range(SUB_BLOCK)], axis=0)
                res = _dot_nt(y.astype(BF16), kb_h)
                a_i = jnp.zeros((SUB_BLOCK, chunk), F32)
                for j in range(SUB_BLOCK):
                    a_i = jnp.where(rcol == i * SUB_BLOCK + j, res[j * SUB_BLOCK:(j + 1) * SUB_BLOCK], a_i)
                a_i = jnp.where(rcol <= i * SUB_BLOCK + rrow, a_i, 0.0)
                if i > 0:
                    r_i = rr[i * SUB_BLOCK:i * SUB_BLOCK + 1, kcols]
                    kt = (k_h * jnp.exp(jnp.minimum(r_i - b_h, 0.0))).astype(BF16)
                    a_off = _dot_nt(qt_h[sb].astype(BF16), kt)
                    a_i = jnp.where(rcol < i * SUB_BLOCK, a_off, a_i)
                a_rows.append(a_i)
            a_mat = (jnp.concatenate(a_rows, axis=0) if nsb > 1 else a_rows[0]).astype(BF16)
            v_h = v_ref[0, rows, vcols].astype(BF16)
            st = st_ref[0, h]
            o = _dot(a_mat, v_h) + _dot_nt(qhat[:, kcols].astype(BF16), st.astype(BF16))
            st_ref[0, h] = st * decay[:, kcols] + _dot_tn(v_h, kdec[:, kcols])
            _gla_finish(o, h, rows, r_ref, gain_ref, a_ref)
        return carry

    lax.fori_loop(0, n_chunks, chunk_body, 0)


def _gla(x, w_main, w_low, w_up, b_gk, s0t, gain, chunk, chunks_per_step, wide_chunk, name="gla"):
    bsz, t_len, d = x.shape
    rows = chunk * chunks_per_step
    assert t_len % rows == 0
    const = lambda b, s: (0, 0)
    resident = lambda w: pl.BlockSpec(w.shape, const, pipeline_mode=pl.Buffered(1))
    st_spec = pl.BlockSpec((1, B_HEADS, B_VAL_DIM, B_KEY_DIM), lambda b, s: (b, 0, 0, 0))
    out_spec = pl.BlockSpec((1, rows, B_VD), lambda b, s: (b, s, 0))
    return pl.pallas_call(
        functools.partial(_gla_kernel, chunk=chunk, n_chunks=chunks_per_step, wide_chunk=wide_chunk),
        grid=(bsz, t_len // rows),
        in_specs=[pl.BlockSpec((1, rows, d), lambda b, s: (b, s, 0)),
                  resident(w_main), resident(w_low), resident(w_up), pl.BlockSpec((1, B_QK), const),
                  st_spec, pl.BlockSpec((1, B_VD), const)],
        out_specs=[out_spec, st_spec],
        out_shape=[jax.ShapeDtypeStruct((bsz, t_len, B_VD), BF16),
                   jax.ShapeDtypeStruct(s0t.shape, F32)],
        scratch_shapes=[pltpu.VMEM((1, rows, B_QK), F32)] * 3 + [pltpu.VMEM((1, rows, B_VD), F32)] * 2
                       + [pltpu.VMEM((1, B_HEADS, B_VAL_DIM, B_KEY_DIM), F32)],
        compiler_params=_params("parallel", "arbitrary"),
        name=name,
    )(x, w_main, w_low, w_up, b_gk.reshape(1, B_QK), s0t, gain.reshape(1, B_VD))


def kernel(x_prompt, x_sample, cache_a_k, cache_a_v, state_b, w_in_a, rel_bias_a, w_out_a, w_in_b, w_gk_up_b,
           b_gk_b, gn_gain_b, w_out_b, w_ffn_in, w_ffn_out, ln1_g, ln1_b, ln2_g, ln2_b):
    bsz, t_len, d = x_prompt.shape
    dbsz, dt_len, _ = x_sample.shape
    n_cache = cache_a_k.shape[2]
    keep = min(BAND_PAST, t_len)
    xp = x_prompt.reshape(bsz * t_len, d)
    xs = x_sample.reshape(dbsz * dt_len, d)

    qscale = jnp.concatenate([jnp.full((d,), A_HEAD_DIM ** -0.5, F32), jnp.ones((2 * d,), F32)])
    w_qkv = (w_in_a[0] * qscale).astype(BF16)
    w_oa = w_out_a[0].astype(BF16)
    bias_p = _band_bias_vectors(rel_bias_a[0])
    bias_s = _sample_bias_vectors(rel_bias_a[0], n_cache, dt_len)

    q, k, v = _linear(xp, w_qkv, (d, d, d), BF16, name="qkv_prompt")
    x_tail = x_prompt[:, t_len - keep:].reshape(bsz * keep, d)
    k_tail, v_tail = _linear(x_tail, w_qkv[:, d:], (d, d), F32, name="kv_tail")
    att = _attn_prompt(q.reshape(bsz, t_len, d), k.reshape(bsz, t_len, d), v.reshape(bsz, t_len, d), bias_p)

    qs, ks, vs = _linear(xs, w_qkv, (d, d, d), F32, tm=128, name="qkv_sample")
    att_s = _attn_sample(qs.reshape(dbsz, dt_len, d), ks.reshape(dbsz, dt_len, d), vs.reshape(dbsz, dt_len, d),
                         cache_a_k[0].astype(BF16).reshape(dbsz, n_cache, d),
                         cache_a_v[0].astype(BF16).reshape(dbsz, n_cache, d),
                         bias_s)

    wi0, wo0 = w_ffn_in[0].astype(BF16), w_ffn_out[0].astype(BF16)
    ln1, ln2 = (ln1_g[0], ln1_b[0]), (ln2_g[0], ln2_b[0])
    xp = _layer_tail(att.reshape(bsz * t_len, d), w_oa, xp, ln1, wi0, wo0, ln2, name="tail0_prompt")
    xs = _layer_tail(att_s.reshape(dbsz * dt_len, d), w_oa, xs, ln1, wi0, wo0, ln2, tm=128, name="tail0_sample")

    n_main = 2 * B_QK + 2 * B_VD
    w_main = w_in_b[0][:, :n_main].astype(BF16)
    w_low = jnp.pad(w_in_b[0][:, n_main:], ((0, 0), (0, LANES - B_GATE_RANK))).astype(BF16)
    w_up = jnp.pad(w_gk_up_b[0], ((0, LANES - B_GATE_RANK), (0, 0))).astype(BF16)
    w_ob = w_out_b[0].astype(BF16)

    s0 = jnp.zeros((bsz, B_HEADS, B_VAL_DIM, B_KEY_DIM), F32)
    a_p, st_p = _gla(xp.reshape(bsz, t_len, d), w_main, w_low, w_up, b_gk_b[0], s0, gn_gain_b[0],
                     CHUNK, min(GLA_STEP_ROWS, t_len) // CHUNK, GLA_WIDE_CHUNK, name="gla_prompt")
    a_s, st_s = _gla(xs.reshape(dbsz, dt_len, d), w_main, w_low, w_up, b_gk_b[0],
                     jnp.swapaxes(state_b[0], -1, -2), gn_gain_b[0], dt_len, 1, None, name="gla_sample")

    wi1, wo1 = w_ffn_in[1].astype(BF16), w_ffn_out[1].astype(BF16)
    ln1, ln2 = (ln1_g[1], ln1_b[1]), (ln2_g[1], ln2_b[1])
    xp = _layer_tail(a_p.reshape(bsz * t_len, B_VD), w_ob, xp, ln1, wi1, wo1, ln2, name="tail1_prompt")
    xs = _layer_tail(a_s.reshape(dbsz * dt_len, B_VD), w_ob, xs, ln1, wi1, wo1, ln2, tm=128, name="tail1_sample")

    heads = lambda a, n, t: a.reshape(1, n, t, A_HEADS, A_HEAD_DIM)
    return (xp.reshape(bsz, t_len, d), xs.reshape(dbsz, dt_len, d),
            heads(k_tail, bsz, keep), heads(v_tail, bsz, keep), jnp.swapaxes(st_p, -1, -2)[None],
            heads(ks, dbsz, dt_len), heads(vs, dbsz, dt_len), jnp.swapaxes(st_s, -1, -2)[None])
```

```python
import functools

import jax
import jax.numpy as jnp
from jax import lax
from jax.experimental import pallas as pl
from jax.experimental.pallas import tpu as pltpu

F32 = jnp.float32
BF16 = jnp.bfloat16

D_MODEL = 1024
CHUNK = 64
A_HEADS = 16
A_HEAD_DIM = D_MODEL // A_HEADS
BAND_CHUNKS = 8
BAND_PAST = BAND_CHUNKS * CHUNK
REL_CLIP = 128
B_HEADS = 4
B_KEY_DIM = D_MODEL // 2 // B_HEADS
B_VAL_DIM = D_MODEL // B_HEADS
B_QK = B_HEADS * B_KEY_DIM
B_VD = B_HEADS * B_VAL_DIM
B_GATE_RANK = 16
B_GATE_NORM = 16.0
D_FF = -(-8 * D_MODEL // 768) * 256
DEPTH = 2
ALPHA = (2.0 * DEPTH) ** 0.25
LN_EPS = 1e-5
GN_EPS = 1e-6
NEG_INF = -1e30

LANES = 128
SUB_BLOCK = 16
ATT_TILE = 4 * CHUNK
GLA_STEP_ROWS = 1024
GLA_WIDE_CHUNK = 256
GLA_MAX_LOG_DECAY = 60.0
VMEM_LIMIT = 56 * 1024 * 1024


def _dot(a, b):
    return jnp.dot(a, b, preferred_element_type=F32)


def _dot_nt(a, b):
    return lax.dot_general(a, b, (((1,), (1,)), ((), ())), preferred_element_type=F32)


def _dot_tn(a, b):
    return lax.dot_general(a, b, (((0,), (0,)), ((), ())), preferred_element_type=F32)


def _params(*sem):
    return pltpu.CompilerParams(dimension_semantics=sem, vmem_limit_bytes=VMEM_LIMIT)


def _row_tile(m, pref):
    t = min(m, pref)
    assert m % t == 0
    return t


def _linear_kernel(x_ref, w_ref, *o_refs, n_chunk):
    xb = x_ref[...].astype(BF16)
    col = 0
    for o_ref in o_refs:
        nj = o_ref.shape[1]
        for c in range(0, nj, n_chunk):
            o_ref[:, c:c + n_chunk] = _dot(xb, w_ref[:, col + c:col + c + n_chunk]).astype(o_ref.dtype)
        col += nj


def _linear(x, w, widths, dtype, tm=512, name="linear"):
    m, k = x.shape
    tm = _row_tile(m, tm)
    n_chunk = 512
    assert all(wd % n_chunk == 0 for wd in widths) and sum(widths) == w.shape[1]
    return pl.pallas_call(
        functools.partial(_linear_kernel, n_chunk=n_chunk),
        grid=(m // tm,),
        in_specs=[pl.BlockSpec((tm, k), lambda i: (i, 0)),
                  pl.BlockSpec(w.shape, lambda i: (0, 0))],
        out_specs=[pl.BlockSpec((tm, wd), lambda i: (i, 0)) for wd in widths],
        out_shape=[jax.ShapeDtypeStruct((m, wd), dtype) for wd in widths],
        compiler_params=_params("parallel"),
        name=name,
    )(x, w)


def _res_ln(x, y, g, b):
    t = ALPHA * x + y
    mu = jnp.mean(t, axis=-1, keepdims=True)
    d = t - mu
    var = jnp.mean(d * d, axis=-1, keepdims=True)
    return d * lax.rsqrt(var + LN_EPS) * g + b


def _layer_tail_kernel(a_ref, wp_ref, x_ref, g1_ref, b1_ref, wi_ref, wo_ref, g2_ref, b2_ref, o_ref, *h_refs,
                       ff_chunk):
    n = len(h_refs)
    tm = x_ref.shape[0]
    d_ff = h_refs[0].shape[1]
    rows = [slice(i * (tm // n), (i + 1) * (tm // n)) for i in range(n)]
    x1, y = [None] * n, [None] * n

    def ln1(i):
        x1[i] = _res_ln(x_ref[rows[i], :], _dot(a_ref[rows[i], :], wp_ref[...]), g1_ref[...], b1_ref[...])

    def ffn(i):
        xb = x1[i].astype(BF16)
        for c in range(0, d_ff, ff_chunk):
            gate = _dot(xb, wi_ref[:, c:c + ff_chunk])
            up = _dot(xb, wi_ref[:, d_ff + c:d_ff + c + ff_chunk])
            h_refs[i][:, c:c + ff_chunk] = (gate * jax.nn.sigmoid(gate) * up).astype(BF16)
        y[i] = _dot(h_refs[i][...], wo_ref[...])

    def ln2(i):
        o_ref[rows[i], :] = _res_ln(x1[i], y[i], g2_ref[...], b2_ref[...])

    ln1(0)
    for i in range(n):
        if i + 1 < n:
            ln1(i + 1)
        ffn(i)
        if i > 0:
            ln2(i - 1)
    ln2(n - 1)


def _layer_tail(a, wp, x, ln1, wi, wo, layer, ln2, tm=1024, name="layer_tail"):
    m, k = a.shape
    d = x.shape[1]
    d_ff = wo.shape[1]
    tm = _row_tile(m, tm)
    ff_chunk = 256
    row_groups = 4 if tm % 1024 == 0 else 1
    assert d_ff % ff_chunk == 0
    rows = lambda width: pl.BlockSpec((tm, width), lambda i: (i, 0))
    resident = lambda w: pl.BlockSpec(w.shape, lambda i: (0, 0), pipeline_mode=pl.Buffered(1))
    of_layer = lambda w: pl.BlockSpec((None,) + w.shape[1:], lambda i: (layer, 0, 0), pipeline_mode=pl.Buffered(1))
    vec = pl.BlockSpec((1, d), lambda i: (0, 0))
    return pl.pallas_call(
        functools.partial(_layer_tail_kernel, ff_chunk=ff_chunk),
        grid=(m // tm,),
        in_specs=[rows(k), resident(wp), rows(d), vec, vec, of_layer(wi), of_layer(wo), vec, vec],
        out_specs=rows(d),
        out_shape=jax.ShapeDtypeStruct((m, d), F32),
        scratch_shapes=[pltpu.VMEM((tm // row_groups, d_ff), BF16)] * row_groups,
        compiler_params=_params("parallel"),
        name=name,
    )(a, wp, x, ln1[0].reshape(1, d), ln1[1].reshape(1, d), wi, wo, ln2[0].reshape(1, d), ln2[1].reshape(1, d))


def _softmax_pv(s_parts, v_parts):
    m = s_parts[0].max(axis=-1, keepdims=True)
    for s in s_parts[1:]:
        m = jnp.maximum(m, s.max(axis=-1, keepdims=True))
    acc, l = None, None
    for s, v in zip(s_parts, v_parts):
        e = jnp.exp(s - m)
        ls = e.sum(axis=-1, keepdims=True)
        pv = _dot(e.astype(BF16), v)
        acc = pv if acc is None else acc + pv
        l = ls if l is None else l + ls
    return acc / l


def _attn_prompt_kernel(q_ref, ka_ref, kb_ref, kc_ref, va_ref, vb_ref, vc_ref, w_ref, o_ref,
                        bias_ref, vext_ref):
    tile = pl.program_id(1)
    win = 3 * ATT_TILE

    @pl.when(tile <= 2)
    def _():
        qc = lax.broadcasted_iota(jnp.int32, (ATT_TILE, win), 0) // CHUNK
        kc = lax.broadcasted_iota(jnp.int32, (ATT_TILE, win), 1) // CHUNK
        valid = (kc >= qc) & (kc <= qc + BAND_CHUNKS) & (kc >= BAND_CHUNKS - tile * (ATT_TILE // CHUNK))
        for h in range(A_HEADS):
            w_rows = jnp.broadcast_to(w_ref[h:h + 1, :], (ATT_TILE, w_ref.shape[1]))
            toeplitz = pltpu.roll(w_rows, 0, 1, stride=1, stride_axis=0)[:, :win]
            bias_ref[h // 2, (h % 2) * ATT_TILE:(h % 2 + 1) * ATT_TILE, :] = jnp.where(valid, toeplitz, NEG_INF)

    n_groups = D_MODEL // LANES

    @pl.when(tile == 0)
    def _():
        vext_ref[:, :, :, LANES:] = jnp.ones((3, n_groups, ATT_TILE, LANES), BF16)

    k_refs, v_refs = (ka_ref, kb_ref, kc_ref), (va_ref, vb_ref, vc_ref)
    for j, v_ref in enumerate(v_refs):
        for p in range(n_groups):
            vext_ref[j, p, :, :LANES] = v_ref[0, :, p * LANES:(p + 1) * LANES]

    lane = lax.broadcasted_iota(jnp.int32, (ATT_TILE, LANES), 1)
    low_half = lane < A_HEAD_DIM
    def scores(p):
        cols = slice(p * LANES, (p + 1) * LANES)
        qp = q_ref[0, :, cols]
        zero = jnp.zeros_like(qp)
        q_pair = jnp.concatenate([jnp.where(low_half, qp, zero), jnp.where(low_half, zero, qp)], axis=0)
        return [_dot_nt(q_pair, k_ref[0, :, cols]) + bias_ref[p, :, j * ATT_TILE:(j + 1) * ATT_TILE]
                for j, k_ref in enumerate(k_refs)]

    ahead = 2
    pending = [scores(p) for p in range(ahead)]
    for p in range(n_groups):
        s_parts = pending.pop(0)
        if p + ahead < n_groups:
            pending.append(scores(p + ahead))
        m =functools.reduce(jnp.maximum, [s.max(axis=-1, keepdims=True) for s in s_parts])
        pv = sum(_dot(jnp.exp(s - m).astype(BF16), vext_ref[j, p]) for j, s in enumerate(s_parts))
        pv = pv[:, :LANES] / pv[:, LANES:]
        o_ref[0, :, p * LANES:(p + 1) * LANES] = jnp.where(low_half, pv[:ATT_TILE], pv[ATT_TILE:]).astype(o_ref.dtype)


def _band_bias_vectors(rel_table):
    n_heads = rel_table.shape[1]
    win = 3 * ATT_TILE
    t = rel_table.T.astype(F32)
    u = jnp.concatenate([jnp.broadcast_to(t[:, 2 * REL_CLIP:], (n_heads, win - REL_CLIP)),
                         jnp.flip(t[:, 1:2 * REL_CLIP], axis=1),
                         jnp.broadcast_to(t[:, :1], (n_heads, REL_CLIP))], axis=1)
    assert u.shape[1] == ATT_TILE + win - 1
    return jnp.roll(jnp.pad(u, ((0, 0), (0, 1))), -(ATT_TILE - 1), axis=1)


def _attn_prompt(q, k, v, bias_vectors):
    bsz, t_len, d = q.shape
    assert t_len % ATT_TILE == 0 and BAND_PAST == 2 * ATT_TILE
    blk = (1, ATT_TILE, d)

    def past(n):
        return lambda b, i: (b, jnp.maximum(i - n, 0), 0)

    kv_specs = [pl.BlockSpec(blk, past(2)), pl.BlockSpec(blk, past(1)), pl.BlockSpec(blk, past(0))]
    return pl.pallas_call(
        _attn_prompt_kernel,
        grid=(bsz, t_len // ATT_TILE),
        in_specs=[pl.BlockSpec(blk, lambda b, i: (b, i, 0))] + kv_specs + kv_specs
                 + [pl.BlockSpec(bias_vectors.shape, lambda b, i: (0, 0))],
        out_specs=pl.BlockSpec(blk, lambda b, i: (b, i, 0)),
        out_shape=jax.ShapeDtypeStruct((bsz, t_len, d), BF16),
        scratch_shapes=[pltpu.VMEM((A_HEADS // 2, 2 * ATT_TILE, 3 * ATT_TILE), F32),
                        pltpu.VMEM((3, d // LANES, ATT_TILE, 2 * LANES), BF16)],
        compiler_params=_params("parallel", "arbitrary"),
        name="attn_prompt",
    )(q, k, k, k, v, v, v, bias_vectors)


def _attn_sample_kernel(q_ref, kn_ref, vn_ref, kc_ref, vc_ref, w_ref, o_ref):
    t_len = q_ref.shape[1]
    n_cache = kc_ref.shape[1]
    lane = lax.broadcasted_iota(jnp.int32, (t_len, LANES), 1)
    low_half = lane < A_HEAD_DIM
    bias = [pltpu.roll(jnp.broadcast_to(w_ref[h:h + 1, :], (t_len, w_ref.shape[1])), 0, 1, stride=1, stride_axis=0)
            for h in range(A_HEADS)]
    for p in range(D_MODEL // LANES):
        cols = slice(p * LANES, (p + 1) * LANES)
        qp = q_ref[0, :, cols].astype(BF16)
        kc = kc_ref[0, :, cols].astype(BF16)
        vc = vc_ref[0, :, cols].astype(BF16)
        kn = kn_ref[0, :, cols].astype(BF16)
        vn = vn_ref[0, :, cols].astype(BF16)
        outs = []
        for hh in range(2):
            qm = jnp.where(low_half if hh == 0 else jnp.logical_not(low_half), qp, jnp.zeros_like(qp))
            s_c = _dot_nt(qm, kc) + bias[2 * p + hh][:, :n_cache]
            s_n = _dot_nt(qm, kn) + bias[2 * p + hh][:, n_cache:n_cache + t_len]
            outs.append(_softmax_pv([s_c, s_n], [vc, vn]))
        o_ref[0, :, cols] = jnp.where(low_half, outs[0], outs[1]).astype(o_ref.dtype)


def _sample_bias_vectors(rel_table, n_cache, t_len):
    assert t_len - 1 <= REL_CLIP <= n_cache + t_len
    n_heads = rel_table.shape[1]
    t = rel_table.T.astype(F32)
    u = jnp.concatenate([jnp.broadcast_to(t[:, 2 * REL_CLIP:], (n_heads, n_cache + t_len - REL_CLIP)),
                         jnp.flip(t[:, REL_CLIP - t_len + 1:2 * REL_CLIP], axis=1)], axis=1)
    assert u.shape[1] == n_cache + 2 * t_len - 1
    width = -(-u.shape[1] // LANES) * LANES
    return jnp.roll(jnp.pad(u, ((0, 0), (0, width - u.shape[1]))), -(t_len - 1), axis=1)


def _attn_sample(q, k_new, v_new, cache_k, cache_v, bias_vectors):
    bsz, t_len, d = q.shape
    n_cache = cache_k.shape[1]
    new_spec = pl.BlockSpec((1, t_len, d), lambda b: (b, 0, 0))
    cache_spec = pl.BlockSpec((1, n_cache, d), lambda b: (b, 0, 0))
    return pl.pallas_call(
        _attn_sample_kernel,
        grid=(bsz,),
        in_specs=[new_spec, new_spec, new_spec, cache_spec, cache_spec,
                  pl.BlockSpec(bias_vectors.shape, lambda b: (0, 0))],
        out_specs=new_spec,
        out_shape=jax.ShapeDtypeStruct((bsz, t_len, d), BF16),
        compiler_params=_params("parallel"),
        name="attn_sample",
    )(q, k_new, v_new, cache_k, cache_v, bias_vectors)


def _log_sigmoid(z):
    return jnp.minimum(z, 0.0) - jnp.log(1.0 + jnp.exp(-jnp.abs(z)))


def _gla_project(rows, x_ref, w_ref, wlow_ref, wup_ref, bgk_ref, q_ref, k_ref, g_ref, v_ref, r_ref):
    xb = x_ref[0, rows, :].astype(BF16)
    n_chunk = 512
    q_ref[0, rows, :] = _dot(xb, w_ref[:, 0:B_QK]) * (B_KEY_DIM ** -0.5)
    k_ref[0, rows, :] = _dot(xb, w_ref[:, B_QK:2 * B_QK])
    for c in range(0, B_VD, n_chunk):
        v_ref[0, rows, c:c + n_chunk] = _dot(xb, w_ref[:, 2 * B_QK + c:2 * B_QK + c + n_chunk])
        r_ref[0, rows, c:c + n_chunk] = _dot(xb, w_ref[:, 2 * B_QK + B_VD + c:2 * B_QK + B_VD + c + n_chunk])
    low = _dot(xb, wlow_ref[...])
    z = _dot(low.astype(BF16), wup_ref[...]) + bgk_ref[...]
    g_ref[0, rows, :] = _log_sigmoid(z) / B_GATE_NORM


def _split3(x):
    hi = x.astype(BF16)
    r1 = x - hi.astype(F32)
    mid = r1.astype(BF16)
    lo = (r1 - mid.astype(F32)).astype(BF16)
    return hi, mid, lo


def _gla_finish(o, h, rows, r_ref, gain_ref, a_ref):
    vcols = slice(h * B_VAL_DIM, (h + 1) * B_VAL_DIM)
    o = o * lax.rsqrt(jnp.mean(o * o, axis=-1, keepdims=True) + GN_EPS) * gain_ref[:, vcols]
    r_h = r_ref[0, rows, vcols]
    a_ref[0, rows, vcols] = (o * (r_h * jax.nn.sigmoid(r_h))).astype(a_ref.dtype)


def _gla_rows_as_one_chunk(rows, b, q_ref, k_ref, v_ref, r_ref, gain_ref, a_ref, st_ref):
    n = b.shape[0]
    q, k = q_ref[0, rows, :], k_ref[0, rows, :]
    b_last = b[n - 1:n, :]
    qhat = (q * jnp.exp(b)).astype(BF16)
    kinv = (k * jnp.exp(-b)).astype(BF16)
    kdec = (k * jnp.exp(b_last - b)).astype(BF16)
    decay = jnp.exp(b_last)
    causal = lax.broadcasted_iota(jnp.int32, (n, n), 1) <= lax.broadcasted_iota(jnp.int32, (n, n), 0)
    for h in range(B_HEADS):
        kcols = slice(h * B_KEY_DIM, (h + 1) * B_KEY_DIM)
        vcols = slice(h * B_VAL_DIM, (h + 1) * B_VAL_DIM)
        a_mat = jnp.where(causal, _dot_nt(qhat[:, kcols], kinv[:, kcols]), 0.0).astype(BF16)
        v_h = v_ref[0, rows, vcols].astype(BF16)
        st = st_ref[0, h]
        o = _dot(a_mat, v_h) + _dot_nt(qhat[:, kcols], st.astype(BF16))
        st_ref[0, h] = st * decay[:, kcols] + _dot_tn(v_h, kdec[:, kcols])
        _gla_finish(o, h, rows, r_ref, gain_ref, a_ref)


def _gla_kernel(x_ref, w_ref, wlow_ref, wup_ref, bgk_ref, s0_ref, gain_ref, a_ref, st_ref,
                q_ref, k_ref, g_ref, v_ref, r_ref, st0_ref, *, chunk, n_chunks, wide_chunk):
    @pl.when(pl.program_id(1) == 0)
    def _():
        st_ref[...] = s0_ref[...]

    n = chunk * n_chunks
    project = functools.partial(_gla_project, x_ref=x_ref, w_ref=w_ref, wlow_ref=wlow_ref, wup_ref=wup_ref,
                                bgk_ref=bgk_ref, q_ref=q_ref, k_ref=k_ref, g_ref=g_ref, v_ref=v_ref, r_ref=r_ref)

    def by_sub_blocks():
        _gla_chunks_by_sub_blocks(q_ref, k_ref, g_ref, v_ref, r_ref, gain_ref, a_ref, st_ref,
                                  chunk=chunk, n_chunks=n_chunks)

    if not wide_chunk:
        project(slice(0, n))
        by_sub_blocks()
        return
    project(slice(0, n))
    st0_ref[...] = st_ref[...]
    tri = (lax.broadcasted_iota(jnp.int32, (wide_chunk, wide_chunk), 1)
           <= lax.broadcasted_iota(jnp.int32, (wide_chunk, wide_chunk), 0))
    tri = jnp.where(tri, 1.0, 0.0).astype(BF16)
    min_log_decay = None
    for r0 in range(0, n, wide_chunk):
        rows = slice(r0, r0 + wide_chunk)
        b = sum(_dot(tri, gp) for gp in _split3(g_ref[0, rows, :]))
        _gla_rows_as_one_chunk(rows, b, q_ref, k_ref, v_ref, r_ref, gain_ref, a_ref, st_ref)
        total = b[wide_chunk - 1:wide_chunk, :]
        min_log_decay = total if min_log_decay is None else jnp.minimum(min_log_decay, total)

    @pl.when(jnp.logical_not(jnp.min(min_log_decay) >= -GLA_MAX_LOG_DECAY))
    def _():
        st_ref[...] = st0_ref[...]
        by_sub_blocks()


def _gla_chunks_by_sub_blocks(q_ref, k_ref, g_ref, v_ref, r_ref, gain_ref, a_ref, st_ref, *, chunk, n_chunks):
    nsb = chunk // SUB_BLOCK
    row = lax.broadcasted_iota(jnp.int32, (chunk, chunk), 0)
    col = lax.broadcasted_iota(jnp.int32, (chunk, chunk), 1)
    blk_start = (row // SUB_BLOCK) * SUB_BLOCK
    tri_local = jnp.where((col <= row) & (col >= blk_start), 1.0, 0.0).astype(BF16)
    tri_before = jnp.where(col < blk_start, 1.0, 0.0).astype(BF16)
    rcol = lax.broadcasted_iota(jnp.int32, (SUB_BLOCK, chunk), 1)
    rrow = lax.broadcasted_iota(jnp.int32, (SUB_BLOCK, chunk), 0)

    def chunk_body(c, carry):
        row0 = pl.multiple_of(c * chunk, chunk)
        rows = pl.ds(row0, chunk)
        q = q_ref[0, rows, :]
        k = k_ref[0, rows, :]
        g_parts = _split3(g_ref[0, rows, :])
        bl = sum(_dot(tri_local, gp) for gp in g_parts)
        if nsb > 1:
            rr = sum(_dot(tri_before, gp) for gp in g_parts)
            b = bl + rr
        else:
            rr = None
            b = bl
        b_last = b[chunk - 1:chunk, :]
        qt = q * jnp.exp(bl)
        qhat = qt * jnp.exp(rr) if nsb > 1 else qt
        kdec = (k * jnp.exp(b_last - b)).astype(BF16)
        decay = jnp.exp(b_last)
        kb = k.astype(BF16)

        for h in range(B_HEADS):
            kcols = slice(h * B_KEY_DIM, (h + 1) * B_KEY_DIM)
            vcols = slice(h * B_VAL_DIM, (h + 1) * B_VAL_DIM)
            q_h, k_h, bl_h, b_h, qt_h = q[:, kcols], k[:, kcols], bl[:, kcols], b[:, kcols], qt[:, kcols]
            kb_h = kb[:, kcols]
            a_rows = []
            for i in range(nsb):
                sb = slice(i * SUB_BLOCK, (i + 1) * SUB_BLOCK)
                q_s, bl_s = q_h[sb], bl_h[sb]
                y = jnp.concatenate(
                    [q_s * jnp.exp(jnp.minimum(bl_s - bl_s[j:j + 1], 0.0)) for j in range(SUB_BLOCK)], axis=0)
                res = _dot_nt(y.astype(BF16), kb_h)
                a_i = jnp.zeros((SUB_BLOCK, chunk), F32)
                for j in range(SUB_BLOCK):
                    a_i = jnp.where(rcol == i * SUB_BLOCK + j, res[j * SUB_BLOCK:(j + 1) * SUB_BLOCK], a_i)
                a_i = jnp.where(rcol <= i * SUB_BLOCK + rrow, a_i, 0.0)
                if i > 0:
                    r_i = rr[i * SUB_BLOCK:i * SUB_BLOCK + 1, kcols]
                    kt = (k_h * jnp.exp(jnp.minimum(r_i - b_h, 0.0))).astype(BF16)
                    a_off = _dot_nt(qt_h[sb].astype(BF16), kt)
                    a_i = jnp.where(rcol < i * SUB_BLOCK, a_off, a_i)
                a_rows.append(a_i)
            a_mat = (jnp.concatenate(a_rows, axis=0) if nsb > 1 else a_rows[0]).astype(BF16)
            v_h = v_ref[0, rows, vcols].astype(BF16)
            st = st_ref[0, h]
            o = _dot(a_mat, v_h) + _dot_nt(qhat[:, kcols].astype(BF16), st.astype(BF16))
            st_ref[0, h] = st * decay[:, kcols] + _dot_tn(v_h, kdec[:, kcols])
            _gla_finish(o, h, rows, r_ref, gain_ref, a_ref)
        return carry

    lax.fori_loop(0, n_chunks, chunk_body, 0)


def _gla(x, w_main, w_low, w_up, b_gk, s0t, gain, chunk, chunks_per_step, wide_chunk, name="gla"):
    bsz, t_len, d = x.shape
    rows = chunk * chunks_per_step
    assert t_len % rows == 0
    const = lambda b, s: (0, 0)
    resident = lambda w: pl.BlockSpec(w.shape, const, pipeline_mode=pl.Buffered(1))
    st_spec = pl.BlockSpec((1, B_HEADS, B_VAL_DIM, B_KEY_DIM), lambda b, s: (b, 0, 0, 0))
    out_spec = pl.BlockSpec((1, rows, B_VD), lambda b, s: (b, s, 0))
    return pl.pallas_call(
        functools.partial(_gla_kernel, chunk=chunk, n_chunks=chunks_per_step, wide_chunk=wide_chunk),
        grid=(bsz, t_len // rows),
        in_specs=[pl.BlockSpec((1, rows, d), lambda b, s: (b, s, 0)),
                  resident(w_main), resident(w_low), resident(w_up), pl.BlockSpec((1, B_QK), const),
                  st_spec, pl.BlockSpec((1, B_VD), const)],
        out_specs=[out_spec, st_spec],
        out_shape=[jax.ShapeDtypeStruct((bsz, t_len, B_VD), BF16),
                   jax.ShapeDtypeStruct(s0t.shape, F32)],
        scratch_shapes=[pltpu.VMEM((1, rows, B_QK), F32)] * 3 + [pltpu.VMEM((1, rows, B_VD), F32)] * 2
                       + [pltpu.VMEM((1, B_HEADS, B_VAL_DIM, B_KEY_DIM), F32)],
        compiler_params=_params("parallel", "arbitrary"),
        name=name,
    )(x, w_main, w_low, w_up, b_gk.reshape(1, B_QK), s0t, gain.reshape(1, B_VD))


def kernel(x_prompt, x_sample, cache_a_k, cache_a_v, state_b, w_in_a, rel_bias_a, w_out_a, w_in_b, w_gk_up_b,
           b_gk_b, gn_gain_b, w_out_b, w_ffn_in, w_ffn_out, ln1_g, ln1_b, ln2_g, ln2_b):
    bsz, t_len, d = x_prompt.shape
    dbsz, dt_len, _ = x_sample.shape
    n_cache = cache_a_k.shape[2]
    keep = min(BAND_PAST, t_len)
    xp = x_prompt.reshape(bsz * t_len, d)
    xs = x_sample.reshape(dbsz * dt_len, d)

    qscale = jnp.concatenate([jnp.full((d,), A_HEAD_DIM ** -0.5, F32), jnp.ones((2 * d,), F32)])
    w_qkv = (w_in_a[0] * qscale).astype(BF16)
    w_oa = w_out_a[0].astype(BF16)
    bias_p = _band_bias_vectors(rel_bias_a[0])
    bias_s = _sample_bias_vectors(rel_bias_a[0], n_cache, dt_len)

    q, k, v = _linear(xp, w_qkv, (d, d, d), BF16, name="qkv_prompt")
    x_tail = x_prompt[:, t_len - keep:].reshape(bsz * keep, d)
    k_tail, v_tail = _linear(x_tail, w_qkv[:, d:], (d, d), F32, name="kv_tail")
    att = _attn_prompt(q.reshape(bsz, t_len, d), k.reshape(bsz, t_len, d), v.reshape(bsz, t_len, d), bias_p)

    qs, ks, vs = _linear(xs, w_qkv, (d, d, d), F32, tm=128, name="qkv_sample")
    att_s = _attn_sample(qs.reshape(dbsz, dt_len, d), ks.reshape(dbsz, dt_len, d), vs.reshape(dbsz, dt_len, d),
                         cache_a_k[0].reshape(dbsz, n_cache, d), cache_a_v[0].reshape(dbsz, n_cache, d),
                         bias_s)

    wi, wo = w_ffn_in.astype(BF16), w_ffn_out.astype(BF16)
    ln1, ln2 = (ln1_g[0], ln1_b[0]), (ln2_g[0], ln2_b[0])
    xp = _layer_tail(att.reshape(bsz * t_len, d), w_oa, xp, ln1, wi, wo, 0, ln2, name="tail0_prompt")
    xs = _layer_tail(att_s.reshape(dbsz * dt_len, d), w_oa, xs, ln1, wi, wo, 0, ln2, tm=128, name="tail0_sample")

    n_main = 2 * B_QK + 2 * B_VD
    w_main = w_in_b[0][:, :n_main].astype(BF16)
    w_low = jnp.pad(w_in_b[0][:, n_main:], ((0, 0), (0, LANES - B_GATE_RANK))).astype(BF16)
    w_up = jnp.pad(w_gk_up_b[0], ((0, LANES - B_GATE_RANK), (0, 0))).astype(BF16)
    w_ob = w_out_b[0].astype(BF16)

    s0 = jnp.zeros((bsz, B_HEADS, B_VAL_DIM, B_KEY_DIM), F32)
    a_p, st_p = _gla(xp.reshape(bsz, t_len, d), w_main, w_low, w_up, b_gk_b[0], s0, gn_gain_b[0],
                     CHUNK, min(GLA_STEP_ROWS, t_len) // CHUNK, GLA_WIDE_CHUNK, name="gla_prompt")
    a_s, st_s = _gla(xs.reshape(dbsz, dt_len, d), w_main, w_low, w_up, b_gk_b[0],
                     jnp.swapaxes(state_b[0], -1, -2), gn_gain_b[0], dt_len, 1, None, name="gla_sample")

    ln1, ln2 = (ln1_g[1], ln1_b[1]), (ln2_g[1], ln2_b[1])
    xp = _layer_tail(a_p.reshape(bsz * t_len, B_VD), w_ob, xp, ln1, wi, wo, 1, ln2, name="tail1_prompt")
    xs = _layer_tail(a_s.reshape(dbsz * dt_len, B_VD), w_ob, xs, ln1, wi, wo, 1, ln2, tm=128, name="tail1_sample")

    heads = lambda a, n, t: a.reshape(1, n, t, A_HEADS, A_HEAD_DIM)
    return (xp.reshape(bsz, t_len, d), xs.reshape(dbsz, dt_len, d),
            heads(k_tail, bsz, keep), heads(v_tail, bsz, keep), jnp.swapaxes(st_p, -1, -2)[None],
            heads(ks, dbsz, dt_len), heads(vs, dbsz, dt_len), jnp.swapaxes(st_s, -1, -2)[None])
```

```python
import functools

import jax
import jax.numpy as jnp
from jax import lax
from jax.experimental import pallas as pl
from jax.experimental.pallas import tpu as pltpu

F32 = jnp.float32
BF16 = jnp.bfloat16

D_MODEL = 1024
CHUNK = 64
A_HEADS = 16
A_HEAD_DIM = D_MODEL // A_HEADS
BAND_CHUNKS = 8
BAND_PAST = BAND_CHUNKS * CHUNK
REL_CLIP = 128
B_HEADS = 4
B_KEY_DIM = D_MODEL // 2 // B_HEADS
B_VAL_DIM = D_MODEL // B_HEADS
B_QK = B_HEADS * B_KEY_DIM
B_VD = B_HEADS * B_VAL_DIM
B_GATE_RANK = 16
B_GATE_NORM = 16.0
D_FF = -(-8 * D_MODEL // 768) * 256
DEPTH = 2
ALPHA = (2.0 * DEPTH) ** 0.25
LN_EPS = 1e-5
GN_EPS = 1e-6
NEG_INF = -1e30

LANES = 128
SUB_BLOCK = 16
ATT_TILE = 4 * CHUNK
GLA_STEP_ROWS = 1024
GLA_WIDE_CHUNK = 256
GLA_MAX_LOG_DECAY = 60.0
VMEM_LIMIT = 56 * 1024 * 1024


def _dot(a, b):
    return jnp.dot(a, b, preferred_element_type=F32)


def _dot_nt(a, b):
    return lax.dot_general(a, b, (((1,), (1,)), ((), ())), preferred_element_type=F32)


def _dot_tn(a, b):
    return lax.dot_general(a, b, (((0,), (0,)), ((), ())), preferred_element_type=F32)


def _params(*sem):
    return pltpu.CompilerParams(dimension_semantics=sem, vmem_limit_bytes=VMEM_LIMIT)


def _row_tile(m, pref):
    t = min(m, pref)
    assert m % t == 0
    return t


def _qkv_kernel(x_ref, w_ref, q_ref, k_ref, v_ref, *tail_refs, n_chunk):
    xb = x_ref[...].astype(BF16)
    tm, d = q_ref.shape
    for idx, o_ref in enumerate((q_ref, k_ref, v_ref)):
        for c in range(0, d, n_chunk):
            y = _dot(xb, w_ref[:, idx * d + c:idx * d + c + n_chunk].astype(BF16))
            if idx == 0:
                y = y * (A_HEAD_DIM ** -0.5)
            o_ref[:, c:c + n_chunk] = y.astype(o_ref.dtype)
            if tail_refs and idx > 0:
                tail_ref = tail_refs[idx - 1]
                tail_ref[:, c:c + n_chunk] = y[tm - tail_ref.shape[0]:, :]


def _qkv(x, w, dtype, t_len, tail_rows=0, tm=1024, name="qkv"):
    m, d = x.shape
    tm = _row_tile(t_len if tail_rows else m, tm)
    n_chunk = 512
    assert d % n_chunk == 0 and w.shape == (d, 3 * d) and tail_rows <= tm
    tiles_per_seq = t_len // tm
    rows = pl.BlockSpec((tm, d), lambda i: (i, 0))
    out_specs, out_shape = [rows] * 3, [jax.ShapeDtypeStruct((m, d), dtype)] * 3
    if tail_rows:
        out_specs += [pl.BlockSpec((tail_rows, d), lambda i: (i // tiles_per_seq, 0))] * 2
        out_shape += [jax.ShapeDtypeStruct((m // t_len * tail_rows, d), F32)] * 2
    return pl.pallas_call(
        functools.partial(_qkv_kernel, n_chunk=n_chunk),
        grid=(m // tm,),
        in_specs=[rows, pl.BlockSpec(w.shape, lambda i: (0, 0), pipeline_mode=pl.Buffered(1))],
        out_specs=out_specs,
        out_shape=out_shape,
        compiler_params=_params("arbitrary"),
        name=name,
    )(x, w)


def _res_ln(x, y, g, b):
    t = ALPHA * x + y
    mu = jnp.mean(t, axis=-1, keepdims=True)
    d = t - mu
    var = jnp.mean(d * d, axis=-1, keepdims=True)
    return d * lax.rsqrt(var + LN_EPS) * g + b


def _layer_tail_kernel(a_ref, wp_ref, x_ref, g1_ref, b1_ref, wi_ref, wo_ref, g2_ref, b2_ref, o_ref, *h_refs,
                       ff_chunk):
    n = len(h_refs)
    tm = x_ref.shape[0]
    d_ff = h_refs[0].shape[1]
    rows = [slice(i * (tm // n), (i + 1) * (tm // n)) for i in range(n)]
    x1, y = [None] * n, [None] * n

    def ln1(i):
        x1[i] = _res_ln(x_ref[rows[i], :], _dot(a_ref[rows[i], :], wp_ref[...]), g1_ref[...], b1_ref[...])

    def ffn(i):
        xb = x1[i].astype(BF16)
        for c in range(0, d_ff, ff_chunk):
            gate = _dot(xb, wi_ref[:, c:c + ff_chunk])
            up = _dot(xb, wi_ref[:, d_ff + c:d_ff + c + ff_chunk])
            h_refs[i][:, c:c + ff_chunk] = (gate * jax.nn.sigmoid(gate) * up).astype(BF16)
        y[i] = _dot(h_refs[i][...], wo_ref[...])

    def ln2(i):
        o_ref[rows[i], :] = _res_ln(x1[i], y[i], g2_ref[...], b2_ref[...])

    ln1(0)
    for i in range(n):
        if i + 1 < n:
            ln1(i + 1)
        ffn(i)
        if i > 0:
            ln2(i - 1)
    ln2(n - 1)


def _layer_tail(a, wp, x, ln1, wi, wo, layer, ln2, tm=1024, name="layer_tail"):
    m, k = a.shape
    d = x.shape[1]
    d_ff = wo.shape[1]
    tm = _row_tile(m, tm)
    ff_chunk = 256
    row_groups = 4 if tm % 1024 == 0 else 1
    assert d_ff % ff_chunk == 0
    rows = lambda width: pl.BlockSpec((tm, width), lambda i: (i, 0))
    resident = lambda w: pl.BlockSpec(w.shape, lambda i: (0, 0), pipeline_mode=pl.Buffered(1))
    of_layer = lambda w: pl.BlockSpec((None,) + w.shape[1:], lambda i: (layer, 0, 0), pipeline_mode=pl.Buffered(1))
    vec = pl.BlockSpec((1, d), lambda i: (0, 0))
    return pl.pallas_call(
        functools.partial(_layer_tail_kernel, ff_chunk=ff_chunk),
        grid=(m // tm,),
        in_specs=[rows(k), resident(wp), rows(d), vec, vec, of_layer(wi), of_layer(wo), vec, vec],
        out_specs=rows(d),
        out_shape=jax.ShapeDtypeStruct((m, d), F32),
        scratch_shapes=[pltpu.VMEM((tm // row_groups, d_ff), BF16)] * row_groups,
        compiler_params=_params("parallel"),
        name=name,
    )(a, wp, x, ln1[0].reshape(1, d), ln1[1].reshape(1, d), wi, wo, ln2[0].reshape(1, d), ln2[1].reshape(1, d))


def _softmax_pv(s_parts, v_parts):
    m = s_parts[0].max(axis=-1, keepdims=True)
    for s in s_parts[1:]:
        m = jnp.maximum(m, s.max(axis=-1, keepdims=True))
    acc, l = None, None
    for s, v in zip(s_parts, v_parts):
        e = jnp.exp(s - m)
        ls = e.sum(axis=-1, keepdims=True)
        pv = _dot(e.astype(BF16), v)
        acc = pv if acc is None else acc + pv
        l = ls if l is None else l + ls
    return acc / l


def _attn_prompt_kernel(q_ref, ka_ref, kb_ref, kc_ref, va_ref, vb_ref, vc_ref, w_ref, o_ref,
                        bias_ref, vext_ref):
    tile = pl.program_id(1)
    win = 3 * ATT_TILE

    @pl.when(tile <= 2)
    def _():
        qc = lax.broadcasted_iota(jnp.int32, (ATT_TILE, win), 0) // CHUNK
        kc = lax.broadcasted_iota(jnp.int32, (ATT_TILE, win), 1) // CHUNK
        valid = (kc >= qc) & (kc <= qc + BAND_CHUNKS) & (kc >= BAND_CHUNKS - tile * (ATT_TILE // CHUNK))
        for h in range(A_HEADS):
            w_rows = jnp.broadcast_to(w_ref[h:h + 1, :], (ATT_TILE, w_ref.shape[1]))
            toeplitz = pltpu.roll(w_rows, 0, 1, stride=1, stride_axis=0)[:, :win]
            bias_ref[h // 2, (h % 2) * ATT_TILE:(h % 2 + 1) * ATT_TILE, :] = jnp.where(valid, toeplitz, NEG_INF)

    n_groups = D_MODEL // LANES

    @pl.when(tile == 0)
    def _():
        vext_ref[:, :, :, LANES:] = jnp.ones((3, n_groups, ATT_TILE, LANES), BF16)

    k_refs, v_refs = (ka_ref, kb_ref, kc_ref), (va_ref, vb_ref, vc_ref)
    for j, v_ref in enumerate(v_refs):
        for p in range(n_groups):
            vext_ref[j, p, :, :LANES] = v_ref[0, :, p * LANES:(p + 1) * LANES]

    lane = lax.broadcasted_iota(jnp.int32, (ATT_TILE, LANES), 1)
    low_half = lane < A_HEAD_DIM

    def scores(p):
        cols = slice(p * LANES, (p + 1) * LANES)
        qp = q_ref[0, :, cols]
        zero = jnp.zeros_like(qp)
        q_pair = jnp.concatenate([jnp.where(low_half, qp, zero), jnp.where(low_half, zero, qp)], axis=0)
        return [_dot_nt(q_pair, k_ref[0, :, cols]) + bias_ref[p, :, j * ATT_TILE:(j + 1) * ATT_TILE]
                for j, k_ref in enumerate(k_refs)]

    ahead = 2
    pending = [scores(p) for p in range(ahead)]
    for p in range(n_groups):
        s_parts = pending.pop(0)
        if p + ahead < n_groups:
            pending.append(scores(p + ahead))
        m = functools.reduce(jnp.maximum, [s.max(axis=-1, keepdims=True) for s in s_parts])
        pv = sum(_dot(jnp.exp(s - m).astype(BF16), vext_ref[j, p]) for j, s in enumerate(s_parts))
        pv = pv[:, :LANES] / pv[:, LANES:]
        o_ref[0, :, p * LANES:(p + 1) * LANES] = jnp.where(low_half, pv[:ATT_TILE], pv[ATT_TILE:]).astype(o_ref.dtype)


def _band_bias_vectors(rel_table):
    n_heads = rel_table.shape[1]
    win = 3 * ATT_TILE
    t = rel_table.T.astype(F32)
    u = jnp.concatenate([jnp.broadcast_to(t[:, 2 * REL_CLIP:], (n_heads, win - REL_CLIP)),
                         jnp.flip(t[:, 1:2 * REL_CLIP], axis=1),
                         jnp.broadcast_to(t[:, :1], (n_heads, REL_CLIP))], axis=1)
    assert u.shape[1] == ATT_TILE + win - 1
    return jnp.roll(jnp.pad(u, ((0, 0), (0, 1))), -(ATT_TILE - 1), axis=1)


def _attn_prompt(q, k, v, bias_vectors):
    bsz, t_len, d = q.shape
    assert t_len % ATT_TILE == 0 and BAND_PAST == 2 * ATT_TILE
    blk = (1, ATT_TILE, d)

    def past(n):
        return lambda b, i: (b, jnp.maximum(i - n, 0), 0)

    kv_specs = [pl.BlockSpec(blk, past(2)), pl.BlockSpec(blk, past(1)), pl.BlockSpec(blk, past(0))]
    return pl.pallas_call(
        _attn_prompt_kernel,
        grid=(bsz, t_len // ATT_TILE),
        in_specs=[pl.BlockSpec(blk, lambda b, i: (b, i, 0))] + kv_specs + kv_specs
                 + [pl.BlockSpec(bias_vectors.shape, lambda b, i: (0, 0))],
        out_specs=pl.BlockSpec(blk, lambda b, i: (b, i, 0)),
        out_shape=jax.ShapeDtypeStruct((bsz, t_len, d), BF16),
        scratch_shapes=[pltpu.VMEM((A_HEADS // 2, 2 * ATT_TILE, 3 * ATT_TILE), F32),
                        pltpu.VMEM((3, d // LANES, ATT_TILE, 2 * LANES), BF16)],
        compiler_params=_params("parallel", "arbitrary"),
        name="attn_prompt",
    )(q, k, k, k, v, v, v, bias_vectors)


def _attn_sample_kernel(q_ref, kn_ref, vn_ref, kc_ref, vc_ref, w_ref, o_ref):
    t_len = q_ref.shape[1]
    n_cache = kc_ref.shape[1]
    lane = lax.broadcasted_iota(jnp.int32, (t_len, LANES), 1)
    low_half = lane < A_HEAD_DIM
    bias = [pltpu.roll(jnp.broadcast_to(w_ref[h:h + 1, :], (t_len, w_ref.shape[1])), 0, 1, stride=1, stride_axis=0)
            for h in range(A_HEADS)]
    for p in range(D_MODEL // LANES):
        cols = slice(p * LANES, (p + 1) * LANES)
        qp = q_ref[0, :, cols].astype(BF16)
        kc = kc_ref[0, :, cols].astype(BF16)
        vc = vc_ref[0, :, cols].astype(BF16)
        kn = kn_ref[0, :, cols].astype(BF16)
        vn = vn_ref[0, :, cols].astype(BF16)
        zero = jnp.zeros_like(qp)
        q_pair = jnp.concatenate([jnp.where(low_half, qp, zero), jnp.where(low_half, zero, qp)], axis=0)
        b_pair = jnp.concatenate([bias[2 * p], bias[2 * p + 1]], axis=0)
        s_c = _dot_nt(q_pair, kc) + b_pair[:, :n_cache]
        s_n = _dot_nt(q_pair, kn) + b_pair[:, n_cache:n_cache + t_len]
        pv = _softmax_pv([s_c, s_n], [vc, vn])
        o_ref[0, :, cols] = jnp.where(low_half, pv[:t_len], pv[t_len:]).astype(o_ref.dtype)


def _sample_bias_vectors(rel_table, n_cache, t_len):
    assert t_len - 1 <= REL_CLIP <= n_cache + t_len
    n_heads = rel_table.shape[1]
    t = rel_table.T.astype(F32)
    u = jnp.concatenate([jnp.broadcast_to(t[:, 2 * REL_CLIP:], (n_heads, n_cache + t_len - REL_CLIP)),
                         jnp.flip(t[:, REL_CLIP - t_len + 1:2 * REL_CLIP], axis=1)], axis=1)
    assert u.shape[1] == n_cache + 2 * t_len - 1
    width = -(-u.shape[1] // LANES) * LANES
    return jnp.roll(jnp.pad(u, ((0, 0), (0, width - u.shape[1]))), -(t_len - 1), axis=1)


def _attn_sample(q, k_new, v_new, cache_k, cache_v, bias_vectors):
    bsz, t_len, d = q.shape
    n_cache = cache_k.shape[1]
    new_spec = pl.BlockSpec((1, t_len, d), lambda b: (b, 0, 0))
    cache_spec = pl.BlockSpec((1, n_cache, d), lambda b: (b, 0, 0))
    return pl.pallas_call(
        _attn_sample_kernel,
        grid=(bsz,),
        in_specs=[new_spec, new_spec, new_spec, cache_spec, cache_spec,
                  pl.BlockSpec(bias_vectors.shape, lambda b: (0, 0))],
        out_specs=new_spec,
        out_shape=jax.ShapeDtypeStruct((bsz, t_len, d), BF16),
        compiler_params=_params("parallel"),
        name="attn_sample",
    )(q, k_new, v_new, cache_k, cache_v, bias_vectors)


def _log_sigmoid(z):
    return jnp.minimum(z, 0.0) - jnp.log(1.0 + jnp.exp(-jnp.abs(z)))


def _gla_project(rows, x_ref, w_ref, wup_ref, bgk_ref, q_ref, k_ref, g_ref, v_ref, r_ref):
    xb = x_ref[0, rows, :].astype(BF16)
    n_chunk = 512
    proj = lambda lo, hi: _dot(xb, w_ref[:, lo:hi].astype(BF16))
    q_ref[0, rows, :] = proj(0, B_QK) * (B_KEY_DIM ** -0.5)
    k_ref[0, rows, :] = proj(B_QK, 2 * B_QK)
    for c in range(0, B_VD, n_chunk):
        v_ref[0, rows, c:c + n_chunk] = proj(2 * B_QK + c, 2 * B_QK + c + n_chunk)
        r_ref[0, rows, c:c + n_chunk] = proj(2 * B_QK + B_VD + c, 2 * B_QK + B_VD + c + n_chunk)
    n_main = 2 * B_QK + 2 * B_VD
    low = proj(n_main, n_main + B_GATE_RANK)
    z = _dot(low.astype(BF16), wup_ref[...].astype(BF16)) + bgk_ref[...]
    g_ref[0, rows, :] = _log_sigmoid(z) / B_GATE_NORM


def _split3(x):
    hi = x.astype(BF16)
    r1 = x - hi.astype(F32)
    mid = r1.astype(BF16)
    lo = (r1 - mid.astype(F32)).astype(BF16)
    return hi, mid, lo


def _gla_finish(o, h, rows, r_ref, gain_ref, a_ref):
    vcols = slice(h * B_VAL_DIM, (h + 1) * B_VAL_DIM)
    o = o * lax.rsqrt(jnp.mean(o * o, axis=-1, keepdims=True) + GN_EPS) * gain_ref[:, vcols]
    r_h = r_ref[0, rows, vcols]
    a_ref[0, rows, vcols] = (o * (r_h * jax.nn.sigmoid(r_h))).astype(a_ref.dtype)


def _gla_rows_as_one_chunk(rows, b, q_ref, k_ref, v_ref, r_ref, gain_ref, a_ref, st_ref):
    n = b.shape[0]
    q, k = q_ref[0, rows, :], k_ref[0, rows, :]
    b_last = b[n - 1:n, :]
    qhat = (q * jnp.exp(b)).astype(BF16)
    kinv = (k * jnp.exp(-b)).astype(BF16)
    kdec = (k * jnp.exp(b_last - b)).astype(BF16)
    decay = jnp.exp(b_last)
    causal = lax.broadcasted_iota(jnp.int32, (n, n), 1) <= lax.broadcasted_iota(jnp.int32, (n, n), 0)
    for h in range(B_HEADS):
        kcols = slice(h * B_KEY_DIM, (h + 1) * B_KEY_DIM)
        vcols = slice(h * B_VAL_DIM, (h + 1) * B_VAL_DIM)
        a_mat = jnp.where(causal, _dot_nt(qhat[:, kcols], kinv[:, kcols]), 0.0).astype(BF16)
        v_h = v_ref[0, rows, vcols].astype(BF16)
        st = st_ref[0, h]
        o = _dot(a_mat, v_h) + _dot_nt(qhat[:, kcols], st.astype(BF16))
        st_ref[0, h] = st * decay[:, kcols] + _dot_tn(v_h, kdec[:, kcols])
        _gla_finish(o, h, rows, r_ref, gain_ref, a_ref)


def _gla_kernel(x_ref, w_ref, wup_ref, bgk_ref, s0_ref, gain_ref, a_ref, st_ref,
                q_ref, k_ref, g_ref, v_ref, r_ref, st0_ref, *, chunk, n_chunks, wide_chunk):
    @pl.when(pl.program_id(1) == 0)
    def _():
        st_ref[...] = s0_ref[...]

    n = chunk * n_chunks
    project = functools.partial(_gla_project, x_ref=x_ref, w_ref=w_ref, wup_ref=wup_ref,
                                bgk_ref=bgk_ref, q_ref=q_ref, k_ref=k_ref, g_ref=g_ref, v_ref=v_ref, r_ref=r_ref)

    def by_sub_blocks():
        _gla_chunks_by_sub_blocks(q_ref, k_ref, g_ref, v_ref, r_ref, gain_ref, a_ref, st_ref,
                                  chunk=chunk, n_chunks=n_chunks)

    if not wide_chunk:
        project(slice(0, n))
        by_sub_blocks()
        return
    project(slice(0, n))
    st0_ref[...] = st_ref[...]
    tri = (lax.broadcasted_iota(jnp.int32, (wide_chunk, wide_chunk), 1)
           <= lax.broadcasted_iota(jnp.int32, (wide_chunk, wide_chunk), 0))
    tri = jnp.where(tri, 1.0, 0.0).astype(BF16)
    min_log_decay = None
    for r0 in range(0, n, wide_chunk):
        rows = slice(r0, r0 + wide_chunk)
        b = sum(_dot(tri, gp) for gp in _split3(g_ref[0, rows, :]))
        _gla_rows_as_one_chunk(rows, b, q_ref, k_ref, v_ref, r_ref, gain_ref, a_ref, st_ref)
        total = b[wide_chunk - 1:wide_chunk, :]
        min_log_decay = total if min_log_decay is None else jnp.minimum(min_log_decay, total)

    @pl.when(jnp.logical_not(jnp.min(min_log_decay) >= -GLA_MAX_LOG_DECAY))
    def _():
        st_ref[...] = st0_ref[...]
        by_sub_blocks()


def _gla_chunks_by_sub_blocks(q_ref, k_ref, g_ref, v_ref, r_ref, gain_ref, a_ref, st_ref, *, chunk, n_chunks):
    nsb = chunk // SUB_BLOCK
    row = lax.broadcasted_iota(jnp.int32, (chunk, chunk), 0)
    col = lax.broadcasted_iota(jnp.int32, (chunk, chunk), 1)
    blk_start = (row // SUB_BLOCK) * SUB_BLOCK
    tri_local = jnp.where((col <= row) & (col >= blk_start), 1.0, 0.0).astype(BF16)
    tri_before = jnp.where(col < blk_start, 1.0, 0.0).astype(BF16)
    rcol = lax.broadcasted_iota(jnp.int32, (SUB_BLOCK, chunk), 1)
    rrow = lax.broadcasted_iota(jnp.int32, (SUB_BLOCK, chunk), 0)

    def chunk_body(c, carry):
        row0 = pl.multiple_of(c * chunk, chunk)
        rows = pl.ds(row0, chunk)
        q = q_ref[0, rows, :]
        k = k_ref[0, rows, :]
        g_parts = _split3(g_ref[0, rows, :])
        bl = sum(_dot(tri_local, gp) for gp in g_parts)
        if nsb > 1:
            rr = sum(_dot(tri_before, gp) for gp in g_parts)
            b = bl + rr
        else:
            rr = None
            b = bl
        b_last = b[chunk - 1:chunk, :]
        qt = q * jnp.exp(bl)
        qhat = qt * jnp.exp(rr) if nsb > 1 else qt
        kdec = (k * jnp.exp(b_last - b)).astype(BF16)
        decay = jnp.exp(b_last)
        kb = k.astype(BF16)

        for h in range(B_HEADS):
            kcols = slice(h * B_KEY_DIM, (h + 1) * B_KEY_DIM)
            vcols = slice(h * B_VAL_DIM, (h + 1) * B_VAL_DIM)
            q_h, k_h, bl_h, b_h, qt_h = q[:, kcols], k[:, kcols], bl[:, kcols], b[:, kcols], qt[:, kcols]
            kb_h = kb[:, kcols]
            a_rows = []
            for i in range(nsb):
                sb = slice(i * SUB_BLOCK, (i + 1) * SUB_BLOCK)
                q_s, bl_s = q_h[sb], bl_h[sb]
                y = jnp.concatenate(
                    [q_s * jnp.exp(jnp.minimum(bl_s - bl_s[j:j + 1], 0.0)) for j in range(SUB_BLOCK)], axis=0)
                res = _dot_nt(y.astype(BF16), kb_h)
                a_i = jnp.zeros((SUB_BLOCK, chunk), F32)
                for j in range(SUB_BLOCK):
                    a_i = jnp.where(rcol == i * SUB_BLOCK + j, res[j * SUB_BLOCK:(j + 1) * SUB_BLOCK], a_i)
                a_i = jnp.where(rcol <= i * SUB_BLOCK + rrow, a_i, 0.0)
                if i > 0:
                    r_i = rr[i * SUB_BLOCK:i * SUB_BLOCK + 1, kcols]
                    kt = (k_h * jnp.exp(jnp.minimum(r_i - b_h, 0.0))).astype(BF16)
                    a_off = _dot_nt(qt_h[sb].astype(BF16), kt)
                    a_i = jnp.where(rcol < i * SUB_BLOCK, a_off, a_i)
                a_rows.append(a_i)
            a_mat = (jnp.concatenate(a_rows, axis=0) if nsb > 1 else a_rows[0]).astype(BF16)
            v_h = v_ref[0, rows, vcols].astype(BF16)
            st = st_ref[0, h]
            o = _dot(a_mat, v_h) + _dot_nt(qhat[:, kcols].astype(BF16), st.astype(BF16))
            st_ref[0, h] = st * decay[:, kcols] + _dot_tn(v_h, kdec[:, kcols])
            _gla_finish(o, h, rows, r_ref, gain_ref, a_ref)
        return carry

    lax.fori_loop(0, n_chunks, chunk_body, 0)


def _gla(x, w_in, w_up, b_gk, s0t, gain, chunk, chunks_per_step, wide_chunk, name="gla"):
    bsz, t_len, d = x.shape
    rows = chunk * chunks_per_step
    assert t_len % rows == 0
    const = lambda b, s: (0, 0)
    resident = lambda w: pl.BlockSpec(w.shape, const, pipeline_mode=pl.Buffered(1))
    st_spec = pl.BlockSpec((1, B_HEADS, B_VAL_DIM, B_KEY_DIM), lambda b, s: (b, 0, 0, 0))
    out_spec = pl.BlockSpec((1, rows, B_VD), lambda b, s: (b, s, 0))
    return pl.pallas_call(
        functools.partial(_gla_kernel, chunk=chunk, n_chunks=chunks_per_step, wide_chunk=wide_chunk),
        grid=(bsz, t_len // rows),
        in_specs=[pl.BlockSpec((1, rows, d), lambda b, s: (b, s, 0)),
                  resident(w_in), resident(w_up), pl.BlockSpec((1, B_QK), const),
                  st_spec, pl.BlockSpec((1, B_VD), const)],
        out_specs=[out_spec, st_spec],
        out_shape=[jax.ShapeDtypeStruct((bsz, t_len, B_VD), BF16),
                   jax.ShapeDtypeStruct(s0t.shape, F32)],
        scratch_shapes=[pltpu.VMEM((1, rows, B_QK), F32)] * 3 + [pltpu.VMEM((1, rows, B_VD), F32)] * 2
                       + [pltpu.VMEM((1, B_HEADS, B_VAL_DIM, B_KEY_DIM), F32)],
        compiler_params=_params("parallel", "arbitrary"),
        name=name,
    )(x, w_in, w_up, b_gk.reshape(1, B_QK), s0t, gain.reshape(1, B_VD))


def kernel(x_prompt, x_sample, cache_a_k, cache_a_v, state_b, w_in_a, rel_bias_a, w_out_a, w_in_b, w_gk_up_b,
           b_gk_b, gn_gain_b, w_out_b, w_ffn_in, w_ffn_out, ln1_g, ln1_b, ln2_g, ln2_b):
    bsz, t_len, d = x_prompt.shape
    dbsz, dt_len, _ = x_sample.shape
    n_cache = cache_a_k.shape[2]
    keep = min(BAND_PAST, t_len)
    xp = x_prompt.reshape(bsz * t_len, d)
    xs = x_sample.reshape(dbsz * dt_len, d)

    w_oa = w_out_a[0].astype(BF16)
    bias_p = _band_bias_vectors(rel_bias_a[0])
    bias_s = _sample_bias_vectors(rel_bias_a[0], n_cache, dt_len)

    q, k, v, k_tail, v_tail = _qkv(xp, w_in_a[0], BF16, t_len, tail_rows=keep, name="qkv_prompt")
    att = _attn_prompt(q.reshape(bsz, t_len, d), k.reshape(bsz, t_len, d), v.reshape(bsz, t_len, d), bias_p)

    qs, ks, vs = _qkv(xs, w_in_a[0], F32, dt_len, name="qkv_sample")
    att_s = _attn_sample(qs.reshape(dbsz, dt_len, d), ks.reshape(dbsz, dt_len, d), vs.reshape(dbsz, dt_len, d),
                         cache_a_k[0].reshape(dbsz, n_cache, d), cache_a_v[0].reshape(dbsz, n_cache, d),
                         bias_s)

    wi, wo = w_ffn_in.astype(BF16), w_ffn_out.astype(BF16)
    ln1, ln2 = (ln1_g[0], ln1_b[0]), (ln2_g[0], ln2_b[0])
    xp = _layer_tail(att.reshape(bsz * t_len, d), w_oa, xp, ln1, wi, wo, 0, ln2, name="tail0_prompt")
    xs = _layer_tail(att_s.reshape(dbsz * dt_len, d), w_oa, xs, ln1, wi, wo, 0, ln2, tm=128, name="tail0_sample")

    w_ob = w_out_b[0].astype(BF16)

    s0 = jnp.zeros((bsz, B_HEADS, B_VAL_DIM, B_KEY_DIM), F32)
    a_p, st_p = _gla(xp.reshape(bsz, t_len, d), w_in_b[0], w_gk_up_b[0], b_gk_b[0], s0, gn_gain_b[0],
                     CHUNK, min(GLA_STEP_ROWS, t_len) // CHUNK, GLA_WIDE_CHUNK, name="gla_prompt")
    a_s, st_s = _gla(xs.reshape(dbsz, dt_len, d), w_in_b[0], w_gk_up_b[0], b_gk_b[0],
                     jnp.swapaxes(state_b[0], -1, -2), gn_gain_b[0], dt_len, 1, None, name="gla_sample")

    ln1, ln2 = (ln1_g[1], ln1_b[1]), (ln2_g[1], ln2_b[1])
    xp = _layer_tail(a_p.reshape(bsz * t_len, B_VD), w_ob, xp, ln1, wi, wo, 1, ln2, name="tail1_prompt")
    xs = _layer_tail(a_s.reshape(dbsz * dt_len, B_VD), w_ob, xs, ln1, wi, wo, 1, ln2, tm=128, name="tail1_sample")

    heads = lambda a, n, t: a.reshape(1, n, t, A_HEADS, A_HEAD_DIM)
    return (xp.reshape(bsz, t_len, d), xs.reshape(dbsz, dt_len, d),
            heads(k_tail, bsz, keep), heads(v_tail, bsz, keep), jnp.swapaxes(st_p, -1, -2)[None],
            heads(ks, dbsz, dt_len), heads(vs, dbsz, dt_len), jnp.swapaxes(st_s, -1, -2)[None])
```

```python
import functools

import jax
import jax.numpy as jnp
from jax import lax
from jax.experimental import pallas as pl
from jax.experimental.pallas import tpu as pltpu

F32 = jnp.float32
BF16 = jnp.bfloat16

D_MODEL = 1024
CHUNK = 64
A_HEADS = 16
A_HEAD_DIM = D_MODEL // A_HEADS
BAND_CHUNKS = 8
BAND_PAST = BAND_CHUNKS * CHUNK
REL_CLIP = 128
B_HEADS = 4
B_KEY_DIM = D_MODEL // 2 // B_HEADS
B_VAL_DIM = D_MODEL // B_HEADS
B_QK = B_HEADS * B_KEY_DIM
B_VD = B_HEADS * B_VAL_DIM
B_GATE_RANK = 16
B_GATE_NORM = 16.0
D_FF = -(-8 * D_MODEL // 768) * 256
DEPTH = 2
ALPHA = (2.0 * DEPTH) ** 0.25
LN_EPS = 1e-5
GN_EPS = 1e-6
NEG_INF = -1e30

LANES = 128
SUB_BLOCK = 16
ATT_TILE = 4 * CHUNK
GLA_STEP_ROWS = 1024
GLA_WIDE_CHUNK = 256
GLA_MAX_LOG_DECAY = 60.0
VMEM_LIMIT = 56 * 1024 * 1024


def _dot(a, b):
    return jnp.dot(a, b, preferred_element_type=F32)


def _dot_nt(a, b):
    return lax.dot_general(a, b, (((1,), (1,)), ((), ())), preferred_element_type=F32)


def _dot_tn(a, b):
    return lax.dot_general(a, b, (((0,), (0,)), ((), ())), preferred_element_type=F32)


def _params(*sem):
    return pltpu.CompilerParams(dimension_semantics=sem, vmem_limit_bytes=VMEM_LIMIT)


def _row_tile(m, pref):
    t = min(m, pref)
    assert m % t == 0
    return t


def _qkv_kernel(x_ref, w_ref, q_ref, k_ref, v_ref, *tail_refs, n_chunk, tiles_per_seq):
    xb = x_ref[...].astype(BF16)
    tm, d = q_ref.shape
    if tail_refs:
        last_of_seq = pl.program_id(0) % tiles_per_seq == tiles_per_seq - 1
    for idx, o_ref in enumerate((q_ref, k_ref, v_ref)):
        for c in range(0, d, n_chunk):
            y = _dot(xb, w_ref[:, idx * d + c:idx * d + c + n_chunk].astype(BF16))
            if idx == 0:
                y = y * (A_HEAD_DIM ** -0.5)
            o_ref[:, c:c + n_chunk] = y.astype(o_ref.dtype)
            if tail_refs and idx > 0:
                tail_ref = tail_refs[idx - 1]

                @pl.when(last_of_seq)
                def _(y=y, c=c, tail_ref=tail_ref):
                    tail_ref[0, c:c + n_chunk, :] = y[tm - tail_ref.shape[2]:, :].T


def _qkv(x, w, dtype, t_len, tail_rows=0, tm=1024, name="qkv"):
    m, d = x.shape
    tm = _row_tile(t_len if tail_rows else m, tm)
    n_chunk = 512
    assert d % n_chunk == 0 and w.shape == (d, 3 * d) and tail_rows <= tm
    tiles_per_seq = t_len // tm
    rows = pl.BlockSpec((tm, d), lambda i: (i, 0))
    out_specs, out_shape = [rows] * 3, [jax.ShapeDtypeStruct((m, d), dtype)] * 3
    if tail_rows:
        out_specs += [pl.BlockSpec((1, d, tail_rows), lambda i: (i // tiles_per_seq, 0, 0))] * 2
        out_shape += [jax.ShapeDtypeStruct((m // t_len, d, tail_rows), F32)] * 2
    return pl.pallas_call(
        functools.partial(_qkv_kernel, n_chunk=n_chunk, tiles_per_seq=tiles_per_seq),
        grid=(m // tm,),
        in_specs=[rows, pl.BlockSpec(w.shape, lambda i: (0, 0), pipeline_mode=pl.Buffered(1))],
        out_specs=out_specs,
        out_shape=out_shape,
        compiler_params=_params("arbitrary"),
        name=name,
    )(x, w)


def _res_ln(x, y, g, b):
    t = ALPHA * x + y
    mu = jnp.mean(t, axis=-1, keepdims=True)
    d = t - mu
    var = jnp.mean(d * d, axis=-1, keepdims=True)
    return d * lax.rsqrt(var + LN_EPS) * g + b


def _layer_tail_kernel(a_ref, wp_ref, x_ref, g1_ref, b1_ref, wi_ref, wo_ref, g2_ref, b2_ref, o_ref, *h_refs,
                       ff_chunk):
    n = len(h_refs)
    tm = x_ref.shape[0]
    d_ff = h_refs[0].shape[1]
    rows = [slice(i * (tm // n), (i + 1) * (tm // n)) for i in range(n)]
    x1, y = [None] * n, [None] * n

    def ln1(i):
        x1[i] = _res_ln(x_ref[rows[i], :], _dot(a_ref[rows[i], :], wp_ref[...]), g1_ref[...], b1_ref[...])

    def ffn(i):
        xb = x1[i].astype(BF16)
        for c in range(0, d_ff, ff_chunk):
            gate = _dot(xb, wi_ref[:, c:c + ff_chunk])
            up = _dot(xb, wi_ref[:, d_ff + c:d_ff + c + ff_chunk])
            h_refs[i][:, c:c + ff_chunk] = (gate * jax.nn.sigmoid(gate) * up).astype(BF16)
        y[i] = _dot(h_refs[i][...], wo_ref[...])

    def ln2(i):
        o_ref[rows[i], :] = _res_ln(x1[i], y[i], g2_ref[...], b2_ref[...])

    ln1(0)
    for i in range(n):
        if i + 1 < n:
            ln1(i + 1)
        ffn(i)
        if i > 0:
            ln2(i - 1)
    ln2(n - 1)


def _layer_tail(a, wp, x, ln1, wi, wo, layer, ln2, tm=1024, name="layer_tail"):
    m, k = a.shape
    d = x.shape[1]
    d_ff = wo.shape[1]
    tm = _row_tile(m, tm)
    ff_chunk = 256
    row_groups = 4 if tm % 1024 == 0 else 1
    assert d_ff % ff_chunk == 0
    rows = lambda width: pl.BlockSpec((tm, width), lambda i: (i, 0))
    resident = lambda w: pl.BlockSpec(w.shape, lambda i: (0, 0), pipeline_mode=pl.Buffered(1))
    of_layer = lambda w: pl.BlockSpec((None,) + w.shape[1:], lambda i: (layer, 0, 0), pipeline_mode=pl.Buffered(1))
    vec = pl.BlockSpec((1, d), lambda i: (0, 0))
    return pl.pallas_call(
        functools.partial(_layer_tail_kernel, ff_chunk=ff_chunk),
        grid=(m // tm,),
        in_specs=[rows(k), resident(wp), rows(d), vec, vec, of_layer(wi), of_layer(wo), vec, vec],
        out_specs=rows(d),
        out_shape=jax.ShapeDtypeStruct((m, d), F32),
        scratch_shapes=[pltpu.VMEM((tm // row_groups, d_ff), BF16)] * row_groups,
        compiler_params=_params("parallel"),
        name=name,
    )(a, wp, x, ln1[0].reshape(1, d), ln1[1].reshape(1, d), wi, wo, ln2[0].reshape(1, d), ln2[1].reshape(1, d))


def _attn_prompt_kernel(q_ref, ka_ref, kb_ref, kc_ref, va_ref, vb_ref, vc_ref, w_ref, o_ref,
                        bias_ref, vext_ref):
    tile = pl.program_id(1)
    win = 3 * ATT_TILE

    @pl.when(tile <= 2)
    def _():
        qc = lax.broadcasted_iota(jnp.int32, (ATT_TILE, win), 0) // CHUNK
        kc = lax.broadcasted_iota(jnp.int32, (ATT_TILE, win), 1) // CHUNK
        valid = (kc >= qc) & (kc <= qc + BAND_CHUNKS) & (kc >= BAND_CHUNKS - tile * (ATT_TILE // CHUNK))
        for h in range(A_HEADS):
            w_rows = jnp.broadcast_to(w_ref[h:h + 1, :], (ATT_TILE, w_ref.shape[1]))
            toeplitz = pltpu.roll(w_rows, 0, 1, stride=1, stride_axis=0)[:, :win]
            bias_ref[h // 2, (h % 2) * ATT_TILE:(h % 2 + 1) * ATT_TILE, :] = jnp.where(valid, toeplitz, NEG_INF)

    n_groups = D_MODEL // LANES

    @pl.when(tile == 0)
    def _():
        vext_ref[:, :, :, LANES:] = jnp.ones((3, n_groups, ATT_TILE, LANES), BF16)

    k_refs, v_refs = (ka_ref, kb_ref, kc_ref), (va_ref, vb_ref, vc_ref)
    for j, v_ref in enumerate(v_refs):
        for p in range(n_groups):
            vext_ref[j, p, :, :LANES] = v_ref[0, :, p * LANES:(p + 1) * LANES]

    lane = lax.broadcasted_iota(jnp.int32, (ATT_TILE, LANES), 1)
    low_half = lane < A_HEAD_DIM

    def scores(p):
        cols = slice(p * LANES, (p + 1) * LANES)
        qp = q_ref[0, :, cols]
        zero = jnp.zeros_like(qp)
        q_pair = jnp.concatenate([jnp.where(low_half, qp, zero), jnp.where(low_half, zero, qp)], axis=0)
        return [_dot_nt(q_pair, k_ref[0, :, cols]) + bias_ref[p, :, j * ATT_TILE:(j + 1) * ATT_TILE]
                for j, k_ref in enumerate(k_refs)]

    ahead = 2
    pending = [scores(p) for p in range(ahead)]
    for p in range(n_groups):
        s_parts = pending.pop(0)
        if p + ahead < n_groups:
            pending.append(scores(p + ahead))
        m = functools.reduce(jnp.maximum, [s.max(axis=-1, keepdims=True) for s in s_parts])
        pv = sum(_dot(jnp.exp(s - m).astype(BF16), vext_ref[j, p]) for j, s in enumerate(s_parts))
        pv = pv[:, :LANES] / pv[:, LANES:]
        o_ref[0, :, p * LANES:(p + 1) * LANES] = jnp.where(low_half, pv[:ATT_TILE], pv[ATT_TILE:]).astype(o_ref.dtype)


def _band_bias_vectors(rel_table):
    n_heads = rel_table.shape[1]
    win = 3 * ATT_TILE
    t = rel_table.T.astype(F32)
    u = jnp.concatenate([jnp.broadcast_to(t[:, 2 * REL_CLIP:], (n_heads, win - REL_CLIP)),
                         jnp.flip(t[:, 1:2 * REL_CLIP], axis=1),
                         jnp.broadcast_to(t[:, :1], (n_heads, REL_CLIP))], axis=1)
    assert u.shape[1] == ATT_TILE + win - 1
    return jnp.roll(jnp.pad(u, ((0, 0), (0, 1))), -(ATT_TILE - 1), axis=1)


def _attn_prompt(q, k, v, bias_vectors):
    bsz, t_len, d = q.shape
    assert t_len % ATT_TILE == 0 and BAND_PAST == 2 * ATT_TILE
    blk = (1, ATT_TILE, d)

    def past(n):
        return lambda b, i: (b, jnp.maximum(i - n, 0), 0)

    kv_specs = [pl.BlockSpec(blk, past(2)), pl.BlockSpec(blk, past(1)), pl.BlockSpec(blk, past(0))]
    return pl.pallas_call(
        _attn_prompt_kernel,
        grid=(bsz, t_len // ATT_TILE),
        in_specs=[pl.BlockSpec(blk, lambda b, i: (b, i, 0))] + kv_specs + kv_specs
                 + [pl.BlockSpec(bias_vectors.shape, lambda b, i: (0, 0))],
        out_specs=pl.BlockSpec(blk, lambda b, i: (b, i, 0)),
        out_shape=jax.ShapeDtypeStruct((bsz, t_len, d), BF16),
        scratch_shapes=[pltpu.VMEM((A_HEADS // 2, 2 * ATT_TILE, 3 * ATT_TILE), F32),
                        pltpu.VMEM((3, d // LANES, ATT_TILE, 2 * LANES), BF16)],
        compiler_params=_params("parallel", "arbitrary"),
        name="attn_prompt",
    )(q, k, k, k, v, v, v, bias_vectors)


def _attn_sample_kernel(q_ref, kn_ref, vn_ref, kc_ref, vc_ref, w_ref, o_ref):
    t_len = q_ref.shape[1]
    n_cache = kc_ref.shape[2]
    lane = lax.broadcasted_iota(jnp.int32, (t_len, LANES), 1)
    low_half = lane < A_HEAD_DIM
    bias = [pltpu.roll(jnp.broadcast_to(w_ref[h:h + 1, :], (t_len, w_ref.shape[1])), 0, 1, stride=1, stride_axis=0)
            for h in range(A_HEADS)]
    for p in range(D_MODEL // LANES):
        cols = slice(p * LANES, (p + 1) * LANES)
        qp = q_ref[0, :, cols].astype(BF16)
        kc_t = kc_ref[0, cols, :].astype(BF16)
        vc_t = vc_ref[0, cols, :].astype(BF16)
        kn = kn_ref[0, :, cols].astype(BF16)
        vn = vn_ref[0, :, cols].astype(BF16)
        zero = jnp.zeros_like(qp)
        q_pair = jnp.concatenate([jnp.where(low_half, qp, zero), jnp.where(low_half, zero, qp)], axis=0)
        b_pair = jnp.concatenate([bias[2 * p], bias[2 * p + 1]], axis=0)
        s_c = _dot(q_pair, kc_t) + b_pair[:, :n_cache]
        s_n = _dot_nt(q_pair, kn) + b_pair[:, n_cache:n_cache + t_len]
        m = jnp.maximum(s_c.max(axis=-1, keepdims=True), s_n.max(axis=-1, keepdims=True))
        e_c, e_n = jnp.exp(s_c - m), jnp.exp(s_n - m)
        l = e_c.sum(axis=-1, keepdims=True) + e_n.sum(axis=-1, keepdims=True)
        pv = (_dot_nt(e_c.astype(BF16), vc_t) + _dot(e_n.astype(BF16), vn)) / l
        o_ref[0, :, cols] = jnp.where(low_half, pv[:t_len], pv[t_len:]).astype(o_ref.dtype)


def _sample_bias_vectors(rel_table, n_cache, t_len):
    assert t_len - 1 <= REL_CLIP <= n_cache + t_len
    n_heads = rel_table.shape[1]
    t = rel_table.T.astype(F32)
    u = jnp.concatenate([jnp.broadcast_to(t[:, 2 * REL_CLIP:], (n_heads, n_cache + t_len - REL_CLIP)),
                         jnp.flip(t[:, REL_CLIP - t_len + 1:2 * REL_CLIP], axis=1)], axis=1)
    assert u.shape[1] == n_cache + 2 * t_len - 1
    width = -(-u.shape[1] // LANES) * LANES
    return jnp.roll(jnp.pad(u, ((0, 0), (0, width - u.shape[1]))), -(t_len - 1), axis=1)


def _attn_sample(q, k_new, v_new, cache_k, cache_v, bias_vectors):
    bsz, t_len, d = q.shape
    n_cache = cache_k.shape[2]
    new_spec = pl.BlockSpec((1, t_len, d), lambda b: (b, 0, 0))
    cache_spec = pl.BlockSpec((1, d, n_cache), lambda b: (b, 0, 0))
    return pl.pallas_call(
        _attn_sample_kernel,
        grid=(bsz,),
        in_specs=[new_spec, new_spec, new_spec, cache_spec, cache_spec,
                  pl.BlockSpec(bias_vectors.shape, lambda b: (0, 0))],
        out_specs=new_spec,
        out_shape=jax.ShapeDtypeStruct((bsz, t_len, d), BF16),
        compiler_params=_params("parallel"),
        name="attn_sample",
    )(q, k_new, v_new, cache_k, cache_v, bias_vectors)


def _log_sigmoid(z):
    return jnp.minimum(z, 0.0) - jnp.log(1.0 + jnp.exp(-jnp.abs(z)))


def _gla_project(rows, x_ref, w_ref, wup_ref, bgk_ref, q_ref, k_ref, g_ref, v_ref, r_ref):
    xb = x_ref[0, rows, :].astype(BF16)
    n_chunk = 512
    proj = lambda lo, hi: _dot(xb, w_ref[:, lo:hi].astype(BF16))
    q_ref[0, rows, :] = proj(0, B_QK) * (B_KEY_DIM ** -0.5)
    k_ref[0, rows, :] = proj(B_QK, 2 * B_QK)
    for c in range(0, B_VD, n_chunk):
        v_ref[0, rows, c:c + n_chunk] = proj(2 * B_QK + c, 2 * B_QK + c + n_chunk)
        r_ref[0, rows, c:c + n_chunk] = proj(2 * B_QK + B_VD + c, 2 * B_QK + B_VD + c + n_chunk)
    n_main = 2 * B_QK + 2 * B_VD
    low = proj(n_main, n_main + B_GATE_RANK)
    z = _dot(low.astype(BF16), wup_ref[...].astype(BF16)) + bgk_ref[...]
    g_ref[0, rows, :] = _log_sigmoid(z) / B_GATE_NORM


def _split3(x):
    hi = x.astype(BF16)
    r1 = x - hi.astype(F32)
    mid = r1.astype(BF16)
    lo = (r1 - mid.astype(F32)).astype(BF16)
    return hi, mid, lo


def _gla_finish(o, h, rows, r_ref, gain_ref, a_ref):
    vcols = slice(h * B_VAL_DIM, (h + 1) * B_VAL_DIM)
    o = o * lax.rsqrt(jnp.mean(o * o, axis=-1, keepdims=True) + GN_EPS) * gain_ref[:, vcols]
    r_h = r_ref[0, rows, vcols]
    a_ref[0, rows, vcols] = (o * (r_h * jax.nn.sigmoid(r_h))).astype(a_ref.dtype)


def _gla_rows_as_one_chunk(rows, b, q_ref, k_ref, v_ref, r_ref, gain_ref, a_ref, st_ref):
    n = b.shape[0]
    q, k = q_ref[0, rows, :], k_ref[0, rows, :]
    b_last = b[n - 1:n, :]
    qhat = (q * jnp.exp(b)).astype(BF16)
    kinv = (k * jnp.exp(-b)).astype(BF16)
    kdec = (k * jnp.exp(b_last - b)).astype(BF16)
    decay = jnp.exp(b_last)
    causal = lax.broadcasted_iota(jnp.int32, (n, n), 1) <= lax.broadcasted_iota(jnp.int32, (n, n), 0)
    for h in range(B_HEADS):
        kcols = slice(h * B_KEY_DIM, (h + 1) * B_KEY_DIM)
        vcols = slice(h * B_VAL_DIM, (h + 1) * B_VAL_DIM)
        a_mat = jnp.where(causal, _dot_nt(qhat[:, kcols], kinv[:, kcols]), 0.0).astype(BF16)
        v_h = v_ref[0, rows, vcols].astype(BF16)
        st = st_ref[0, h]
        o = _dot(a_mat, v_h) + _dot_nt(qhat[:, kcols], st.astype(BF16))
        st_ref[0, h] = st * decay[:, kcols] + _dot_tn(v_h, kdec[:, kcols])
        _gla_finish(o, h, rows, r_ref, gain_ref, a_ref)


def _gla_kernel(x_ref, w_ref, wup_ref, bgk_ref, s0_ref, gain_ref, a_ref, st_ref,
                q_ref, k_ref, g_ref, v_ref, r_ref, st0_ref, *, chunk, n_chunks, wide_chunk):
    @pl.when(pl.program_id(1) == 0)
    def _():
        st_ref[...] = s0_ref[...]

    n = chunk * n_chunks
    project = functools.partial(_gla_project, x_ref=x_ref, w_ref=w_ref, wup_ref=wup_ref,
                                bgk_ref=bgk_ref, q_ref=q_ref, k_ref=k_ref, g_ref=g_ref, v_ref=v_ref, r_ref=r_ref)

    def by_sub_blocks():
        _gla_chunks_by_sub_blocks(q_ref, k_ref, g_ref, v_ref, r_ref, gain_ref, a_ref, st_ref,
                                  chunk=chunk, n_chunks=n_chunks)

    if not wide_chunk:
        project(slice(0, n))
        by_sub_blocks()
        return
    project(slice(0, n))
    st0_ref[...] = st_ref[...]
    tri = (lax.broadcasted_iota(jnp.int32, (wide_chunk, wide_chunk), 1)
           <= lax.broadcasted_iota(jnp.int32, (wide_chunk, wide_chunk), 0))
    tri = jnp.where(tri, 1.0, 0.0).astype(BF16)
    min_log_decay = None
    for r0 in range(0, n, wide_chunk):
        rows = slice(r0, r0 + wide_chunk)
        b = sum(_dot(tri, gp) for gp in _split3(g_ref[0, rows, :]))
        _gla_rows_as_one_chunk(rows, b, q_ref, k_ref, v_ref, r_ref, gain_ref, a_ref, st_ref)
        total = b[wide_chunk - 1:wide_chunk, :]
        min_log_decay = total if min_log_decay is None else jnp.minimum(min_log_decay, total)

    @pl.when(jnp.logical_not(jnp.min(min_log_decay) >= -GLA_MAX_LOG_DECAY))
    def _():
        st_ref[...] = st0_ref[...]
        by_sub_blocks()


def _gla_chunks_by_sub_blocks(q_ref, k_ref, g_ref, v_ref, r_ref, gain_ref, a_ref, st_ref, *, chunk, n_chunks):
    nsb = chunk // SUB_BLOCK
    row = lax.broadcasted_iota(jnp.int32, (chunk, chunk), 0)
    col = lax.broadcasted_iota(jnp.int32, (chunk, chunk), 1)
    blk_start = (row // SUB_BLOCK) * SUB_BLOCK
    tri_local = jnp.where((col <= row) & (col >= blk_start), 1.0, 0.0).astype(BF16)
    tri_before = jnp.where(col < blk_start, 1.0, 0.0).astype(BF16)
    rcol = lax.broadcasted_iota(jnp.int32, (SUB_BLOCK, chunk), 1)
    rrow = lax.broadcasted_iota(jnp.int32, (SUB_BLOCK, chunk), 0)

    def chunk_body(c, carry):
        row0 = pl.multiple_of(c * chunk, chunk)
        rows = pl.ds(row0, chunk)
        q = q_ref[0, rows, :]
        k = k_ref[0, rows, :]
        g_parts = _split3(g_ref[0, rows, :])
        bl = sum(_dot(tri_local, gp) for gp in g_parts)
        if nsb > 1:
            rr = sum(_dot(tri_before, gp) for gp in g_parts)
            b = bl + rr
        else:
            rr = None
            b = bl
        b_last = b[chunk - 1:chunk, :]
        qt = q * jnp.exp(bl)
        qhat = qt * jnp.exp(rr) if nsb > 1 else qt
        kdec = (k * jnp.exp(b_last - b)).astype(BF16)
        decay = jnp.exp(b_last)
        kb = k.astype(BF16)

        for h in range(B_HEADS):
            kcols = slice(h * B_KEY_DIM, (h + 1) * B_KEY_DIM)
            vcols = slice(h * B_VAL_DIM, (h + 1) * B_VAL_DIM)
            q_h, k_h, bl_h, b_h, qt_h = q[:, kcols], k[:, kcols], bl[:, kcols], b[:, kcols], qt[:, kcols]
            kb_h = kb[:, kcols]
            a_rows = []
            for i in range(nsb):
                sb = slice(i * SUB_BLOCK, (i + 1) * SUB_BLOCK)
                q_s, bl_s = q_h[sb], bl_h[sb]
                y = jnp.concatenate(
                    [q_s * jnp.exp(jnp.minimum(bl_s - bl_s[j:j + 1], 0.0)) for j in range(SUB_BLOCK)], axis=0)
                res = _dot_nt(y.astype(BF16), kb_h)
                a_i = jnp.zeros((SUB_BLOCK, chunk), F32)
                for j in range(SUB_BLOCK):
                    a_i = jnp.where(rcol == i * SUB_BLOCK + j, res[j * SUB_BLOCK:(j + 1) * SUB_BLOCK], a_i)
                a_i = jnp.where(rcol <= i * SUB_BLOCK + rrow, a_i, 0.0)
                if i > 0:
                    r_i = rr[i * SUB_BLOCK:i * SUB_BLOCK + 1, kcols]
                    kt = (k_h * jnp.exp(jnp.minimum(r_i - b_h, 0.0))).astype(BF16)
                    a_off = _dot_nt(qt_h[sb].astype(BF16), kt)
                    a_i = jnp.where(rcol < i * SUB_BLOCK, a_off, a_i)
                a_rows.append(a_i)
            a_mat = (jnp.concatenate(a_rows, axis=0) if nsb > 1 else a_rows[0]).astype(BF16)
            v_h = v_ref[0, rows, vcols].astype(BF16)
            st = st_ref[0, h]
            o = _dot(a_mat, v_h) + _dot_nt(qhat[:, kcols].astype(BF16), st.astype(BF16))
            st_ref[0, h] = st * decay[:, kcols] + _dot_tn(v_h, kdec[:, kcols])
            _gla_finish(o, h, rows, r_ref, gain_ref, a_ref)
        return carry

    lax.fori_loop(0, n_chunks, chunk_body, 0)


def _gla(x, w_in, w_up, b_gk, s0t, gain, chunk, chunks_per_step, wide_chunk, name="gla"):
    bsz, t_len, d = x.shape
    rows = chunk * chunks_per_step
    assert t_len % rows == 0
    const = lambda b, s: (0, 0)
    resident = lambda w: pl.BlockSpec(w.shape, const, pipeline_mode=pl.Buffered(1))
    st_spec = pl.BlockSpec((1, B_HEADS, B_VAL_DIM, B_KEY_DIM), lambda b, s: (b, 0, 0, 0))
    out_spec = pl.BlockSpec((1, rows, B_VD), lambda b, s: (b, s, 0))
    return pl.pallas_call(
        functools.partial(_gla_kernel, chunk=chunk, n_chunks=chunks_per_step, wide_chunk=wide_chunk),
        grid=(bsz, t_len // rows),
        in_specs=[pl.BlockSpec((1, rows, d), lambda b, s: (b, s, 0)),
                  resident(w_in), resident(w_up), pl.BlockSpec((1, B_QK), const),
                  st_spec, pl.BlockSpec((1, B_VD), const)],
        out_specs=[out_spec, st_spec],
        out_shape=[jax.ShapeDtypeStruct((bsz, t_len, B_VD), BF16),
                   jax.ShapeDtypeStruct(s0t.shape, F32)],
        scratch_shapes=[pltpu.VMEM((1, rows, B_QK), F32)] * 3 + [pltpu.VMEM((1, rows, B_VD), F32)] * 2
                       + [pltpu.VMEM((1, B_HEADS, B_VAL_DIM, B_KEY_DIM), F32)],
        compiler_params=_params("parallel", "arbitrary"),
        name=name,
    )(x, w_in, w_up, b_gk.reshape(1, B_QK), s0t, gain.reshape(1, B_VD))


def kernel(x_prompt, x_sample, cache_a_k, cache_a_v, state_b, w_in_a, rel_bias_a, w_out_a, w_in_b, w_gk_up_b,
           b_gk_b, gn_gain_b, w_out_b, w_ffn_in, w_ffn_out, ln1_g, ln1_b, ln2_g, ln2_b):
    bsz, t_len, d = x_prompt.shape
    dbsz, dt_len, _ = x_sample.shape
    n_cache = cache_a_k.shape[2]
    keep = min(BAND_PAST, t_len)
    xp = x_prompt.reshape(bsz * t_len, d)
    xs = x_sample.reshape(dbsz * dt_len, d)

    w_oa = w_out_a[0].astype(BF16)
    bias_p = _band_bias_vectors(rel_bias_a[0])
    bias_s = _sample_bias_vectors(rel_bias_a[0], n_cache, dt_len)

    q, k, v, k_tail, v_tail = _qkv(xp, w_in_a[0], BF16, t_len, tail_rows=keep, name="qkv_prompt")
    att = _attn_prompt(q.reshape(bsz, t_len, d), k.reshape(bsz, t_len, d), v.reshape(bsz, t_len, d), bias_p)

    qs, ks, vs = _qkv(xs, w_in_a[0], F32, dt_len, name="qkv_sample")
    heads_first = lambda a: jnp.transpose(a, (0, 2, 3, 1)).reshape(a.shape[0], d, a.shape[1])
    att_s = _attn_sample(qs.reshape(dbsz, dt_len, d), ks.reshape(dbsz, dt_len, d), vs.reshape(dbsz, dt_len, d),
                         heads_first(cache_a_k[0]), heads_first(cache_a_v[0]), bias_s)

    wi, wo = w_ffn_in.astype(BF16), w_ffn_out.astype(BF16)
    ln1, ln2 = (ln1_g[0], ln1_b[0]), (ln2_g[0], ln2_b[0])
    xp = _layer_tail(att.reshape(bsz * t_len, d), w_oa, xp, ln1, wi, wo, 0, ln2, name="tail0_prompt")
    xs = _layer_tail(att_s.reshape(dbsz * dt_len, d), w_oa, xs, ln1, wi, wo, 0, ln2, tm=128, name="tail0_sample")

    w_ob = w_out_b[0].astype(BF16)

    s0 = jnp.zeros((bsz, B_HEADS, B_VAL_DIM, B_KEY_DIM), F32)
    a_p, st_p = _gla(xp.reshape(bsz, t_len, d), w_in_b[0], w_gk_up_b[0], b_gk_b[0], s0, gn_gain_b[0],
                     CHUNK, min(GLA_STEP_ROWS, t_len) // CHUNK, GLA_WIDE_CHUNK, name="gla_prompt")
    a_s, st_s = _gla(xs.reshape(dbsz, dt_len, d), w_in_b[0], w_gk_up_b[0], b_gk_b[0],
                     jnp.swapaxes(state_b[0], -1, -2), gn_gain_b[0], dt_len, 1, None, name="gla_sample")

    ln1, ln2 = (ln1_g[1], ln1_b[1]), (ln2_g[1], ln2_b[1])
    xp = _layer_tail(a_p.reshape(bsz * t_len, B_VD), w_ob, xp, ln1, wi, wo, 1, ln2, name="tail1_prompt")
    xs = _layer_tail(a_s.reshape(dbsz * dt_len, B_VD), w_ob, xs, ln1, wi, wo, 1, ln2, tm=128, name="tail1_sample")

    heads = lambda a, n, t: a.reshape(1, n, t, A_HEADS, A_HEAD_DIM)
    seq_first = lambda a: jnp.transpose(a.reshape(a.shape[0], A_HEADS, A_HEAD_DIM, a.shape[2]), (0, 3, 1, 2))[None]
    return (xp.reshape(bsz, t_len, d), xs.reshape(dbsz, dt_len, d),
            seq_first(k_tail), seq_first(v_tail), jnp.swapaxes(st_p, -1, -2)[None],
            heads(ks, dbsz, dt_len), heads(vs, dbsz, dt_len), jnp.swapaxes(st_s, -1, -2)[None])
```

```python
import functools

import jax
import jax.numpy as jnp
from jax import lax
from jax.experimental import pallas as pl
from jax.experimental.pallas import tpu as pltpu

F32 = jnp.float32
BF16 = jnp.bfloat16

D_MODEL = 1024
CHUNK = 64
A_HEADS = 16
A_HEAD_DIM = D_MODEL // A_HEADS
BAND_CHUNKS = 8
BAND_PAST = BAND_CHUNKS * CHUNK
REL_CLIP = 128
B_HEADS = 4
B_KEY_DIM = D_MODEL // 2 // B_HEADS
B_VAL_DIM = D_MODEL // B_HEADS
B_QK = B_HEADS * B_KEY_DIM
B_VD = B_HEADS * B_VAL_DIM
B_GATE_RANK = 16
B_GATE_NORM = 16.0
D_FF = -(-8 * D_MODEL // 768) * 256
DEPTH = 2
ALPHA = (2.0 * DEPTH) ** 0.25
LN_EPS = 1e-5
GN_EPS = 1e-6
NEG_INF = -1e30

LANES = 128
SUB_BLOCK = 16
ATT_TILE = 4 * CHUNK
GLA_STEP_ROWS = 1024
GLA_WIDE_CHUNK = 256
GLA_MAX_LOG_DECAY = 60.0
VMEM_LIMIT = 56 * 1024 * 1024


def _dot(a, b):
    return jnp.dot(a, b, preferred_element_type=F32)


def _dot_nt(a, b):
    return lax.dot_general(a, b, (((1,), (1,)), ((), ())), preferred_element_type=F32)


def _dot_tn(a, b):
    return lax.dot_general(a, b, (((0,), (0,)), ((), ())), preferred_element_type=F32)


def _params(*sem):
    return pltpu.CompilerParams(dimension_semantics=sem, vmem_limit_bytes=VMEM_LIMIT)


def _row_tile(m, pref):
    t = min(m, pref)
    assert m % t == 0
    return t


def _qkv_kernel(x_ref, w_ref, q_ref, k_ref, v_ref, *tail_refs, n_chunk, tiles_per_seq):
    xb = x_ref[...].astype(BF16)
    tm, d = q_ref.shape
    if tail_refs:
        *tail_refs, stash_ref = tail_refs
    for idx, o_ref in enumerate((q_ref, k_ref, v_ref)):
        for c in range(0, d, n_chunk):
            y = _dot(xb, w_ref[:, idx * d + c:idx * d + c + n_chunk].astype(BF16))
            if idx == 0:
                y = y * (A_HEAD_DIM ** -0.5)
            o_ref[:, c:c + n_chunk] = y.astype(o_ref.dtype)
            if tail_refs and idx > 0:
                stash_ref[idx - 1, :, c:c + n_chunk] = y[tm - stash_ref.shape[1]:, :]
    if not tail_refs:
        return

    @pl.when(pl.program_id(0) % tiles_per_seq == tiles_per_seq - 1)
    def _():
        for idx, tail_ref in enumerate(tail_refs):
            for c in range(0, d, n_chunk):
                tail_ref[0, c:c + n_chunk, :] = stash_ref[idx, :, c:c + n_chunk].T


def _qkv(x, w, dtype, t_len, tail_rows=0, tm=1024, name="qkv"):
    m, d = x.shape
    tm = _row_tile(t_len if tail_rows else m, tm)
    n_chunk = 512
    assert d % n_chunk == 0 and w.shape == (d, 3 * d) and tail_rows <= tm
    tiles_per_seq = t_len // tm
    rows = pl.BlockSpec((tm, d), lambda i: (i, 0))
    out_specs, out_shape = [rows] * 3, [jax.ShapeDtypeStruct((m, d), dtype)] * 3
    if tail_rows:
        out_specs += [pl.BlockSpec((1, d, tail_rows), lambda i: (i // tiles_per_seq, 0, 0))] * 2
        out_shape += [jax.ShapeDtypeStruct((m // t_len, d, tail_rows), F32)] * 2
    return pl.pallas_call(
        functools.partial(_qkv_kernel, n_chunk=n_chunk, tiles_per_seq=tiles_per_seq),
        grid=(m // tm,),
        in_specs=[rows, pl.BlockSpec(w.shape, lambda i: (0, 0), pipeline_mode=pl.Buffered(1))],
        out_specs=out_specs,
        out_shape=out_shape,
        scratch_shapes=[pltpu.VMEM((2, tail_rows, d), F32)] if tail_rows else [],
        compiler_params=_params("arbitrary"),
        name=name,
    )(x, w)


def _res_ln(x, y, g, b):
    t = ALPHA * x + y
    mu = jnp.mean(t, axis=-1, keepdims=True)
    d = t - mu
    var = jnp.mean(d * d, axis=-1, keepdims=True)
    return d * lax.rsqrt(var + LN_EPS) * g + b


def _layer_tail_kernel(a_ref, wp_ref, x_ref, g1_ref, b1_ref, wi_ref, wo_ref, g2_ref, b2_ref, o_ref, *h_refs,
                       ff_chunk):
    n = len(h_refs)
    tm = x_ref.shape[0]
    d_ff = h_refs[0].shape[1]
    rows = [slice(i * (tm // n), (i + 1) * (tm // n)) for i in range(n)]
    x1, y = [None] * n, [None] * n

    def ln1(i):
        x1[i] = _res_ln(x_ref[rows[i], :], _dot(a_ref[rows[i], :], wp_ref[...]), g1_ref[...], b1_ref[...])

    def ffn(i):
        xb = x1[i].astype(BF16)
        for c in range(0, d_ff, ff_chunk):
            gate = _dot(xb, wi_ref[:, c:c + ff_chunk])
            up = _dot(xb, wi_ref[:, d_ff + c:d_ff + c + ff_chunk])
            h_refs[i][:, c:c + ff_chunk] = (gate * jax.nn.sigmoid(gate) * up).astype(BF16)
        y[i] = _dot(h_refs[i][...], wo_ref[...])

    def ln2(i):
        o_ref[rows[i], :] = _res_ln(x1[i], y[i], g2_ref[...], b2_ref[...])

    ln1(0)
    for i in range(n):
        if i + 1 < n:
            ln1(i + 1)
        ffn(i)
        if i > 0:
            ln2(i - 1)
    ln2(n - 1)


def _layer_tail(a, wp, x, ln1, wi, wo, layer, ln2, tm=1024, name="layer_tail"):
    m, k = a.shape
    d = x.shape[1]
    d_ff = wo.shape[1]
    tm = _row_tile(m, tm)
    ff_chunk = 256
    row_groups = 4 if tm % 1024 == 0 else 1
    assert d_ff % ff_chunk == 0
    rows = lambda width: pl.BlockSpec((tm, width), lambda i: (i, 0))
    resident = lambda w: pl.BlockSpec(w.shape, lambda i: (0, 0), pipeline_mode=pl.Buffered(1))
    of_layer = lambda w: pl.BlockSpec((None,) + w.shape[1:], lambda i: (layer, 0, 0), pipeline_mode=pl.Buffered(1))
    vec = pl.BlockSpec((1, d), lambda i: (0, 0))
    return pl.pallas_call(
        functools.partial(_layer_tail_kernel, ff_chunk=ff_chunk),
        grid=(m // tm,),
        in_specs=[rows(k), resident(wp), rows(d), vec, vec, of_layer(wi), of_layer(wo), vec, vec],
        out_specs=rows(d),
        out_shape=jax.ShapeDtypeStruct((m, d), F32),
        scratch_shapes=[pltpu.VMEM((tm // row_groups, d_ff), BF16)] * row_groups,
        compiler_params=_params("parallel"),
        name=name,
    )(a, wp, x, ln1[0].reshape(1, d), ln1[1].reshape(1, d), wi, wo, ln2[0].reshape(1, d), ln2[1].reshape(1, d))


def _attn_prompt_kernel(q_ref, ka_ref, kb_ref, kc_ref, va_ref, vb_ref, vc_ref, w_ref, o_ref,
                        bias_ref, vext_ref):
    tile = pl.program_id(1)
    win = 3 * ATT_TILE

    @pl.when(tile <= 2)
    def _():
        qc = lax.broadcasted_iota(jnp.int32, (ATT_TILE, win), 0) // CHUNK
        kc = lax.broadcasted_iota(jnp.int32, (ATT_TILE, win), 1) // CHUNK
        valid = (kc >= qc) & (kc <= qc + BAND_CHUNKS) & (kc >= BAND_CHUNKS - tile * (ATT_TILE // CHUNK))
        for h in range(A_HEADS):
            w_rows = jnp.broadcast_to(w_ref[h:h + 1, :], (ATT_TILE, w_ref.shape[1]))
            toeplitz = pltpu.roll(w_rows, 0, 1, stride=1, stride_axis=0)[:, :win]
            bias_ref[h // 2, (h % 2) * ATT_TILE:(h % 2 + 1) * ATT_TILE, :] = jnp.where(valid, toeplitz, NEG_INF)

    n_groups = D_MODEL // LANES

    @pl.when(tile == 0)
    def _():
        vext_ref[:, :, :, LANES:] = jnp.ones((3, n_groups, ATT_TILE, LANES), BF16)

    k_refs, v_refs = (ka_ref, kb_ref, kc_ref), (va_ref, vb_ref, vc_ref)
    for j, v_ref in enumerate(v_refs):
        for p in range(n_groups):
            vext_ref[j, p, :, :LANES] = v_ref[0, :, p * LANES:(p + 1) * LANES]

    lane = lax.broadcasted_iota(jnp.int32, (ATT_TILE, LANES), 1)
    low_half = lane < A_HEAD_DIM

    def scores(p):
        cols = slice(p * LANES, (p + 1) * LANES)
        qp = q_ref[0, :, cols]
        zero = jnp.zeros_like(qp)
        q_pair = jnp.concatenate([jnp.where(low_half, qp, zero), jnp.where(low_half, zero, qp)], axis=0)
        return [_dot_nt(q_pair, k_ref[0, :, cols]) + bias_ref[p, :, j * ATT_TILE:(j + 1) * ATT_TILE]
                for j, k_ref in enumerate(k_refs)]

    ahead = 2
    pending = [scores(p) for p in range(ahead)]
    for p in range(n_groups):
        s_parts = pending.pop(0)
        if p + ahead < n_groups:
            pending.append(scores(p + ahead))
        m = functools.reduce(jnp.maximum, [s.max(axis=-1, keepdims=True) for s in s_parts])
        pv = sum(_dot(jnp.exp(s - m).astype(BF16), vext_ref[j, p]) for j, s in enumerate(s_parts))
        pv = pv[:, :LANES] / pv[:, LANES:]
        o_ref[0, :, p * LANES:(p + 1) * LANES] = jnp.where(low_half, pv[:ATT_TILE], pv[ATT_TILE:]).astype(o_ref.dtype)


def _band_bias_vectors(rel_table):
    n_heads = rel_table.shape[1]
    win = 3 * ATT_TILE
    t = rel_table.T.astype(F32)
    u = jnp.concatenate([jnp.broadcast_to(t[:, 2 * REL_CLIP:], (n_heads, win - REL_CLIP)),
                         jnp.flip(t[:, 1:2 * REL_CLIP], axis=1),
                         jnp.broadcast_to(t[:, :1], (n_heads, REL_CLIP))], axis=1)
    assert u.shape[1] == ATT_TILE + win - 1
    return jnp.roll(jnp.pad(u, ((0, 0), (0, 1))), -(ATT_TILE - 1), axis=1)


def _attn_prompt(q, k, v, bias_vectors):
    bsz, t_len, d = q.shape
    assert t_len % ATT_TILE == 0 and BAND_PAST == 2 * ATT_TILE
    blk = (1, ATT_TILE, d)

    def past(n):
        return lambda b, i: (b, jnp.maximum(i - n, 0), 0)

    kv_specs = [pl.BlockSpec(blk, past(2)), pl.BlockSpec(blk, past(1)), pl.BlockSpec(blk, past(0))]
    return pl.pallas_call(
        _attn_prompt_kernel,
        grid=(bsz, t_len // ATT_TILE),
        in_specs=[pl.BlockSpec(blk, lambda b, i: (b, i, 0))] + kv_specs + kv_specs
                 + [pl.BlockSpec(bias_vectors.shape, lambda b, i: (0, 0))],
        out_specs=pl.BlockSpec(blk, lambda b, i: (b, i, 0)),
        out_shape=jax.ShapeDtypeStruct((bsz, t_len, d), BF16),
        scratch_shapes=[pltpu.VMEM((A_HEADS // 2, 2 * ATT_TILE, 3 * ATT_TILE), F32),
                        pltpu.VMEM((3, d // LANES, ATT_TILE, 2 * LANES), BF16)],
        compiler_params=_params("parallel", "arbitrary"),
        name="attn_prompt",
    )(q, k, k, k, v, v, v, bias_vectors)


def _attn_sample_kernel(q_ref, kn_ref, vn_ref, kc_ref, vc_ref, w_ref, o_ref):
    t_len = q_ref.shape[1]
    n_cache = kc_ref.shape[2]
    lane = lax.broadcasted_iota(jnp.int32, (t_len, LANES), 1)
    low_half = lane < A_HEAD_DIM
    bias = [pltpu.roll(jnp.broadcast_to(w_ref[h:h + 1, :], (t_len, w_ref.shape[1])), 0, 1, stride=1, stride_axis=0)
            for h in range(A_HEADS)]
    for p in range(D_MODEL // LANES):
        cols = slice(p * LANES, (p + 1) * LANES)
        qp = q_ref[0, :, cols].astype(BF16)
        kc_t = kc_ref[0, cols, :].astype(BF16)
        vc_t = vc_ref[0, cols, :].astype(BF16)
        kn = kn_ref[0, :, cols].astype(BF16)
        vn = vn_ref[0, :, cols].astype(BF16)
        zero = jnp.zeros_like(qp)
        q_pair = jnp.concatenate([jnp.where(low_half, qp, zero), jnp.where(low_half, zero, qp)], axis=0)
        b_pair = jnp.concatenate([bias[2 * p], bias[2 * p + 1]], axis=0)
        s_c = _dot(q_pair, kc_t) + b_pair[:, :n_cache]
        s_n = _dot_nt(q_pair, kn) + b_pair[:, n_cache:n_cache + t_len]
        m = jnp.maximum(s_c.max(axis=-1, keepdims=True), s_n.max(axis=-1, keepdims=True))
        e_c, e_n = jnp.exp(s_c - m), jnp.exp(s_n - m)
        l = e_c.sum(axis=-1, keepdims=True) + e_n.sum(axis=-1, keepdims=True)
        pv = (_dot_nt(e_c.astype(BF16), vc_t) + _dot(e_n.astype(BF16), vn)) / l
        o_ref[0, :, cols] = jnp.where(low_half, pv[:t_len], pv[t_len:]).astype(o_ref.dtype)


def _sample_bias_vectors(rel_table, n_cache, t_len):
    assert t_len - 1 <= REL_CLIP <= n_cache + t_len
    n_heads = rel_table.shape[1]
    t = rel_table.T.astype(F32)
    u = jnp.concatenate([jnp.broadcast_to(t[:, 2 * REL_CLIP:], (n_heads, n_cache + t_len - REL_CLIP)),
                         jnp.flip(t[:, REL_CLIP - t_len + 1:2 * REL_CLIP], axis=1)], axis=1)
    assert u.shape[1] == n_cache + 2 * t_len - 1
    width = -(-u.shape[1] // LANES) * LANES
    return jnp.roll(jnp.pad(u, ((0, 0), (0, width - u.shape[1]))), -(t_len - 1), axis=1)


def _attn_sample(q, k_new, v_new, cache_k, cache_v, bias_vectors):
    bsz, t_len, d = q.shape
    n_cache = cache_k.shape[2]
    new_spec = pl.BlockSpec((1, t_len, d), lambda b: (b, 0, 0))
    cache_spec = pl.BlockSpec((1, d, n_cache), lambda b: (b, 0, 0))
    return pl.pallas_call(
        _attn_sample_kernel,
        grid=(bsz,),
        in_specs=[new_spec, new_spec, new_spec, cache_spec, cache_spec,
                  pl.BlockSpec(bias_vectors.shape, lambda b: (0, 0))],
        out_specs=new_spec,
        out_shape=jax.ShapeDtypeStruct((bsz, t_len, d), BF16),
        compiler_params=_params("parallel"),
        name="attn_sample",
    )(q, k_new, v_new, cache_k, cache_v, bias_vectors)


def _log_sigmoid(z):
    return jnp.minimum(z, 0.0) - jnp.log(1.0 + jnp.exp(-jnp.abs(z)))


def _gla_project(rows, x_ref, w_ref, wup_ref, bgk_ref, q_ref, k_ref, g_ref, v_ref, r_ref):
    xb = x_ref[0, rows, :].astype(BF16)
    n_chunk = 512
    proj = lambda lo, hi: _dot_nt(xb, w_ref[lo:hi, :].astype(BF16))
    q_ref[0, rows, :] = proj(0, B_QK) * (B_KEY_DIM ** -0.5)
    k_ref[0, rows, :] = proj(B_QK, 2 * B_QK)
    for c in range(0, B_VD, n_chunk):
        v_ref[0, rows, c:c + n_chunk] = proj(2 * B_QK + c, 2 * B_QK + c + n_chunk)
        r_ref[0, rows, c:c + n_chunk] = proj(2 * B_QK + B_VD + c, 2 * B_QK + B_VD + c + n_chunk)
    n_main = 2 * B_QK + 2 * B_VD
    low = proj(n_main, n_main + B_GATE_RANK)
    z = _dot(low.astype(BF16), wup_ref[...].astype(BF16)) + bgk_ref[...]
    g_ref[0, rows, :] = _log_sigmoid(z) / B_GATE_NORM


def _split3(x):
    hi = x.astype(BF16)
    r1 = x - hi.astype(F32)
    mid = r1.astype(BF16)
    lo = (r1 - mid.astype(F32)).astype(BF16)
    return hi, mid, lo


def _gla_finish(o, h, rows, r_ref, gain_ref, a_ref):
    vcols = slice(h * B_VAL_DIM, (h + 1) * B_VAL_DIM)
    o = o * lax.rsqrt(jnp.mean(o * o, axis=-1, keepdims=True) + GN_EPS) * gain_ref[:, vcols]
    r_h = r_ref[0, rows, vcols]
    a_ref[0, rows, vcols] = (o * (r_h * jax.nn.sigmoid(r_h))).astype(a_ref.dtype)


def _gla_rows_as_one_chunk(rows, b, q_ref, k_ref, v_ref, r_ref, gain_ref, a_ref, st_ref):
    n = b.shape[0]
    q, k = q_ref[0, rows, :], k_ref[0, rows, :]
    b_last = b[n - 1:n, :]
    qhat = (q * jnp.exp(b)).astype(BF16)
    kinv = (k * jnp.exp(-b)).astype(BF16)
    kdec = (k * jnp.exp(b_last - b)).astype(BF16)
    decay = jnp.exp(b_last)
    causal = lax.broadcasted_iota(jnp.int32, (n, n), 1) <= lax.broadcasted_iota(jnp.int32, (n, n), 0)
    for h in range(B_HEADS):
        kcols = slice(h * B_KEY_DIM, (h + 1) * B_KEY_DIM)
        vcols = slice(h * B_VAL_DIM, (h + 1) * B_VAL_DIM)
        a_mat = jnp.where(causal, _dot_nt(qhat[:, kcols], kinv[:, kcols]), 0.0).astype(BF16)
        v_h = v_ref[0, rows, vcols].astype(BF16)
        st = st_ref[0, h]
        o = _dot(a_mat, v_h) + _dot_nt(qhat[:, kcols], st.astype(BF16))
        st_ref[0, h] = st * decay[:, kcols] + _dot_tn(v_h, kdec[:, kcols])
        _gla_finish(o, h, rows, r_ref, gain_ref, a_ref)


def _gla_kernel(x_ref, w_ref, wup_ref, bgk_ref, s0_ref, gain_ref, a_ref, st_ref,
                q_ref, k_ref, g_ref, v_ref, r_ref, st0_ref, *, chunk, n_chunks, wide_chunk):
    @pl.when(pl.program_id(1) == 0)
    def _():
        st_ref[...] = s0_ref[...]

    n = chunk * n_chunks
    project = functools.partial(_gla_project, x_ref=x_ref, w_ref=w_ref, wup_ref=wup_ref,
                                bgk_ref=bgk_ref, q_ref=q_ref, k_ref=k_ref, g_ref=g_ref, v_ref=v_ref, r_ref=r_ref)

    def by_sub_blocks():
        _gla_chunks_by_sub_blocks(q_ref, k_ref, g_ref, v_ref, r_ref, gain_ref, a_ref, st_ref,
                                  chunk=chunk, n_chunks=n_chunks)

    if not wide_chunk:
        project(slice(0, n))
        by_sub_blocks()
        return
    project(slice(0, n))
    st0_ref[...] = st_ref[...]
    tri = (lax.broadcasted_iota(jnp.int32, (wide_chunk, wide_chunk), 1)
           <= lax.broadcasted_iota(jnp.int32, (wide_chunk, wide_chunk), 0))
    tri = jnp.where(tri, 1.0, 0.0).astype(BF16)
    min_log_decay = None
    for r0 in range(0, n, wide_chunk):
        rows = slice(r0, r0 + wide_chunk)
        b = sum(_dot(tri, gp) for gp in _split3(g_ref[0, rows, :]))
        _gla_rows_as_one_chunk(rows, b, q_ref, k_ref, v_ref, r_ref, gain_ref, a_ref, st_ref)
        total = b[wide_chunk - 1:wide_chunk, :]
        min_log_decay = total if min_log_decay is None else jnp.minimum(min_log_decay, total)

    @pl.when(jnp.logical_not(jnp.min(min_log_decay) >= -GLA_MAX_LOG_DECAY))
    def _():
        st_ref[...] = st0_ref[...]
        by_sub_blocks()


def _gla_chunks_by_sub_blocks(q_ref, k_ref, g_ref, v_ref, r_ref, gain_ref, a_ref, st_ref, *, chunk, n_chunks):
    nsb = chunk // SUB_BLOCK
    row = lax.broadcasted_iota(jnp.int32, (chunk, chunk), 0)
    col = lax.broadcasted_iota(jnp.int32, (chunk, chunk), 1)
    blk_start = (row // SUB_BLOCK) * SUB_BLOCK
    tri_local = jnp.where((col <= row) & (col >= blk_start), 1.0, 0.0).astype(BF16)
    tri_before = jnp.where(col < blk_start, 1.0, 0.0).astype(BF16)
    rcol = lax.broadcasted_iota(jnp.int32, (SUB_BLOCK, chunk), 1)
    rrow = lax.broadcasted_iota(jnp.int32, (SUB_BLOCK, chunk), 0)

    def chunk_body(c, carry):
        row0 = pl.multiple_of(c * chunk, chunk)
        rows = pl.ds(row0, chunk)
        q = q_ref[0, rows, :]
        k = k_ref[0, rows, :]
        g_parts = _split3(g_ref[0, rows, :])
        bl = sum(_dot(tri_local, gp) for gp in g_parts)
        if nsb > 1:
            rr = sum(_dot(tri_before, gp) for gp in g_parts)
            b = bl + rr
        else:
            rr = None
            b = bl
        b_last = b[chunk - 1:chunk, :]
        qt = q * jnp.exp(bl)
        qhat = qt * jnp.exp(rr) if nsb > 1 else qt
        kdec = (k * jnp.exp(b_last - b)).astype(BF16)
        decay = jnp.exp(b_last)
        kb = k.astype(BF16)

        for h in range(B_HEADS):
            kcols = slice(h * B_KEY_DIM, (h + 1) * B_KEY_DIM)
            vcols = slice(h * B_VAL_DIM, (h + 1) * B_VAL_DIM)
            q_h, k_h, bl_h, b_h, qt_h = q[:, kcols], k[:, kcols], bl[:, kcols], b[:, kcols], qt[:, kcols]
            kb_h = kb[:, kcols]
            a_rows = []
            for i in range(nsb):
                sb = slice(i * SUB_BLOCK, (i + 1) * SUB_BLOCK)
                q_s, bl_s = q_h[sb], bl_h[sb]
                y = jnp.concatenate(
                    [q_s * jnp.exp(jnp.minimum(bl_s - bl_s[j:j + 1], 0.0)) for j in range(SUB_BLOCK)], axis=0)
                res = _dot_nt(y.astype(BF16), kb_h)
                a_i = jnp.zeros((SUB_BLOCK, chunk), F32)
                for j in range(SUB_BLOCK):
                    a_i = jnp.where(rcol == i * SUB_BLOCK + j, res[j * SUB_BLOCK:(j + 1) * SUB_BLOCK], a_i)
                a_i = jnp.where(rcol <= i * SUB_BLOCK + rrow, a_i, 0.0)
                if i > 0:
                    r_i = rr[i * SUB_BLOCK:i * SUB_BLOCK + 1, kcols]
                    kt = (k_h * jnp.exp(jnp.minimum(r_i - b_h, 0.0))).astype(BF16)
                    a_off = _dot_nt(qt_h[sb].astype(BF16), kt)
                    a_i = jnp.where(rcol < i * SUB_BLOCK, a_off, a_i)
                a_rows.append(a_i)
            a_mat = (jnp.concatenate(a_rows, axis=0) if nsb > 1 else a_rows[0]).astype(BF16)
            v_h = v_ref[0, rows, vcols].astype(BF16)
            st = st_ref[0, h]
            o = _dot(a_mat, v_h) + _dot_nt(qhat[:, kcols].astype(BF16), st.astype(BF16))
            st_ref[0, h] = st * decay[:, kcols] + _dot_tn(v_h, kdec[:, kcols])
            _gla_finish(o, h, rows, r_ref, gain_ref, a_ref)
        return carry

    lax.fori_loop(0, n_chunks, chunk_body, 0)


def _gla(x, w_in, w_up, b_gk, s0t, gain, chunk, chunks_per_step, wide_chunk, name="gla"):
    bsz, t_len, d = x.shape
    rows = chunk * chunks_per_step
    assert t_len % rows == 0
    const = lambda b, s: (0, 0)
    resident = lambda w: pl.BlockSpec(w.shape, const, pipeline_mode=pl.Buffered(1))
    st_spec = pl.BlockSpec((1, B_HEADS, B_VAL_DIM, B_KEY_DIM), lambda b, s: (b, 0, 0, 0))
    out_spec = pl.BlockSpec((1, rows, B_VD), lambda b, s: (b, s, 0))
    return pl.pallas_call(
        functools.partial(_gla_kernel, chunk=chunk, n_chunks=chunks_per_step, wide_chunk=wide_chunk),
        grid=(bsz, t_len // rows),
        in_specs=[pl.BlockSpec((1, rows, d), lambda b, s: (b, s, 0)),
                  resident(w_in), resident(w_up), pl.BlockSpec((1, B_QK), const),
                  st_spec, pl.BlockSpec((1, B_VD), const)],
        out_specs=[out_spec, st_spec],
        out_shape=[jax.ShapeDtypeStruct((bsz, t_len, B_VD), BF16),
                   jax.ShapeDtypeStruct(s0t.shape, F32)],
        scratch_shapes=[pltpu.VMEM((1, rows, B_QK), F32)] * 3 + [pltpu.VMEM((1, rows, B_VD), F32)] * 2
                       + [pltpu.VMEM((1, B_HEADS, B_VAL_DIM, B_KEY_DIM), F32)],
        compiler_params=_params("parallel", "arbitrary"),
        name=name,
    )(x, w_in, w_up, b_gk.reshape(1, B_QK), s0t, gain.reshape(1, B_VD))


def kernel(x_prompt, x_sample, cache_a_k, cache_a_v, state_b, w_in_a, rel_bias_a, w_out_a, w_in_b, w_gk_up_b,
           b_gk_b, gn_gain_b, w_out_b, w_ffn_in, w_ffn_out, ln1_g, ln1_b, ln2_g, ln2_b):
    bsz, t_len, d = x_prompt.shape
    dbsz, dt_len, _ = x_sample.shape
    n_cache = cache_a_k.shape[2]
    keep = min(BAND_PAST, t_len)
    xp = x_prompt.reshape(bsz * t_len, d)
    xs = x_sample.reshape(dbsz * dt_len, d)

    w_oa = w_out_a[0].astype(BF16)
    bias_p = _band_bias_vectors(rel_bias_a[0])
    bias_s = _sample_bias_vectors(rel_bias_a[0], n_cache, dt_len)

    q, k, v, k_tail, v_tail = _qkv(xp, w_in_a[0], BF16, t_len, tail_rows=keep, name="qkv_prompt")
    att = _attn_prompt(q.reshape(bsz, t_len, d), k.reshape(bsz, t_len, d), v.reshape(bsz, t_len, d), bias_p)

    qs, ks, vs = _qkv(xs, w_in_a[0], F32, dt_len, name="qkv_sample")
    heads_first = lambda a: jnp.transpose(a, (0, 2, 3, 1)).reshape(a.shape[0], d, a.shape[1])
    att_s = _attn_sample(qs.reshape(dbsz, dt_len, d), ks.reshape(dbsz, dt_len, d), vs.reshape(dbsz, dt_len, d),
                         heads_first(cache_a_k[0]), heads_first(cache_a_v[0]), bias_s)

    wi, wo = w_ffn_in.astype(BF16), w_ffn_out.astype(BF16)
    ln1, ln2 = (ln1_g[0], ln1_b[0]), (ln2_g[0], ln2_b[0])
    xp = _layer_tail(att.reshape(bsz * t_len, d), w_oa, xp, ln1, wi, wo, 0, ln2, name="tail0_prompt")
    xs = _layer_tail(att_s.reshape(dbsz * dt_len, d), w_oa, xs, ln1, wi, wo, 0, ln2, tm=128, name="tail0_sample")

    w_ob = w_out_b[0].astype(BF16)
    w_in_t = w_in_b[0].T

    s0 = jnp.zeros((bsz, B_HEADS, B_VAL_DIM, B_KEY_DIM), F32)
    a_p, st_p = _gla(xp.reshape(bsz, t_len, d), w_in_t, w_gk_up_b[0], b_gk_b[0], s0, gn_gain_b[0],
                     CHUNK, min(GLA_STEP_ROWS, t_len) // CHUNK, GLA_WIDE_CHUNK, name="gla_prompt")
    a_s, st_s = _gla(xs.reshape(dbsz, dt_len, d), w_in_t, w_gk_up_b[0], b_gk_b[0],
                     jnp.swapaxes(state_b[0], -1, -2), gn_gain_b[0], dt_len, 1, None, name="gla_sample")

    ln1, ln2 = (ln1_g[1], ln1_b[1]), (ln2_g[1], ln2_b[1])
    xp = _layer_tail(a_p.reshape(bsz * t_len, B_VD), w_ob, xp, ln1, wi, wo, 1, ln2, name="tail1_prompt")
    xs = _layer_tail(a_s.reshape(dbsz * dt_len, B_VD), w_ob, xs, ln1, wi, wo, 1, ln2, tm=128, name="tail1_sample")

    heads = lambda a, n, t: a.reshape(1, n, t, A_HEADS, A_HEAD_DIM)
    seq_first = lambda a: jnp.transpose(a.reshape(a.shape[0], A_HEADS, A_HEAD_DIM, a.shape[2]), (0, 3, 1, 2))[None]
    return (xp.reshape(bsz, t_len, d), xs.reshape(dbsz, dt_len, d),
            seq_first(k_tail), seq_first(v_tail), jnp.swapaxes(st_p, -1, -2)[None],
            heads(ks, dbsz, dt_len), heads(vs, dbsz, dt_len), jnp.swapaxes(st_s, -1, -2)[None])
```

```python
import functools

import jax
import jax.numpy as jnp
from jax import lax
from jax.experimental import pallas as pl
from jax.experimental.pallas import tpu as pltpu

F32 = jnp.float32
BF16 = jnp.bfloat16

D_MODEL = 1024
CHUNK = 64
A_HEADS = 16
A_HEAD_DIM = D_MODEL // A_HEADS
BAND_CHUNKS = 8
BAND_PAST = BAND_CHUNKS * CHUNK
REL_CLIP = 128
B_HEADS = 4
B_KEY_DIM = D_MODEL // 2 // B_HEADS
B_VAL_DIM = D_MODEL // B_HEADS
B_QK = B_HEADS * B_KEY_DIM
B_VD = B_HEADS * B_VAL_DIM
B_GATE_RANK = 16
B_GATE_NORM = 16.0
D_FF = -(-8 * D_MODEL // 768) * 256
DEPTH = 2
ALPHA = (2.0 * DEPTH) ** 0.25
LN_EPS = 1e-5
GN_EPS = 1e-6
NEG_INF = -1e30

LANES = 128
SUB_BLOCK = 16
ATT_TILE = 4 * CHUNK
GLA_STEP_ROWS = 1024
GLA_WIDE_CHUNK = 256
GLA_MAX_LOG_DECAY = 60.0
V7X_VMEM_BYTES = 64 * 1024 * 1024
VMEM_LIMIT = V7X_VMEM_BYTES * 7 // 8


def _dot(a, b):
    return jnp.dot(a, b, preferred_element_type=F32)


def _dot_nt(a, b):
    return lax.dot_general(a, b, (((1,), (1,)), ((), ())), preferred_element_type=F32)


def _dot_tn(a, b):
    return lax.dot_general(a, b, (((0,), (0,)), ((), ())), preferred_element_type=F32)


def _params(*sem):
    return pltpu.CompilerParams(dimension_semantics=sem, vmem_limit_bytes=VMEM_LIMIT)


def _row_tile(m, pref):
    t = min(m, pref)
    assert m % t == 0
    return t


def _qkv_kernel(x_ref, w_ref, q_ref, k_ref, v_ref, *tail_refs, n_chunk, tiles_per_seq):
    xb = x_ref[...].astype(BF16)
    tm, d = q_ref.shape
    if tail_refs:
        *tail_refs, stash_ref = tail_refs
    for idx, o_ref in enumerate((q_ref, k_ref, v_ref)):
        for c in range(0, d, n_chunk):
            y = _dot(xb, w_ref[:, idx * d + c:idx * d + c + n_chunk].astype(BF16))
            if idx == 0:
                y = y * (A_HEAD_DIM ** -0.5)
            o_ref[:, c:c + n_chunk] = y.astype(o_ref.dtype)
            if tail_refs and idx > 0:
                stash_ref[idx - 1, :, c:c + n_chunk] = y[tm - stash_ref.shape[1]:, :]
    if not tail_refs:
        return

    @pl.when(pl.program_id(0) % tiles_per_seq == tiles_per_seq - 1)
    def _():
        for idx, tail_ref in enumerate(tail_refs):
            for c in range(0, d, n_chunk):
                tail_ref[0, c:c + n_chunk, :] = stash_ref[idx, :, c:c + n_chunk].T


def _qkv(x, w, dtype, t_len, tail_rows=0, tm=1024, name="qkv"):
    m, d = x.shape
    tm = _row_tile(t_len if tail_rows else m, tm)
    n_chunk = 512
    assert d % n_chunk == 0 and w.shape == (d, 3 * d) and tail_rows <= tm
    tiles_per_seq = t_len // tm
    rows = pl.BlockSpec((tm, d), lambda i: (i, 0))
    out_specs, out_shape = [rows] * 3, [jax.ShapeDtypeStruct((m, d), dtype)] * 3
    if tail_rows:
        out_specs += [pl.BlockSpec((1, d, tail_rows), lambda i: (i // tiles_per_seq, 0, 0))] * 2
        out_shape += [jax.ShapeDtypeStruct((m // t_len, d, tail_rows), F32)] * 2
    return pl.pallas_call(
        functools.partial(_qkv_kernel, n_chunk=n_chunk, tiles_per_seq=tiles_per_seq),
        grid=(m // tm,),
        in_specs=[rows, pl.BlockSpec(w.shape, lambda i: (0, 0), pipeline_mode=pl.Buffered(1))],
        out_specs=out_specs,
        out_shape=out_shape,
        scratch_shapes=[pltpu.VMEM((2, tail_rows, d), F32)] if tail_rows else [],
        compiler_params=_params("arbitrary"),
        name=name,
    )(x, w)


def _res_ln(x, y, g, b):
    t = ALPHA * x + y
    mu = jnp.mean(t, axis=-1, keepdims=True)
    d = t - mu
    var = jnp.mean(d * d, axis=-1, keepdims=True)
    return d * lax.rsqrt(var + LN_EPS) * g + b


def _layer_tail_kernel(a_ref, wp_ref, x_ref, g1_ref, b1_ref, wi_ref, wo_ref, g2_ref, b2_ref, o_ref, *h_refs,
                       ff_chunk):
    n = len(h_refs)
    tm = x_ref.shape[0]
    d_ff = h_refs[0].shape[1]
    rows = [slice(i * (tm // n), (i + 1) * (tm // n)) for i in range(n)]
    x1, y = [None] * n, [None] * n

    def ln1(i):
        x1[i] = _res_ln(x_ref[rows[i], :], _dot(a_ref[rows[i], :], wp_ref[...]), g1_ref[...], b1_ref[...])

    def ffn(i):
        xb = x1[i].astype(BF16)
        for c in range(0, d_ff, ff_chunk):
            gate = _dot(xb, wi_ref[:, c:c + ff_chunk])
            up = _dot(xb, wi_ref[:, d_ff + c:d_ff + c + ff_chunk])
            h_refs[i][:, c:c + ff_chunk] = (gate * jax.nn.sigmoid(gate) * up).astype(BF16)
        y[i] = _dot(h_refs[i][...], wo_ref[...])

    def ln2(i):
        o_ref[rows[i], :] = _res_ln(x1[i], y[i], g2_ref[...], b2_ref[...])

    ln1(0)
    for i in range(n):
        if i + 1 < n:
            ln1(i + 1)
        ffn(i)
        if i > 0:
            ln2(i - 1)
    ln2(n - 1)


def _layer_tail(a, wp, x, ln1, wi, wo, layer, ln2, tm=1024, name="layer_tail"):
    m, k = a.shape
    d = x.shape[1]
    d_ff = wo.shape[1]
    tm = _row_tile(m, tm)
    ff_chunk = 256
    row_groups = 4 if tm % 1024 == 0 else 1
    assert d_ff % ff_chunk == 0
    rows = lambda width: pl.BlockSpec((tm, width), lambda i: (i, 0))
    resident = lambda w: pl.BlockSpec(w.shape, lambda i: (0, 0), pipeline_mode=pl.Buffered(1))
    of_layer = lambda w: pl.BlockSpec((None,) + w.shape[1:], lambda i: (layer, 0, 0), pipeline_mode=pl.Buffered(1))
    vec = pl.BlockSpec((1, d), lambda i: (0, 0))
    return pl.pallas_call(
        functools.partial(_layer_tail_kernel, ff_chunk=ff_chunk),
        grid=(m // tm,),
        in_specs=[rows(k), resident(wp), rows(d), vec, vec, of_layer(wi), of_layer(wo), vec, vec],
        out_specs=rows(d),
        out_shape=jax.ShapeDtypeStruct((m, d), F32),
        scratch_shapes=[pltpu.VMEM((tm // row_groups, d_ff), BF16)] * row_groups,
        compiler_params=_params("parallel"),
        name=name,
    )(a, wp, x, ln1[0].reshape(1, d), ln1[1].reshape(1, d), wi, wo, ln2[0].reshape(1, d), ln2[1].reshape(1, d))


def _attn_prompt_kernel(q_ref, ka_ref, kb_ref, kc_ref, va_ref, vb_ref, vc_ref, w_ref, o_ref,
                        bias_ref, vext_ref):
    tile = pl.program_id(1)
    win = 3 * ATT_TILE

    @pl.when(tile <= 2)
    def _():
        qc = lax.broadcasted_iota(jnp.int32, (ATT_TILE, win), 0) // CHUNK
        kc = lax.broadcasted_iota(jnp.int32, (ATT_TILE, win), 1) // CHUNK
        valid = (kc >= qc) & (kc <= qc + BAND_CHUNKS) & (kc >= BAND_CHUNKS - tile * (ATT_TILE // CHUNK))
        for h in range(A_HEADS):
            w_rows = jnp.broadcast_to(w_ref[h:h + 1, :], (ATT_TILE, w_ref.shape[1]))
            toeplitz = pltpu.roll(w_rows, 0, 1, stride=1, stride_axis=0)[:, :win]
            bias_ref[h // 2, (h % 2) * ATT_TILE:(h % 2 + 1) * ATT_TILE, :] = jnp.where(valid, toeplitz, NEG_INF)

    n_groups = D_MODEL // LANES

    @pl.when(tile == 0)
    def _():
        vext_ref[:, :, :, LANES:] = jnp.ones((3, n_groups, ATT_TILE, LANES), BF16)

    k_refs, v_refs = (ka_ref, kb_ref, kc_ref), (va_ref, vb_ref, vc_ref)
    for j, v_ref in enumerate(v_refs):
        for p in range(n_groups):
            vext_ref[j, p, :, :LANES] = v_ref[0, :, p * LANES:(p + 1) * LANES]

    lane = lax.broadcasted_iota(jnp.int32, (ATT_TILE, LANES), 1)
    low_half = lane < A_HEAD_DIM

    def scores(p):
        cols = slice(p * LANES, (p + 1) * LANES)
        qp = q_ref[0, :, cols]
        zero = jnp.zeros_like(qp)
        q_pair = jnp.concatenate([jnp.where(low_half, qp, zero), jnp.where(low_half, zero, qp)], axis=0)
        return [_dot_nt(q_pair, k_ref[0, :, cols]) + bias_ref[p, :, j * ATT_TILE:(j + 1) * ATT_TILE]
                for j, k_ref in enumerate(k_refs)]

    ahead = 2
    pending = [scores(p) for p in range(ahead)]
    for p in range(n_groups):
        s_parts = pending.pop(0)
        if p + ahead < n_groups:
            pending.append(scores(p + ahead))
        m = functools.reduce(jnp.maximum, [s.max(axis=-1, keepdims=True) for s in s_parts])
        pv = sum(_dot(jnp.exp(s - m).astype(BF16), vext_ref[j, p]) for j, s in enumerate(s_parts))
        pv = pv[:, :LANES] / pv[:, LANES:]
        o_ref[0, :, p * LANES:(p + 1) * LANES] = jnp.where(low_half, pv[:ATT_TILE], pv[ATT_TILE:]).astype(o_ref.dtype)


def _band_bias_vectors(rel_table):
    n_heads = rel_table.shape[1]
    win = 3 * ATT_TILE
    t = rel_table.T.astype(F32)
    u = jnp.concatenate([jnp.broadcast_to(t[:, 2 * REL_CLIP:], (n_heads, win - REL_CLIP)),
                         jnp.flip(t[:, 1:2 * REL_CLIP], axis=1),
                         jnp.broadcast_to(t[:, :1], (n_heads, REL_CLIP))], axis=1)
    assert u.shape[1] == ATT_TILE + win - 1
    return jnp.roll(jnp.pad(u, ((0, 0), (0, 1))), -(ATT_TILE - 1), axis=1)


def _attn_prompt(q, k, v, bias_vectors):
    bsz, t_len, d = q.shape
    assert t_len % ATT_TILE == 0 and BAND_PAST == 2 * ATT_TILE
    blk = (1, ATT_TILE, d)

    def past(n):
        return lambda b, i: (b, jnp.maximum(i - n, 0), 0)

    kv_specs = [pl.BlockSpec(blk, past(2)), pl.BlockSpec(blk, past(1)), pl.BlockSpec(blk, past(0))]
    return pl.pallas_call(
        _attn_prompt_kernel,
        grid=(bsz, t_len // ATT_TILE),
        in_specs=[pl.BlockSpec(blk, lambda b, i: (b, i, 0))] + kv_specs + kv_specs
                 + [pl.BlockSpec(bias_vectors.shape, lambda b, i: (0, 0))],
        out_specs=pl.BlockSpec(blk, lambda b, i: (b, i, 0)),
        out_shape=jax.ShapeDtypeStruct((bsz, t_len, d), BF16),
        scratch_shapes=[pltpu.VMEM((A_HEADS // 2, 2 * ATT_TILE, 3 * ATT_TILE), F32),
                        pltpu.VMEM((3, d // LANES, ATT_TILE, 2 * LANES), BF16)],
        compiler_params=_params("parallel", "arbitrary"),
        name="attn_prompt",
    )(q, k, k, k, v, v, v, bias_vectors)


def _attn_sample_kernel(q_ref, kn_ref, vn_ref, kc_ref, vc_ref, w_ref, o_ref):
    t_len = q_ref.shape[1]
    n_cache = kc_ref.shape[2]
    lane = lax.broadcasted_iota(jnp.int32, (t_len, LANES), 1)
    low_half = lane < A_HEAD_DIM
    bias = [pltpu.roll(jnp.broadcast_to(w_ref[h:h + 1, :], (t_len, w_ref.shape[1])), 0, 1, stride=1, stride_axis=0)
            for h in range(A_HEADS)]
    for p in range(D_MODEL // LANES):
        cols = slice(p * LANES, (p + 1) * LANES)
        qp = q_ref[0, :, cols].astype(BF16)
        kc_t = kc_ref[0, cols, :].astype(BF16)
        vc_t = vc_ref[0, cols, :].astype(BF16)
        kn = kn_ref[0, :, cols].astype(BF16)
        vn = vn_ref[0, :, cols].astype(BF16)
        zero = jnp.zeros_like(qp)
        q_pair = jnp.concatenate([jnp.where(low_half, qp, zero), jnp.where(low_half, zero, qp)], axis=0)
        b_pair = jnp.concatenate([bias[2 * p], bias[2 * p + 1]], axis=0)
        s_c = _dot(q_pair, kc_t) + b_pair[:, :n_cache]
        s_n = _dot_nt(q_pair, kn) + b_pair[:, n_cache:n_cache + t_len]
        m = jnp.maximum(s_c.max(axis=-1, keepdims=True), s_n.max(axis=-1, keepdims=True))
        e_c, e_n = jnp.exp(s_c - m), jnp.exp(s_n - m)
        l = e_c.sum(axis=-1, keepdims=True) + e_n.sum(axis=-1, keepdims=True)
        pv = (_dot_nt(e_c.astype(BF16), vc_t) + _dot(e_n.astype(BF16), vn)) / l
        o_ref[0, :, cols] = jnp.where(low_half, pv[:t_len], pv[t_len:]).astype(o_ref.dtype)


def _sample_bias_vectors(rel_table, n_cache, t_len):
    assert t_len - 1 <= REL_CLIP <= n_cache + t_len
    n_heads = rel_table.shape[1]
    t = rel_table.T.astype(F32)
    u = jnp.concatenate([jnp.broadcast_to(t[:, 2 * REL_CLIP:], (n_heads, n_cache + t_len - REL_CLIP)),
                         jnp.flip(t[:, REL_CLIP - t_len + 1:2 * REL_CLIP], axis=1)], axis=1)
    assert u.shape[1] == n_cache + 2 * t_len - 1
    width = -(-u.shape[1] // LANES) * LANES
    return jnp.roll(jnp.pad(u, ((0, 0), (0, width - u.shape[1]))), -(t_len - 1), axis=1)


def _attn_sample(q, k_new, v_new, cache_k, cache_v, bias_vectors):
    bsz, t_len, d = q.shape
    n_cache = cache_k.shape[2]
    new_spec = pl.BlockSpec((1, t_len, d), lambda b: (b, 0, 0))
    cache_spec = pl.BlockSpec((1, d, n_cache), lambda b: (b, 0, 0))
    return pl.pallas_call(
        _attn_sample_kernel,
        grid=(bsz,),
        in_specs=[new_spec, new_spec, new_spec, cache_spec, cache_spec,
                  pl.BlockSpec(bias_vectors.shape, lambda b: (0, 0))],
        out_specs=new_spec,
        out_shape=jax.ShapeDtypeStruct((bsz, t_len, d), BF16),
        compiler_params=_params("parallel"),
        name="attn_sample",
    )(q, k_new, v_new, cache_k, cache_v, bias_vectors)


def _log_sigmoid(z):
    return jnp.minimum(z, 0.0) - jnp.log(1.0 + jnp.exp(-jnp.abs(z)))


def _gla_project(x_ref, w_ref, wup_ref, bgk_ref, q_ref, k_ref, g_ref, v_ref, r_ref):
    n_seq, n_rows, d = x_ref.shape
    xb = x_ref[...].reshape(n_seq * n_rows, d).astype(BF16)
    n_chunk = 512
    proj = lambda lo, hi: _dot_nt(xb, w_ref[lo:hi, :].astype(BF16))
    per_seq = lambda y: y.reshape(n_seq, n_rows, y.shape[-1])
    q_ref[...] = per_seq(proj(0, B_QK) * (B_KEY_DIM ** -0.5))
    k_ref[...] = per_seq(proj(B_QK, 2 * B_QK))
    for c in range(0, B_VD, n_chunk):
        v_ref[:, :, c:c + n_chunk] = per_seq(proj(2 * B_QK + c, 2 * B_QK + c + n_chunk))
        r_ref[:, :, c:c + n_chunk] = per_seq(proj(2 * B_QK + B_VD + c, 2 * B_QK + B_VD + c + n_chunk))
    n_main = 2 * B_QK + 2 * B_VD
    low = proj(n_main, n_main + B_GATE_RANK)
    z = _dot(low.astype(BF16), wup_ref[...].astype(BF16)) + bgk_ref[...]
    g_ref[...] = per_seq(_log_sigmoid(z) / B_GATE_NORM)


def _split3(x):
    hi = x.astype(BF16)
    r1 = x - hi.astype(F32)
    mid = r1.astype(BF16)
    lo = (r1 - mid.astype(F32)).astype(BF16)
    return hi, mid, lo


def _gla_finish(o, h, rows, r_ref, gain_ref, a_ref):
    vcols = slice(h * B_VAL_DIM, (h + 1) * B_VAL_DIM)
    o = o * lax.rsqrt(jnp.mean(o * o, axis=-1, keepdims=True) + GN_EPS) * gain_ref[:, vcols]
    r_h = r_ref[0, rows, vcols]
    a_ref[0, rows, vcols] = (o * (r_h * jax.nn.sigmoid(r_h))).astype(a_ref.dtype)


def _gla_rows_as_one_chunk(rows, b, q_ref, k_ref, v_ref, r_ref, gain_ref, a_ref, st_ref):
    n = b.shape[0]
    q, k = q_ref[0, rows, :], k_ref[0, rows, :]
    b_last = b[n - 1:n, :]
    qhat = (q * jnp.exp(b)).astype(BF16)
    kinv = (k * jnp.exp(-b)).astype(BF16)
    kdec = (k * jnp.exp(b_last - b)).astype(BF16)
    decay = jnp.exp(b_last)
    causal = lax.broadcasted_iota(jnp.int32, (n, n), 1) <= lax.broadcasted_iota(jnp.int32, (n, n), 0)
    for h in range(B_HEADS):
        kcols = slice(h * B_KEY_DIM, (h + 1) * B_KEY_DIM)
        vcols = slice(h * B_VAL_DIM, (h + 1) * B_VAL_DIM)
        a_mat = jnp.where(causal, _dot_nt(qhat[:, kcols], kinv[:, kcols]), 0.0).astype(BF16)
        v_h = v_ref[0, rows, vcols].astype(BF16)
        st = st_ref[0, h]
        o = _dot(a_mat, v_h) + _dot_nt(qhat[:, kcols], st.astype(BF16))
        st_ref[0, h] = st * decay[:, kcols] + _dot_tn(v_h, kdec[:, kcols])
        _gla_finish(o, h, rows, r_ref, gain_ref, a_ref)


def _gla_kernel(x_ref, w_ref, wup_ref, bgk_ref, s0_ref, gain_ref, a_ref, st_ref,
                q_ref, k_ref, g_ref, v_ref, r_ref, st0_ref, *, chunk, n_chunks, wide_chunk):
    @pl.when(pl.program_id(1) == 0)
    def _():
        st_ref[...] = s0_ref[...]

    _gla_project(x_ref, w_ref, wup_ref, bgk_ref, q_ref, k_ref, g_ref, v_ref, r_ref)
    for s in range(x_ref.shape[0]):
        one = lambda ref: ref.at[pl.ds(s, 1)]
        _gla_sequence(one(q_ref), one(k_ref), one(g_ref), one(v_ref), one(r_ref), gain_ref, one(a_ref),
                      one(st_ref), one(st0_ref), chunk=chunk, n_chunks=n_chunks, wide_chunk=wide_chunk)


def _gla_sequence(q_ref, k_ref, g_ref, v_ref, r_ref, gain_ref, a_ref, st_ref, st0_ref, *, chunk, n_chunks,
                  wide_chunk):
    n = chunk * n_chunks

    def by_sub_blocks():
        _gla_chunks_by_sub_blocks(q_ref, k_ref, g_ref, v_ref, r_ref, gain_ref, a_ref, st_ref,
                                  chunk=chunk, n_chunks=n_chunks)

    if not wide_chunk:
        by_sub_blocks()
        return
    st0_ref[...] = st_ref[...]
    tri = (lax.broadcasted_iota(jnp.int32, (wide_chunk, wide_chunk), 1)
           <= lax.broadcasted_iota(jnp.int32, (wide_chunk, wide_chunk), 0))
    tri = jnp.where(tri, 1.0, 0.0).astype(BF16)
    min_log_decay = None
    for r0 in range(0, n, wide_chunk):
        rows = slice(r0, r0 + wide_chunk)
        b = sum(_dot(tri, gp) for gp in _split3(g_ref[0, rows, :]))
        _gla_rows_as_one_chunk(rows, b, q_ref, k_ref, v_ref, r_ref, gain_ref, a_ref, st_ref)
        total = b[wide_chunk - 1:wide_chunk, :]
        min_log_decay = total if min_log_decay is None else jnp.minimum(min_log_decay, total)

    @pl.when(jnp.logical_not(jnp.min(min_log_decay) >= -GLA_MAX_LOG_DECAY))
    def _():
        st_ref[...] = st0_ref[...]
        by_sub_blocks()


def _gla_chunks_by_sub_blocks(q_ref, k_ref, g_ref, v_ref, r_ref, gain_ref, a_ref, st_ref, *, chunk, n_chunks):
    nsb = chunk // SUB_BLOCK
    row = lax.broadcasted_iota(jnp.int32, (chunk, chunk), 0)
    col = lax.broadcasted_iota(jnp.int32, (chunk, chunk), 1)
    blk_start = (row // SUB_BLOCK) * SUB_BLOCK
    tri_local = jnp.where((col <= row) & (col >= blk_start), 1.0, 0.0).astype(BF16)
    tri_before = jnp.where(col < blk_start, 1.0, 0.0).astype(BF16)
    rcol = lax.broadcasted_iota(jnp.int32, (SUB_BLOCK, chunk), 1)
    rrow = lax.broadcasted_iota(jnp.int32, (SUB_BLOCK, chunk), 0)

    def chunk_body(c, carry):
        row0 = pl.multiple_of(c * chunk, chunk)
        rows = pl.ds(row0, chunk)
        q = q_ref[0, rows, :]
        k = k_ref[0, rows, :]
        g_parts = _split3(g_ref[0, rows, :])
        bl = sum(_dot(tri_local, gp) for gp in g_parts)
        if nsb > 1:
            rr = sum(_dot(tri_before, gp) for gp in g_parts)
            b = bl + rr
        else:
            rr = None
            b = bl
        b_last = b[chunk - 1:chunk, :]
        qt = q * jnp.exp(bl)
        qhat = qt * jnp.exp(rr) if nsb > 1 else qt
        kdec = (k * jnp.exp(b_last - b)).astype(BF16)
        decay = jnp.exp(b_last)
        kb = k.astype(BF16)

        for h in range(B_HEADS):
            kcols = slice(h * B_KEY_DIM, (h + 1) * B_KEY_DIM)
            vcols = slice(h * B_VAL_DIM, (h + 1) * B_VAL_DIM)
            q_h, k_h, bl_h, b_h, qt_h = q[:, kcols], k[:, kcols], bl[:, kcols], b[:, kcols], qt[:, kcols]
            kb_h = kb[:, kcols]
            a_rows = []
            for i in range(nsb):
                sb = slice(i * SUB_BLOCK, (i + 1) * SUB_BLOCK)
                q_s, bl_s = q_h[sb], bl_h[sb]
                y = jnp.concatenate(
                    [q_s * jnp.exp(jnp.minimum(bl_s - bl_s[j:j + 1], 0.0)) for j in range(SUB_BLOCK)], axis=0)
                res = _dot_nt(y.astype(BF16), kb_h)
                a_i = jnp.zeros((SUB_BLOCK, chunk), F32)
                for j in range(SUB_BLOCK):
                    a_i = jnp.where(rcol == i * SUB_BLOCK + j, res[j * SUB_BLOCK:(j + 1) * SUB_BLOCK], a_i)
                a_i = jnp.where(rcol <= i * SUB_BLOCK + rrow, a_i, 0.0)
                if i > 0:
                    r_i = rr[i * SUB_BLOCK:i * SUB_BLOCK + 1, kcols]
                    kt = (k_h * jnp.exp(jnp.minimum(r_i - b_h, 0.0))).astype(BF16)
                    a_off = _dot_nt(qt_h[sb].astype(BF16), kt)
                    a_i = jnp.where(rcol < i * SUB_BLOCK, a_off, a_i)
                a_rows.append(a_i)
            a_mat = (jnp.concatenate(a_rows, axis=0) if nsb > 1 else a_rows[0]).astype(BF16)
            v_h = v_ref[0, rows, vcols].astype(BF16)
            st = st_ref[0, h]
            o = _dot(a_mat, v_h) + _dot_nt(qhat[:, kcols].astype(BF16), st.astype(BF16))
            st_ref[0, h] = st * decay[:, kcols] + _dot_tn(v_h, kdec[:, kcols])
            _gla_finish(o, h, rows, r_ref, gain_ref, a_ref)
        return carry

    lax.fori_loop(0, n_chunks, chunk_body, 0)


def _gla(x, w_in, w_up, b_gk, s0t, gain, chunk, chunks_per_step, wide_chunk, seqs_per_step=1, name="gla"):
    bsz, t_len, d = x.shape
    rows, n_seq = chunk * chunks_per_step, seqs_per_step
    assert t_len % rows == 0 and bsz % n_seq == 0
    const = lambda b, s: (0, 0)
    resident = lambda w: pl.BlockSpec(w.shape, const, pipeline_mode=pl.Buffered(1))
    st_shape = (n_seq, B_HEADS, B_VAL_DIM, B_KEY_DIM)
    st_spec = pl.BlockSpec(st_shape, lambda b, s: (b, 0, 0, 0))
    out_spec = pl.BlockSpec((n_seq, rows, B_VD), lambda b, s: (b, s, 0))
    return pl.pallas_call(
        functools.partial(_gla_kernel, chunk=chunk, n_chunks=chunks_per_step, wide_chunk=wide_chunk),
        grid=(bsz // n_seq, t_len // rows),
        in_specs=[pl.BlockSpec((n_seq, rows, d), lambda b, s: (b, s, 0)),
                  resident(w_in), resident(w_up), pl.BlockSpec((1, B_QK), const),
                  st_spec, pl.BlockSpec((1, B_VD), const)],
        out_specs=[out_spec, st_spec],
        out_shape=[jax.ShapeDtypeStruct((bsz, t_len, B_VD), BF16),
                   jax.ShapeDtypeStruct(s0t.shape, F32)],
        scratch_shapes=[pltpu.VMEM((n_seq, rows, B_QK), F32)] * 3 + [pltpu.VMEM((n_seq, rows, B_VD), F32)] * 2
                       + [pltpu.VMEM(st_shape, F32)],
        compiler_params=_params("parallel", "arbitrary"),
        name=name,
    )(x, w_in, w_up, b_gk.reshape(1, B_QK), s0t, gain.reshape(1, B_VD))


def kernel(x_prompt, x_sample, cache_a_k, cache_a_v, state_b, w_in_a, rel_bias_a, w_out_a, w_in_b, w_gk_up_b,
           b_gk_b, gn_gain_b, w_out_b, w_ffn_in, w_ffn_out, ln1_g, ln1_b, ln2_g, ln2_b):
    bsz, t_len, d = x_prompt.shape
    dbsz, dt_len, _ = x_sample.shape
    n_cache = cache_a_k.shape[2]
    keep = min(BAND_PAST, t_len)
    xp = x_prompt.reshape(bsz * t_len, d)
    xs = x_sample.reshape(dbsz * dt_len, d)

    w_oa = w_out_a[0].astype(BF16)
    bias_p = _band_bias_vectors(rel_bias_a[0])
    bias_s = _sample_bias_vectors(rel_bias_a[0], n_cache, dt_len)

    q, k, v, k_tail, v_tail = _qkv(xp, w_in_a[0], BF16, t_len, tail_rows=keep, name="qkv_prompt")
    att = _attn_prompt(q.reshape(bsz, t_len, d), k.reshape(bsz, t_len, d), v.reshape(bsz, t_len, d), bias_p)

    qs, ks, vs = _qkv(xs, w_in_a[0], F32, dt_len, name="qkv_sample")
    heads_first = lambda a: jnp.transpose(a, (0, 2, 3, 1)).reshape(a.shape[0], d, a.shape[1])
    att_s = _attn_sample(qs.reshape(dbsz, dt_len, d), ks.reshape(dbsz, dt_len, d), vs.reshape(dbsz, dt_len, d),
                         heads_first(cache_a_k[0]), heads_first(cache_a_v[0]), bias_s)

    wi, wo = w_ffn_in.astype(BF16), w_ffn_out.astype(BF16)
    ln1, ln2 = (ln1_g[0], ln1_b[0]), (ln2_g[0], ln2_b[0])
    xp = _layer_tail(att.reshape(bsz * t_len, d), w_oa, xp, ln1, wi, wo, 0, ln2, name="tail0_prompt")
    xs = _layer_tail(att_s.reshape(dbsz * dt_len, d), w_oa, xs, ln1, wi, wo, 0, ln2, tm=128, name="tail0_sample")

    w_ob = w_out_b[0].astype(BF16)
    w_in_t = w_in_b[0].T

    s0 = jnp.zeros((bsz, B_HEADS, B_VAL_DIM, B_KEY_DIM), F32)
    a_p, st_p = _gla(xp.reshape(bsz, t_len, d), w_in_t, w_gk_up_b[0], b_gk_b[0], s0, gn_gain_b[0],
                     CHUNK, min(GLA_STEP_ROWS, t_len) // CHUNK, GLA_WIDE_CHUNK, name="gla_prompt")
    a_s, st_s = _gla(xs.reshape(dbsz, dt_len, d), w_in_t, w_gk_up_b[0], b_gk_b[0],
                     jnp.swapaxes(state_b[0], -1, -2), gn_gain_b[0], dt_len, 1, None, seqs_per_step=dbsz,
                     name="gla_sample")

    ln1, ln2 = (ln1_g[1], ln1_b[1]), (ln2_g[1], ln2_b[1])
    xp = _layer_tail(a_p.reshape(bsz * t_len, B_VD), w_ob, xp, ln1, wi, wo, 1, ln2, name="tail1_prompt")
    xs = _layer_tail(a_s.reshape(dbsz * dt_len, B_VD), w_ob, xs, ln1, wi, wo, 1, ln2, tm=128, name="tail1_sample")

    heads = lambda a, n, t: a.reshape(1, n, t, A_HEADS, A_HEAD_DIM)
    seq_first = lambda a: jnp.transpose(a.reshape(a.shape[0], A_HEADS, A_HEAD_DIM, a.shape[2]), (0, 3, 1, 2))[None]
    return (xp.reshape(bsz, t_len, d), xs.reshape(dbsz, dt_len, d),
            seq_first(k_tail), seq_first(v_tail), jnp.swapaxes(st_p, -1, -2)[None],
            heads(ks, dbsz, dt_len), heads(vs, dbsz, dt_len), jnp.swapaxes(st_s, -1, -2)[None])
```

```python
import functools

import jax
import jax.numpy as jnp
from jax import lax
from jax.experimental import pallas as pl
from jax.experimental.pallas import tpu as pltpu

F32 = jnp.float32
BF16 = jnp.bfloat16

D_MODEL = 1024
CHUNK = 64
A_HEADS = 16
A_HEAD_DIM = D_MODEL // A_HEADS
BAND_CHUNKS = 8
BAND_PAST = BAND_CHUNKS * CHUNK
REL_CLIP = 128
B_HEADS = 4
B_KEY_DIM = D_MODEL // 2 // B_HEADS
B_VAL_DIM = D_MODEL // B_HEADS
B_QK = B_HEADS * B_KEY_DIM
B_VD = B_HEADS * B_VAL_DIM
B_GATE_RANK = 16
B_GATE_NORM = 16.0
D_FF = -(-8 * D_MODEL // 768) * 256
DEPTH = 2
ALPHA = (2.0 * DEPTH) ** 0.25
LN_EPS = 1e-5
GN_EPS = 1e-6
NEG_INF = -1e30

LANES = 128
SUB_BLOCK = 16
ATT_TILE = 4 * CHUNK
GLA_STEP_ROWS = 1024
GLA_WIDE_CHUNK = 256
GLA_MAX_LOG_DECAY = 60.0
V7X_VMEM_BYTES = 64 * 1024 * 1024
VMEM_LIMIT = V7X_VMEM_BYTES * 7 // 8


def _dot(a, b):
    return jnp.dot(a, b, preferred_element_type=F32)


def _dot_nt(a, b):
    return lax.dot_general(a, b, (((1,), (1,)), ((), ())), preferred_element_type=F32)


def _dot_tn(a, b):
    return lax.dot_general(a, b, (((0,), (0,)), ((), ())), preferred_element_type=F32)


def _params(*sem):
    return pltpu.CompilerParams(dimension_semantics=sem, vmem_limit_bytes=VMEM_LIMIT)


def _row_tile(m, pref):
    t = min(m, pref)
    assert m % t == 0
    return t


def _qkv_kernel(x_ref, w_ref, q_ref, k_ref, v_ref, *tail_refs, n_chunk, tiles_per_seq):
    xb = x_ref[...].astype(BF16)
    tm, d = q_ref.shape
    if tail_refs:
        *tail_refs, stash_ref = tail_refs
    for idx, o_ref in enumerate((q_ref, k_ref, v_ref)):
        for c in range(0, d, n_chunk):
            y = _dot(xb, w_ref[:, idx * d + c:idx * d + c + n_chunk].astype(BF16))
            if idx == 0:
                y = y * (A_HEAD_DIM ** -0.5)
            o_ref[:, c:c + n_chunk] = y.astype(o_ref.dtype)
            if tail_refs and idx > 0:
                stash_ref[idx - 1, :, c:c + n_chunk] = y[tm - stash_ref.shape[1]:, :]
    if not tail_refs:
        return

    @pl.when(pl.program_id(0) % tiles_per_seq == tiles_per_seq - 1)
    def _():
        for idx, tail_ref in enumerate(tail_refs):
            for c in range(0, d, n_chunk):
                tail_ref[0, c:c + n_chunk, :] = stash_ref[idx, :, c:c + n_chunk].T


def _qkv(x, w, dtype, t_len, tail_rows=0, tm=1024, name="qkv"):
    m, d = x.shape
    tm = _row_tile(t_len if tail_rows else m, tm)
    n_chunk = 512
    assert d % n_chunk == 0 and w.shape == (d, 3 * d) and tail_rows <= tm
    tiles_per_seq = t_len // tm
    rows = pl.BlockSpec((tm, d), lambda i: (i, 0))
    out_specs, out_shape = [rows] * 3, [jax.ShapeDtypeStruct((m, d), dtype)] * 3
    if tail_rows:
        out_specs += [pl.BlockSpec((1, d, tail_rows), lambda i: (i // tiles_per_seq, 0, 0))] * 2
        out_shape += [jax.ShapeDtypeStruct((m // t_len, d, tail_rows), F32)] * 2
    return pl.pallas_call(
        functools.partial(_qkv_kernel, n_chunk=n_chunk, tiles_per_seq=tiles_per_seq),
        grid=(m // tm,),
        in_specs=[rows, pl.BlockSpec(w.shape, lambda i: (0, 0), pipeline_mode=pl.Buffered(1))],
        out_specs=out_specs,
        out_shape=out_shape,
        scratch_shapes=[pltpu.VMEM((2, tail_rows, d), F32)] if tail_rows else [],
        compiler_params=_params("arbitrary"),
        name=name,
    )(x, w)


def _res_ln(x, y, g, b):
    t = ALPHA * x + y
    mu = jnp.mean(t, axis=-1, keepdims=True)
    d = t - mu
    var = jnp.mean(d * d, axis=-1, keepdims=True)
    return d * lax.rsqrt(var + LN_EPS) * g + b


def _layer_tail_kernel(a_ref, wp_ref, x_ref, g1_ref, b1_ref, wi_ref, wo_ref, g2_ref, b2_ref, o_ref, *h_refs,
                       ff_chunk):
    n = len(h_refs)
    tm = x_ref.shape[0]
    d_ff = h_refs[0].shape[1]
    rows = [slice(i * (tm // n), (i + 1) * (tm // n)) for i in range(n)]
    x1, y = [None] * n, [None] * n

    def ln1(i):
        x1[i] = _res_ln(x_ref[rows[i], :], _dot(a_ref[rows[i], :], wp_ref[...]), g1_ref[...], b1_ref[...])

    def ffn(i):
        xb = x1[i].astype(BF16)
        for c in range(0, d_ff, ff_chunk):
            gate = _dot(xb, wi_ref[:, c:c + ff_chunk])
            up = _dot(xb, wi_ref[:, d_ff + c:d_ff + c + ff_chunk])
            h_refs[i][:, c:c + ff_chunk] = (gate * jax.nn.sigmoid(gate) * up).astype(BF16)
        y[i] = _dot(h_refs[i][...], wo_ref[...])

    def ln2(i):
        o_ref[rows[i], :] = _res_ln(x1[i], y[i], g2_ref[...], b2_ref[...])

    ln1(0)
    for i in range(n):
        if i + 1 < n:
            ln1(i + 1)
        ffn(i)
        if i > 0:
            ln2(i - 1)
    ln2(n - 1)


def _layer_tail(a, wp, x, ln1, wi, wo, layer, ln2, tm=1024, name="layer_tail"):
    m, k = a.shape
    d = x.shape[1]
    d_ff = wo.shape[1]
    tm = _row_tile(m, tm)
    ff_chunk = 256
    row_groups = 4 if tm % 1024 == 0 else 1
    assert d_ff % ff_chunk == 0
    rows = lambda width: pl.BlockSpec((tm, width), lambda i: (i, 0))
    resident = lambda w: pl.BlockSpec(w.shape, lambda i: (0, 0), pipeline_mode=pl.Buffered(1))
    of_layer = lambda w: pl.BlockSpec((None,) + w.shape[1:], lambda i: (layer, 0, 0), pipeline_mode=pl.Buffered(1))
    vec = pl.BlockSpec((1, d), lambda i: (0, 0))
    return pl.pallas_call(
        functools.partial(_layer_tail_kernel, ff_chunk=ff_chunk),
        grid=(m // tm,),
        in_specs=[rows(k), resident(wp), rows(d), vec, vec, of_layer(wi), of_layer(wo), vec, vec],
        out_specs=rows(d),
        out_shape=jax.ShapeDtypeStruct((m, d), F32),
        scratch_shapes=[pltpu.VMEM((tm // row_groups, d_ff), BF16)] * row_groups,
        compiler_params=_params("parallel"),
        name=name,
    )(a, wp, x, ln1[0].reshape(1, d), ln1[1].reshape(1, d), wi, wo, ln2[0].reshape(1, d), ln2[1].reshape(1, d))


def _attn_prompt_kernel(q_ref, ka_ref, kb_ref, kc_ref, va_ref, vb_ref, vc_ref, w_ref, o_ref,
                        bias_ref, vext_ref):
    tile = pl.program_id(1)
    win = 3 * ATT_TILE

    @pl.when(tile <= 2)
    def _():
        qc = lax.broadcasted_iota(jnp.int32, (ATT_TILE, win), 0) // CHUNK
        kc = lax.broadcasted_iota(jnp.int32, (ATT_TILE, win), 1) // CHUNK
        valid = (kc >= qc) & (kc <= qc + BAND_CHUNKS) & (kc >= BAND_CHUNKS - tile * (ATT_TILE // CHUNK))
        for h in range(A_HEADS):
            w_rows = jnp.broadcast_to(w_ref[h:h + 1, :], (ATT_TILE, w_ref.shape[1]))
            toeplitz = pltpu.roll(w_rows, 0, 1, stride=1, stride_axis=0)[:, :win]
            bias_ref[h // 2, (h % 2) * ATT_TILE:(h % 2 + 1) * ATT_TILE, :] = jnp.where(valid, toeplitz, NEG_INF)

    n_groups = D_MODEL // LANES

    @pl.when(tile == 0)
    def _():
        vext_ref[:, :, :, LANES:] = jnp.ones((3, n_groups, ATT_TILE, LANES), BF16)

    k_refs, v_refs = (ka_ref, kb_ref, kc_ref), (va_ref, vb_ref, vc_ref)
    for j, v_ref in enumerate(v_refs):
        for p in range(n_groups):
            vext_ref[j, p, :, :LANES] = v_ref[0, :, p * LANES:(p + 1) * LANES]

    lane = lax.broadcasted_iota(jnp.int32, (ATT_TILE, LANES), 1)
    low_half = lane < A_HEAD_DIM

    def scores(p):
        cols = slice(p * LANES, (p + 1) * LANES)
        qp = q_ref[0, :, cols]
        zero = jnp.zeros_like(qp)
        q_pair = jnp.concatenate([jnp.where(low_half, qp, zero), jnp.where(low_half, zero, qp)], axis=0)
        return [_dot_nt(q_pair, k_ref[0, :, cols]) + bias_ref[p, :, j * ATT_TILE:(j + 1) * ATT_TILE]
                for j, k_ref in enumerate(k_refs)]

    ahead = 2
    pending = [scores(p) for p in range(ahead)]
    for p in range(n_groups):
        s_parts = pending.pop(0)
        if p + ahead < n_groups:
            pending.append(scores(p + ahead))
        m = functools.reduce(jnp.maximum, [s.max(axis=-1, keepdims=True) for s in s_parts])
        pv = sum(_dot(jnp.exp(s - m).astype(BF16), vext_ref[j, p]) for j, s in enumerate(s_parts))
        pv = pv[:, :LANES] / pv[:, LANES:]
        o_ref[0, :, p * LANES:(p + 1) * LANES] = jnp.where(low_half, pv[:ATT_TILE], pv[ATT_TILE:]).astype(o_ref.dtype)


def _band_bias_vectors(rel_table):
    n_heads = rel_table.shape[1]
    win = 3 * ATT_TILE
    t = rel_table.T.astype(F32)
    u = jnp.concatenate([jnp.broadcast_to(t[:, 2 * REL_CLIP:], (n_heads, win - REL_CLIP)),
                         jnp.flip(t[:, 1:2 * REL_CLIP], axis=1),
                         jnp.broadcast_to(t[:, :1], (n_heads, REL_CLIP))], axis=1)
    assert u.shape[1] == ATT_TILE + win - 1
    return jnp.roll(jnp.pad(u, ((0, 0), (0, 1))), -(ATT_TILE - 1), axis=1)


def _attn_prompt(q, k, v, bias_vectors):
    bsz, t_len, d = q.shape
    assert t_len % ATT_TILE == 0 and BAND_PAST == 2 * ATT_TILE
    blk = (1, ATT_TILE, d)

    def past(n):
        return lambda b, i: (b, jnp.maximum(i - n, 0), 0)

    kv_specs = [pl.BlockSpec(blk, past(2)), pl.BlockSpec(blk, past(1)), pl.BlockSpec(blk, past(0))]
    return pl.pallas_call(
        _attn_prompt_kernel,
        grid=(bsz, t_len // ATT_TILE),
        in_specs=[pl.BlockSpec(blk, lambda b, i: (b, i, 0))] + kv_specs + kv_specs
                 + [pl.BlockSpec(bias_vectors.shape, lambda b, i: (0, 0))],
        out_specs=pl.BlockSpec(blk, lambda b, i: (b, i, 0)),
        out_shape=jax.ShapeDtypeStruct((bsz, t_len, d), BF16),
        scratch_shapes=[pltpu.VMEM((A_HEADS // 2, 2 * ATT_TILE, 3 * ATT_TILE), F32),
                        pltpu.VMEM((3, d // LANES, ATT_TILE, 2 * LANES), BF16)],
        compiler_params=_params("parallel", "arbitrary"),
        name="attn_prompt",
    )(q, k, k, k, v, v, v, bias_vectors)


def _attn_sample_kernel(q_ref, kn_ref, vn_ref, kc_ref, vc_ref, w_ref, o_ref):
    t_len = q_ref.shape[1]
    n_cache = kc_ref.shape[2]
    lane = lax.broadcasted_iota(jnp.int32, (t_len, LANES), 1)
    low_half = lane < A_HEAD_DIM
    bias = [pltpu.roll(jnp.broadcast_to(w_ref[h:h + 1, :], (t_len, w_ref.shape[1])), 0, 1, stride=1, stride_axis=0)
            for h in range(A_HEADS)]
    groups = [slice(p * LANES, (p + 1) * LANES) for p in range(D_MODEL // LANES)]

    def scores(p, cols):
        qp = q_ref[0, :, cols].astype(BF16)
        zero = jnp.zeros_like(qp)
        q_pair = jnp.concatenate([jnp.where(low_half, qp, zero), jnp.where(low_half, zero, qp)], axis=0)
        b_pair = jnp.concatenate([bias[2 * p], bias[2 * p + 1]], axis=0)
        s_c = _dot(q_pair, kc_ref[0, cols, :].astype(BF16)) + b_pair[:, :n_cache]
        s_n = _dot_nt(q_pair, kn_ref[0, :, cols].astype(BF16)) + b_pair[:, n_cache:n_cache + t_len]
        return s_c, s_n

    def numerators(s_c, s_n):
        m = jnp.maximum(s_c.max(axis=-1, keepdims=True), s_n.max(axis=-1, keepdims=True))
        e_c, e_n = jnp.exp(s_c - m), jnp.exp(s_n - m)
        return e_c, e_n, e_c.sum(axis=-1, keepdims=True) + e_n.sum(axis=-1, keepdims=True)

    all_scores = [scores(p, cols) for p, cols in enumerate(groups)]
    all_numerators = [numerators(s_c, s_n) for s_c, s_n in all_scores]
    for cols, (e_c, e_n, l) in zip(groups, all_numerators):
        pv = _dot_nt(e_c.astype(BF16), vc_ref[0, cols, :].astype(BF16))
        pv = (pv + _dot(e_n.astype(BF16), vn_ref[0, :, cols].astype(BF16))) / l
        o_ref[0, :, cols] = jnp.where(low_half, pv[:t_len], pv[t_len:]).astype(o_ref.dtype)


def _sample_bias_vectors(rel_table, n_cache, t_len):
    assert t_len - 1 <= REL_CLIP <= n_cache + t_len
    n_heads = rel_table.shape[1]
    t = rel_table.T.astype(F32)
    u = jnp.concatenate([jnp.broadcast_to(t[:, 2 * REL_CLIP:], (n_heads, n_cache + t_len - REL_CLIP)),
                         jnp.flip(t[:, REL_CLIP - t_len + 1:2 * REL_CLIP], axis=1)], axis=1)
    assert u.shape[1] == n_cache + 2 * t_len - 1
    width = -(-u.shape[1] // LANES) * LANES
    return jnp.roll(jnp.pad(u, ((0, 0), (0, width - u.shape[1]))), -(t_len - 1), axis=1)


def _attn_sample(q, k_new, v_new, cache_k, cache_v, bias_vectors):
    bsz, t_len, d = q.shape
    n_cache = cache_k.shape[2]
    new_spec = pl.BlockSpec((1, t_len, d), lambda b: (b, 0, 0))
    cache_spec = pl.BlockSpec((1, d, n_cache), lambda b: (b, 0, 0))
    return pl.pallas_call(
        _attn_sample_kernel,
        grid=(bsz,),
        in_specs=[new_spec, new_spec, new_spec, cache_spec, cache_spec,
                  pl.BlockSpec(bias_vectors.shape, lambda b: (0, 0))],
        out_specs=new_spec,
        out_shape=jax.ShapeDtypeStruct((bsz, t_len, d), BF16),
        compiler_params=_params("parallel"),
        name="attn_sample",
    )(q, k_new, v_new, cache_k, cache_v, bias_vectors)


def _log_sigmoid(z):
    return jnp.minimum(z, 0.0) - jnp.log(1.0 + jnp.exp(-jnp.abs(z)))


def _gla_project(x_ref, w_ref, wup_ref, bgk_ref, q_ref, k_ref, g_ref, v_ref, r_ref):
    n_seq, n_rows, d = x_ref.shape
    xb = x_ref[...].reshape(n_seq * n_rows, d).astype(BF16)
    n_chunk = 512
    proj = lambda lo, hi: _dot_nt(xb, w_ref[lo:hi, :].astype(BF16))
    per_seq = lambda y: y.reshape(n_seq, n_rows, y.shape[-1])
    q_ref[...] = per_seq(proj(0, B_QK) * (B_KEY_DIM ** -0.5))
    k_ref[...] = per_seq(proj(B_QK, 2 * B_QK))
    for c in range(0, B_VD, n_chunk):
        v_ref[:, :, c:c + n_chunk] = per_seq(proj(2 * B_QK + c, 2 * B_QK + c + n_chunk))
        r_ref[:, :, c:c + n_chunk] = per_seq(proj(2 * B_QK + B_VD + c, 2 * B_QK + B_VD + c + n_chunk))
    n_main = 2 * B_QK + 2 * B_VD
    low = proj(n_main, n_main + B_GATE_RANK)
    z = _dot(low.astype(BF16), wup_ref[...].astype(BF16)) + bgk_ref[...]
    g_ref[...] = per_seq(_log_sigmoid(z) / B_GATE_NORM)


def _split3(x):
    hi = x.astype(BF16)
    r1 = x - hi.astype(F32)
    mid = r1.astype(BF16)
    lo = (r1 - mid.astype(F32)).astype(BF16)
    return hi, mid, lo


def _gla_finish(o, h, rows, r_ref, gain_ref, a_ref):
    vcols = slice(h * B_VAL_DIM, (h + 1) * B_VAL_DIM)
    o = o * lax.rsqrt(jnp.mean(o * o, axis=-1, keepdims=True) + GN_EPS) * gain_ref[:, vcols]
    r_h = r_ref[0, rows, vcols]
    a_ref[0, rows, vcols] = (o * (r_h * jax.nn.sigmoid(r_h))).astype(a_ref.dtype)


def _gla_rows_as_one_chunk(rows, b, q_ref, k_ref, v_ref, r_ref, gain_ref, a_ref, st_ref):
    n = b.shape[0]
    q, k = q_ref[0, rows, :], k_ref[0, rows, :]
    b_last = b[n - 1:n, :]
    qhat = (q * jnp.exp(b)).astype(BF16)
    kinv = (k * jnp.exp(-b)).astype(BF16)
    kdec = (k * jnp.exp(b_last - b)).astype(BF16)
    decay = jnp.exp(b_last)
    causal = lax.broadcasted_iota(jnp.int32, (n, n), 1) <= lax.broadcasted_iota(jnp.int32, (n, n), 0)
    for h in range(B_HEADS):
        kcols = slice(h * B_KEY_DIM, (h + 1) * B_KEY_DIM)
        vcols = slice(h * B_VAL_DIM, (h + 1) * B_VAL_DIM)
        a_mat = jnp.where(causal, _dot_nt(qhat[:, kcols], kinv[:, kcols]), 0.0).astype(BF16)
        v_h = v_ref[0, rows, vcols].astype(BF16)
        st = st_ref[0, h]
        o = _dot(a_mat, v_h) + _dot_nt(qhat[:, kcols], st.astype(BF16))
        st_ref[0, h] = st * decay[:, kcols] + _dot_tn(v_h, kdec[:, kcols])
        _gla_finish(o, h, rows, r_ref, gain_ref, a_ref)


def _gla_kernel(x_ref, w_ref, wup_ref, bgk_ref, s0_ref, gain_ref, a_ref, st_ref,
                q_ref, k_ref, g_ref, v_ref, r_ref, st0_ref, *, chunk, n_chunks, wide_chunk):
    @pl.when(pl.program_id(1) == 0)
    def _():
        st_ref[...] = s0_ref[...]

    _gla_project(x_ref, w_ref, wup_ref, bgk_ref, q_ref, k_ref, g_ref, v_ref, r_ref)
    for s in range(x_ref.shape[0]):
        one = lambda ref: ref.at[pl.ds(s, 1)]
        _gla_sequence(one(q_ref), one(k_ref), one(g_ref), one(v_ref), one(r_ref), gain_ref, one(a_ref),
                      one(st_ref), one(st0_ref), chunk=chunk, n_chunks=n_chunks, wide_chunk=wide_chunk)


def _gla_sequence(q_ref, k_ref, g_ref, v_ref, r_ref, gain_ref, a_ref, st_ref, st0_ref, *, chunk, n_chunks,
                  wide_chunk):
    n = chunk * n_chunks

    def by_sub_blocks():
        _gla_chunks_by_sub_blocks(q_ref, k_ref, g_ref, v_ref, r_ref, gain_ref, a_ref, st_ref,
                                  chunk=chunk, n_chunks=n_chunks)

    if not wide_chunk:
        by_sub_blocks()
        return
    st0_ref[...] = st_ref[...]
    tri = (lax.broadcasted_iota(jnp.int32, (wide_chunk, wide_chunk), 1)
           <= lax.broadcasted_iota(jnp.int32, (wide_chunk, wide_chunk), 0))
    tri = jnp.where(tri, 1.0, 0.0).astype(BF16)
    min_log_decay = None
    for r0 in range(0, n, wide_chunk):
        rows = slice(r0, r0 + wide_chunk)
        b = sum(_dot(tri, gp) for gp in _split3(g_ref[0, rows, :]))
        _gla_rows_as_one_chunk(rows, b, q_ref, k_ref, v_ref, r_ref, gain_ref, a_ref, st_ref)
        total = b[wide_chunk - 1:wide_chunk, :]
        min_log_decay = total if min_log_decay is None else jnp.minimum(min_log_decay, total)

    @pl.when(jnp.logical_not(jnp.min(min_log_decay) >= -GLA_MAX_LOG_DECAY))
    def _():
        st_ref[...] = st0_ref[...]
        by_sub_blocks()


def _gla_chunks_by_sub_blocks(q_ref, k_ref, g_ref, v_ref, r_ref, gain_ref, a_ref, st_ref, *, chunk, n_chunks):
    nsb = chunk // SUB_BLOCK
    row = lax.broadcasted_iota(jnp.int32, (chunk, chunk), 0)
    col = lax.broadcasted_iota(jnp.int32, (chunk, chunk), 1)
    blk_start = (row // SUB_BLOCK) * SUB_BLOCK
    tri_local = jnp.where((col <= row) & (col >= blk_start), 1.0, 0.0).astype(BF16)
    tri_before = jnp.where(col < blk_start, 1.0, 0.0).astype(BF16)
    rcol = lax.broadcasted_iota(jnp.int32, (SUB_BLOCK, chunk), 1)
    rrow = lax.broadcasted_iota(jnp.int32, (SUB_BLOCK, chunk), 0)

    def chunk_body(c, carry):
        row0 = pl.multiple_of(c * chunk, chunk)
        rows = pl.ds(row0, chunk)
        q = q_ref[0, rows, :]
        k = k_ref[0, rows, :]
        g_parts = _split3(g_ref[0, rows, :])
        bl = sum(_dot(tri_local, gp) for gp in g_parts)
        if nsb > 1:
            rr = sum(_dot(tri_before, gp) for gp in g_parts)
            b = bl + rr
        else:
            rr = None
            b = bl
        b_last = b[chunk - 1:chunk, :]
        qt = q * jnp.exp(bl)
        qhat = qt * jnp.exp(rr) if nsb > 1 else qt
        kdec = (k * jnp.exp(b_last - b)).astype(BF16)
        decay = jnp.exp(b_last)
        kb = k.astype(BF16)

        for h in range(B_HEADS):
            kcols = slice(h * B_KEY_DIM, (h + 1) * B_KEY_DIM)
            vcols = slice(h * B_VAL_DIM, (h + 1) * B_VAL_DIM)
            q_h, k_h, bl_h, b_h, qt_h = q[:, kcols], k[:, kcols], bl[:, kcols], b[:, kcols], qt[:, kcols]
            kb_h = kb[:, kcols]
            a_rows = []
            for i in range(nsb):
                sb = slice(i * SUB_BLOCK, (i + 1) * SUB_BLOCK)
                q_s, bl_s = q_h[sb], bl_h[sb]
                y = jnp.concatenate(
                    [q_s * jnp.exp(jnp.minimum(bl_s - bl_s[j:j + 1], 0.0)) for j in range(SUB_BLOCK)], axis=0)
                res = _dot_nt(y.astype(BF16), kb_h)
                a_i = jnp.zeros((SUB_BLOCK, chunk), F32)
                for j in range(SUB_BLOCK):
                    a_i = jnp.where(rcol == i * SUB_BLOCK + j, res[j * SUB_BLOCK:(j + 1) * SUB_BLOCK], a_i)
                a_i = jnp.where(rcol <= i * SUB_BLOCK + rrow, a_i, 0.0)
                if i > 0:
                    r_i = rr[i * SUB_BLOCK:i * SUB_BLOCK + 1, kcols]
                    kt = (k_h * jnp.exp(jnp.minimum(r_i - b_h, 0.0))).astype(BF16)
                    a_off = _dot_nt(qt_h[sb].astype(BF16), kt)
                    a_i = jnp.where(rcol < i * SUB_BLOCK, a_off, a_i)
                a_rows.append(a_i)
            a_mat = (jnp.concatenate(a_rows, axis=0) if nsb > 1 else a_rows[0]).astype(BF16)
            v_h = v_ref[0, rows, vcols].astype(BF16)
            st = st_ref[0, h]
            o = _dot(a_mat, v_h) + _dot_nt(qhat[:, kcols].astype(BF16), st.astype(BF16))
            st_ref[0, h] = st * decay[:, kcols] + _dot_tn(v_h, kdec[:, kcols])
            _gla_finish(o, h, rows, r_ref, gain_ref, a_ref)
        return carry

    lax.fori_loop(0, n_chunks, chunk_body, 0)


def _gla(x, w_in, w_up, b_gk, s0t, gain, chunk, chunks_per_step, wide_chunk, seqs_per_step=1, name="gla"):
    bsz, t_len, d = x.shape
    rows, n_seq = chunk * chunks_per_step, seqs_per_step
    assert t_len % rows == 0 and bsz % n_seq == 0
    const = lambda b, s: (0, 0)
    resident = lambda w: pl.BlockSpec(w.shape, const, pipeline_mode=pl.Buffered(1))
    st_shape = (n_seq, B_HEADS, B_VAL_DIM, B_KEY_DIM)
    st_spec = pl.BlockSpec(st_shape, lambda b, s: (b, 0, 0, 0))
    out_spec = pl.BlockSpec((n_seq, rows, B_VD), lambda b, s: (b, s, 0))
    return pl.pallas_call(
        functools.partial(_gla_kernel, chunk=chunk, n_chunks=chunks_per_step, wide_chunk=wide_chunk),
        grid=(bsz // n_seq, t_len // rows),
        in_specs=[pl.BlockSpec((n_seq, rows, d), lambda b, s: (b, s, 0)),
                  resident(w_in), resident(w_up), pl.BlockSpec((1, B_QK), const),
                  st_spec, pl.BlockSpec((1, B_VD), const)],
        out_specs=[out_spec, st_spec],
        out_shape=[jax.ShapeDtypeStruct((bsz, t_len, B_VD), BF16),
                   jax.ShapeDtypeStruct(s0t.shape, F32)],
        scratch_shapes=[pltpu.VMEM((n_seq, rows, B_QK), F32)] * 3 + [pltpu.VMEM((n_seq, rows, B_VD), F32)] * 2
                       + [pltpu.VMEM(st_shape, F32)],
        compiler_params=_params("parallel", "arbitrary"),
        name=name,
    )(x, w_in, w_up, b_gk.reshape(1, B_QK), s0t, gain.reshape(1, B_VD))


def kernel(x_prompt, x_sample, cache_a_k, cache_a_v, state_b, w_in_a, rel_bias_a, w_out_a, w_in_b, w_gk_up_b,
           b_gk_b, gn_gain_b, w_out_b, w_ffn_in, w_ffn_out, ln1_g, ln1_b, ln2_g, ln2_b):
    bsz, t_len, d = x_prompt.shape
    dbsz, dt_len, _ = x_sample.shape
    n_cache = cache_a_k.shape[2]
    keep = min(BAND_PAST, t_len)
    xp = x_prompt.reshape(bsz * t_len, d)
    xs = x_sample.reshape(dbsz * dt_len, d)

    w_oa = w_out_a[0].astype(BF16)
    bias_p = _band_bias_vectors(rel_bias_a[0])
    bias_s = _sample_bias_vectors(rel_bias_a[0], n_cache, dt_len)

    q, k, v, k_tail, v_tail = _qkv(xp, w_in_a[0], BF16, t_len, tail_rows=keep, name="qkv_prompt")
    att = _attn_prompt(q.reshape(bsz, t_len, d), k.reshape(bsz, t_len, d), v.reshape(bsz, t_len, d), bias_p)

    qs, ks, vs = _qkv(xs, w_in_a[0], F32, dt_len, name="qkv_sample")
    heads_first = lambda a: jnp.transpose(a, (0, 2, 3, 1)).reshape(a.shape[0], d, a.shape[1])
    att_s = _attn_sample(qs.reshape(dbsz, dt_len, d), ks.reshape(dbsz, dt_len, d), vs.reshape(dbsz, dt_len, d),
                         heads_first(cache_a_k[0]), heads_first(cache_a_v[0]), bias_s)

    wi, wo = w_ffn_in.astype(BF16), w_ffn_out.astype(BF16)
    ln1, ln2 = (ln1_g[0], ln1_b[0]), (ln2_g[0], ln2_b[0])
    xp = _layer_tail(att.reshape(bsz * t_len, d), w_oa, xp, ln1, wi, wo, 0, ln2, name="tail0_prompt")
    xs = _layer_tail(att_s.reshape(dbsz * dt_len, d), w_oa, xs, ln1, wi, wo, 0, ln2, tm=128, name="tail0_sample")

    w_ob = w_out_b[0].astype(BF16)
    w_in_t = w_in_b[0].T

    s0 = jnp.zeros((bsz, B_HEADS, B_VAL_DIM, B_KEY_DIM), F32)
    a_p, st_p = _gla(xp.reshape(bsz, t_len, d), w_in_t, w_gk_up_b[0], b_gk_b[0], s0, gn_gain_b[0],
                     CHUNK, min(GLA_STEP_ROWS, t_len) // CHUNK, GLA_WIDE_CHUNK, name="gla_prompt")
    a_s, st_s = _gla(xs.reshape(dbsz, dt_len, d), w_in_t, w_gk_up_b[0], b_gk_b[0],
                     jnp.swapaxes(state_b[0], -1, -2), gn_gain_b[0], dt_len, 1, None, seqs_per_step=dbsz,
                     name="gla_sample")

    ln1, ln2 = (ln1_g[1], ln1_b[1]), (ln2_g[1], ln2_b[1])
    xp = _layer_tail(a_p.reshape(bsz * t_len, B_VD), w_ob, xp, ln1, wi, wo, 1, ln2, name="tail1_prompt")
    xs = _layer_tail(a_s.reshape(dbsz * dt_len, B_VD), w_ob, xs, ln1, wi, wo, 1, ln2, tm=128, name="tail1_sample")

    heads = lambda a, n, t: a.reshape(1, n, t, A_HEADS, A_HEAD_DIM)
    seq_first = lambda a: jnp.transpose(a.reshape(a.shape[0], A_HEADS, A_HEAD_DIM, a.shape[2]), (0, 3, 1, 2))[None]
    return (xp.reshape(bsz, t_len, d), xs.reshape(dbsz, dt_len, d),
            seq_first(k_tail), seq_first(v_tail), jnp.swapaxes(st_p, -1, -2)[None],
            heads(ks, dbsz, dt_len), heads(vs, dbsz, dt_len), jnp.swapaxes(st_s, -1, -2)[None])
```

```python
import functools

import jax
import jax.numpy as jnp
from jax import lax
from jax.experimental import pallas as pl
from jax.experimental.pallas import tpu as pltpu

F32 = jnp.float32
BF16 = jnp.bfloat16

D_MODEL = 1024
CHUNK = 64
A_HEADS = 16
A_HEAD_DIM = D_MODEL // A_HEADS
BAND_CHUNKS = 8
BAND_PAST = BAND_CHUNKS * CHUNK
REL_CLIP = 128
B_HEADS = 4
B_KEY_DIM = D_MODEL // 2 // B_HEADS
B_VAL_DIM = D_MODEL // B_HEADS
B_QK = B_HEADS * B_KEY_DIM
B_VD = B_HEADS * B_VAL_DIM
B_GATE_RANK = 16
B_GATE_NORM = 16.0
D_FF = -(-8 * D_MODEL // 768) * 256
DEPTH = 2
ALPHA = (2.0 * DEPTH) ** 0.25
LN_EPS = 1e-5
GN_EPS = 1e-6
NEG_INF = -1e30

LANES = 128
SUB_BLOCK = 16
ATT_TILE = 4 * CHUNK
GLA_STEP_ROWS = 1024
GLA_WIDE_CHUNK = 256
GLA_MAX_LOG_DECAY = 60.0
V7X_VMEM_BYTES = 64 * 1024 * 1024
VMEM_LIMIT = V7X_VMEM_BYTES * 7 // 8


def _dot(a, b):
    return jnp.dot(a, b, preferred_element_type=F32)


def _dot_nt(a, b):
    return lax.dot_general(a, b, (((1,), (1,)), ((), ())), preferred_element_type=F32)


def _dot_tn(a, b):
    return lax.dot_general(a, b, (((0,), (0,)), ((), ())), preferred_element_type=F32)


def _params(*sem):
    return pltpu.CompilerParams(dimension_semantics=sem, vmem_limit_bytes=VMEM_LIMIT)


def _row_tile(m, pref):
    t = min(m, pref)
    assert m % t == 0
    return t


def _qkv_kernel(x_ref, w_ref, q_ref, k_ref, v_ref, *tail_refs, n_chunk, tiles_per_seq):
    xb = x_ref[...].astype(BF16)
    tm, d = q_ref.shape
    if tail_refs:
        *tail_refs, stash_ref = tail_refs
    for idx, o_ref in enumerate((q_ref, k_ref, v_ref)):
        for c in range(0, d, n_chunk):
            y = _dot(xb, w_ref[:, idx * d + c:idx * d + c + n_chunk].astype(BF16))
            if idx == 0:
                y = y * (A_HEAD_DIM ** -0.5)
            o_ref[:, c:c + n_chunk] = y.astype(o_ref.dtype)
            if tail_refs and idx > 0:
                stash_ref[idx - 1, :, c:c + n_chunk] = y[tm - stash_ref.shape[1]:, :]
    if not tail_refs:
        return

    @pl.when(pl.program_id(0) % tiles_per_seq == tiles_per_seq - 1)
    def _():
        for idx, tail_ref in enumerate(tail_refs):
            for c in range(0, d, n_chunk):
                tail_ref[0, c:c + n_chunk, :] = stash_ref[idx, :, c:c + n_chunk].T


def _qkv(x, w, dtype, t_len, tail_rows=0, tm=1024, name="qkv"):
    m, d = x.shape
    tm = _row_tile(t_len if tail_rows else m, tm)
    n_chunk = 512
    assert d % n_chunk == 0 and w.shape == (d, 3 * d) and tail_rows <= tm
    tiles_per_seq = t_len // tm
    rows = pl.BlockSpec((tm, d), lambda i: (i, 0))
    out_specs, out_shape = [rows] * 3, [jax.ShapeDtypeStruct((m, d), dtype)] * 3
    if tail_rows:
        out_specs += [pl.BlockSpec((1, d, tail_rows), lambda i: (i // tiles_per_seq, 0, 0))] * 2
        out_shape += [jax.ShapeDtypeStruct((m // t_len, d, tail_rows), F32)] * 2
    return pl.pallas_call(
        functools.partial(_qkv_kernel, n_chunk=n_chunk, tiles_per_seq=tiles_per_seq),
        grid=(m // tm,),
        in_specs=[rows, pl.BlockSpec(w.shape, lambda i: (0, 0), pipeline_mode=pl.Buffered(1))],
        out_specs=out_specs,
        out_shape=out_shape,
        scratch_shapes=[pltpu.VMEM((2, tail_rows, d), F32)] if tail_rows else [],
        compiler_params=_params("arbitrary"),
        name=name,
    )(x, w)


def _res_ln(x, y, g, b):
    t = ALPHA * x + y
    mu = jnp.mean(t, axis=-1, keepdims=True)
    d = t - mu
    var = jnp.mean(d * d, axis=-1, keepdims=True)
    return d * lax.rsqrt(var + LN_EPS) * g + b


def _layer_tail_kernel(a_ref, wp_ref, x_ref, g1_ref, b1_ref, wi_ref, wo_ref, g2_ref, b2_ref, o_ref, *h_refs,
                       ff_chunk):
    n = len(h_refs)
    tm = x_ref.shape[0]
    d_ff = h_refs[0].shape[1]
    rows = [slice(i * (tm // n), (i + 1) * (tm // n)) for i in range(n)]
    x1, y = [None] * n, [None] * n

    def ln1(i):
        x1[i] = _res_ln(x_ref[rows[i], :], _dot(a_ref[rows[i], :], wp_ref[...]), g1_ref[...], b1_ref[...])

    def ffn(i):
        xb = x1[i].astype(BF16)
        for c in range(0, d_ff, ff_chunk):
            gate = _dot(xb, wi_ref[:, c:c + ff_chunk])
            up = _dot(xb, wi_ref[:, d_ff + c:d_ff + c + ff_chunk])
            h_refs[i][:, c:c + ff_chunk] = (gate * jax.nn.sigmoid(gate) * up).astype(BF16)
        y[i] = _dot(h_refs[i][...], wo_ref[...])

    def ln2(i):
        o_ref[rows[i], :] = _res_ln(x1[i], y[i], g2_ref[...], b2_ref[...])

    ln1(0)
    for i in range(n):
        if i + 1 < n:
            ln1(i + 1)
        ffn(i)
        if i > 0:
            ln2(i - 1)
    ln2(n - 1)


def _layer_tail(a, wp, x, ln1, wi, wo, layer, ln2, tm=1024, name="layer_tail"):
    m, k = a.shape
    d = x.shape[1]
    d_ff = wo.shape[1]
    tm = _row_tile(m, tm)
    ff_chunk = 256
    row_groups = 4 if tm % 1024 == 0 else 1
    assert d_ff % ff_chunk == 0
    rows = lambda width: pl.BlockSpec((tm, width), lambda i: (i, 0))
    resident = lambda w: pl.BlockSpec(w.shape, lambda i: (0, 0), pipeline_mode=pl.Buffered(1))
    of_layer = lambda w: pl.BlockSpec((None,) + w.shape[1:], lambda i: (layer, 0, 0), pipeline_mode=pl.Buffered(1))
    vec = pl.BlockSpec((1, d), lambda i: (0, 0))
    return pl.pallas_call(
        functools.partial(_layer_tail_kernel, ff_chunk=ff_chunk),
        grid=(m // tm,),
        in_specs=[rows(k), resident(wp), rows(d), vec, vec, of_layer(wi), of_layer(wo), vec, vec],
        out_specs=rows(d),
        out_shape=jax.ShapeDtypeStruct((m, d), F32),
        scratch_shapes=[pltpu.VMEM((tm // row_groups, d_ff), BF16)] * row_groups,
        compiler_params=_params("parallel"),
        name=name,
    )(a, wp, x, ln1[0].reshape(1, d), ln1[1].reshape(1, d), wi, wo, ln2[0].reshape(1, d), ln2[1].reshape(1, d))


def _attn_prompt_kernel(q_ref, ka_ref, kb_ref, kc_ref, va_ref, vb_ref, vc_ref, w_ref, o_ref,
                        bias_ref, vext_ref):
    tile = pl.program_id(1)
    win = 3 * ATT_TILE

    @pl.when(tile <= 2)
    def _():
        qc = lax.broadcasted_iota(jnp.int32, (ATT_TILE, win), 0) // CHUNK
        kc = lax.broadcasted_iota(jnp.int32, (ATT_TILE, win), 1) // CHUNK
        valid = (kc >= qc) & (kc <= qc + BAND_CHUNKS) & (kc >= BAND_CHUNKS - tile * (ATT_TILE // CHUNK))
        for h in range(A_HEADS):
            w_rows = jnp.broadcast_to(w_ref[h:h + 1, :], (ATT_TILE, w_ref.shape[1]))
            toeplitz = pltpu.roll(w_rows, 0, 1, stride=1, stride_axis=0)[:, :win]
            bias_ref[h // 2, (h % 2) * ATT_TILE:(h % 2 + 1) * ATT_TILE, :] = jnp.where(valid, toeplitz, NEG_INF)

    n_groups = D_MODEL // LANES

    @pl.when(tile == 0)
    def _():
        vext_ref[:, :, :, LANES:] = jnp.ones((3, n_groups, ATT_TILE, LANES), BF16)

    k_refs, v_refs = (ka_ref, kb_ref, kc_ref), (va_ref, vb_ref, vc_ref)
    for j, v_ref in enumerate(v_refs):
        for p in range(n_groups):
            vext_ref[j, p, :, :LANES] = v_ref[0, :, p * LANES:(p + 1) * LANES]

    lane = lax.broadcasted_iota(jnp.int32, (ATT_TILE, LANES), 1)
    low_half = lane < A_HEAD_DIM

    def scores(p):
        cols = slice(p * LANES, (p + 1) * LANES)
        qp = q_ref[0, :, cols]
        zero = jnp.zeros_like(qp)
        q_pair = jnp.concatenate([jnp.where(low_half, qp, zero), jnp.where(low_half, zero, qp)], axis=0)
        return [_dot_nt(q_pair, k_ref[0, :, cols]) + bias_ref[p, :, j * ATT_TILE:(j + 1) * ATT_TILE]
                for j, k_ref in enumerate(k_refs)]

    ahead = 2
    pending = [scores(p) for p in range(ahead)]
    for p in range(n_groups):
        s_parts = pending.pop(0)
        if p + ahead < n_groups:
            pending.append(scores(p + ahead))
        m = functools.reduce(jnp.maximum, [s.max(axis=-1, keepdims=True) for s in s_parts])
        pv = sum(_dot(jnp.exp(s - m).astype(BF16), vext_ref[j, p]) for j, s in enumerate(s_parts))
        pv = pv[:, :LANES] / pv[:, LANES:]
        o_ref[0, :, p * LANES:(p + 1) * LANES] = jnp.where(low_half, pv[:ATT_TILE], pv[ATT_TILE:]).astype(o_ref.dtype)


def _band_bias_vectors(rel_table):
    n_heads = rel_table.shape[1]
    win = 3 * ATT_TILE
    t = rel_table.T.astype(F32)
    u = jnp.concatenate([jnp.broadcast_to(t[:, 2 * REL_CLIP:], (n_heads, win - REL_CLIP)),
                         jnp.flip(t[:, 1:2 * REL_CLIP], axis=1),
                         jnp.broadcast_to(t[:, :1], (n_heads, REL_CLIP))], axis=1)
    assert u.shape[1] == ATT_TILE + win - 1
    return jnp.roll(jnp.pad(u, ((0, 0), (0, 1))), -(ATT_TILE - 1), axis=1)


def _attn_prompt(q, k, v, bias_vectors):
    bsz, t_len, d = q.shape
    assert t_len % ATT_TILE == 0 and BAND_PAST == 2 * ATT_TILE
    blk = (1, ATT_TILE, d)

    def past(n):
        return lambda b, i: (b, jnp.maximum(i - n, 0), 0)

    kv_specs = [pl.BlockSpec(blk, past(2)), pl.BlockSpec(blk, past(1)), pl.BlockSpec(blk, past(0))]
    return pl.pallas_call(
        _attn_prompt_kernel,
        grid=(bsz, t_len // ATT_TILE),
        in_specs=[pl.BlockSpec(blk, lambda b, i: (b, i, 0))] + kv_specs + kv_specs
                 + [pl.BlockSpec(bias_vectors.shape, lambda b, i: (0, 0))],
        out_specs=pl.BlockSpec(blk, lambda b, i: (b, i, 0)),
        out_shape=jax.ShapeDtypeStruct((bsz, t_len, d), BF16),
        scratch_shapes=[pltpu.VMEM((A_HEADS // 2, 2 * ATT_TILE, 3 * ATT_TILE), F32),
                        pltpu.VMEM((3, d // LANES, ATT_TILE, 2 * LANES), BF16)],
        compiler_params=_params("parallel", "arbitrary"),
        name="attn_prompt",
    )(q, k, k, k, v, v, v, bias_vectors)


def _attn_sample_kernel(q_ref, kn_ref, vn_ref, kc_ref, vc_ref, w_ref, o_ref):
    t_len = q_ref.shape[1]
    n_cache = kc_ref.shape[2]
    lane = lax.broadcasted_iota(jnp.int32, (t_len, LANES), 1)
    low_half = lane < A_HEAD_DIM
    bias = [pltpu.roll(jnp.broadcast_to(w_ref[h:h + 1, :], (t_len, w_ref.shape[1])), 0, 1, stride=1, stride_axis=0)
            for h in range(A_HEADS)]
    groups = [slice(p * LANES, (p + 1) * LANES) for p in range(D_MODEL // LANES)]

    def scores(p, cols):
        qp = q_ref[0, :, cols].astype(BF16)
        zero = jnp.zeros_like(qp)
        q_pair = jnp.concatenate([jnp.where(low_half, qp, zero), jnp.where(low_half, zero, qp)], axis=0)
        b_pair = jnp.concatenate([bias[2 * p], bias[2 * p + 1]], axis=0)
        s_c = _dot(q_pair, kc_ref[0, cols, :].astype(BF16)) + b_pair[:, :n_cache]
        s_n = _dot_nt(q_pair, kn_ref[0, :, cols].astype(BF16)) + b_pair[:, n_cache:n_cache + t_len]
        return s_c, s_n

    def numerators(s_c, s_n):
        m = jnp.maximum(s_c.max(axis=-1, keepdims=True), s_n.max(axis=-1, keepdims=True))
        e_c, e_n = jnp.exp(s_c - m), jnp.exp(s_n - m)
        return e_c, e_n, e_c.sum(axis=-1, keepdims=True) + e_n.sum(axis=-1, keepdims=True)

    all_scores = [scores(p, cols) for p, cols in enumerate(groups)]
    all_numerators = [numerators(s_c, s_n) for s_c, s_n in all_scores]
    for cols, (e_c, e_n, l) in zip(groups, all_numerators):
        pv = _dot_nt(e_c.astype(BF16), vc_ref[0, cols, :].astype(BF16))
        pv = (pv + _dot(e_n.astype(BF16), vn_ref[0, :, cols].astype(BF16))) / l
        o_ref[0, :, cols] = jnp.where(low_half, pv[:t_len], pv[t_len:]).astype(o_ref.dtype)


def _sample_bias_vectors(rel_table, n_cache, t_len):
    assert t_len - 1 <= REL_CLIP <= n_cache + t_len
    n_heads = rel_table.shape[1]
    t = rel_table.T.astype(F32)
    u = jnp.concatenate([jnp.broadcast_to(t[:, 2 * REL_CLIP:], (n_heads, n_cache + t_len - REL_CLIP)),
                         jnp.flip(t[:, REL_CLIP - t_len + 1:2 * REL_CLIP], axis=1)], axis=1)
    assert u.shape[1] == n_cache + 2 * t_len - 1
    width = -(-u.shape[1] // LANES) * LANES
    return jnp.roll(jnp.pad(u, ((0, 0), (0, width - u.shape[1]))), -(t_len - 1), axis=1)


def _attn_sample(q, k_new, v_new, cache_k, cache_v, bias_vectors):
    bsz, t_len, d = q.shape
    n_cache = cache_k.shape[2]
    new_spec = pl.BlockSpec((1, t_len, d), lambda b: (b, 0, 0))
    cache_spec = pl.BlockSpec((1, d, n_cache), lambda b: (b, 0, 0))
    return pl.pallas_call(
        _attn_sample_kernel,
        grid=(bsz,),
        in_specs=[new_spec, new_spec, new_spec, cache_spec, cache_spec,
                  pl.BlockSpec(bias_vectors.shape, lambda b: (0, 0))],
        out_specs=new_spec,
        out_shape=jax.ShapeDtypeStruct((bsz, t_len, d), BF16),
        compiler_params=_params("parallel"),
        name="attn_sample",
    )(q, k_new, v_new, cache_k, cache_v, bias_vectors)


def _log_sigmoid(z):
    return jnp.minimum(z, 0.0) - jnp.log(1.0 + jnp.exp(-jnp.abs(z)))


def _gla_project(x_ref, w_ref, wup_ref, bgk_ref, q_ref, k_ref, g_ref, v_ref, r_ref):
    n_seq, n_rows, d = x_ref.shape
    xb = x_ref[...].reshape(n_seq * n_rows, d).astype(BF16)
    n_chunk = 512
    proj = lambda lo, hi: _dot_nt(xb, w_ref[lo:hi, :].astype(BF16))
    per_seq = lambda y: y.reshape(n_seq, n_rows, y.shape[-1])
    q_ref[...] = per_seq(proj(0, B_QK) * (B_KEY_DIM ** -0.5))
    k_ref[...] = per_seq(proj(B_QK, 2 * B_QK))
    for c in range(0, B_VD, n_chunk):
        v_ref[:, :, c:c + n_chunk] = per_seq(proj(2 * B_QK + c, 2 * B_QK + c + n_chunk))
        r_ref[:, :, c:c + n_chunk] = per_seq(proj(2 * B_QK + B_VD + c, 2 * B_QK + B_VD + c + n_chunk))
    n_main = 2 * B_QK + 2 * B_VD
    low = proj(n_main, n_main + B_GATE_RANK)
    z = _dot(low.astype(BF16), wup_ref[...].astype(BF16)) + bgk_ref[...]
    g_ref[...] = per_seq(_log_sigmoid(z) / B_GATE_NORM)


def _split3(x):
    hi = x.astype(BF16)
    r1 = x - hi.astype(F32)
    mid = r1.astype(BF16)
    lo = (r1 - mid.astype(F32)).astype(BF16)
    return hi, mid, lo


def _gla_finish(o, h, rows, r_ref, gain_ref, a_ref):
    vcols = slice(h * B_VAL_DIM, (h + 1) * B_VAL_DIM)
    o = o * lax.rsqrt(jnp.mean(o * o, axis=-1, keepdims=True) + GN_EPS) * gain_ref[:, vcols]
    r_h = r_ref[0, rows, vcols]
    half = 0.5 * r_h
    a_ref[0, rows, vcols] = (o * (half * (1.0 + jnp.tanh(half)))).astype(a_ref.dtype)


def _gla_rows_as_one_chunk(rows, b, q_ref, k_ref, v_ref, r_ref, gain_ref, a_ref, st_ref):
    n = b.shape[0]
    q, k = q_ref[0, rows, :], k_ref[0, rows, :]
    b_last = b[n - 1:n, :]
    qhat = (q * jnp.exp(b)).astype(BF16)
    decay = jnp.exp(b_last)
    k_scaled = k * jnp.exp(-b)
    kinv = k_scaled.astype(BF16)
    kdec = (k_scaled * decay).astype(BF16)
    causal = lax.broadcasted_iota(jnp.int32, (n, n), 1) <= lax.broadcasted_iota(jnp.int32, (n, n), 0)
    for h in range(B_HEADS):
        kcols = slice(h * B_KEY_DIM, (h + 1) * B_KEY_DIM)
        vcols = slice(h * B_VAL_DIM, (h + 1) * B_VAL_DIM)
        a_mat = jnp.where(causal, _dot_nt(qhat[:, kcols], kinv[:, kcols]), 0.0).astype(BF16)
        v_h = v_ref[0, rows, vcols].astype(BF16)
        st = st_ref[0, h]
        o = _dot(a_mat, v_h) + _dot_nt(qhat[:, kcols], st.astype(BF16))
        st_ref[0, h] = st * decay[:, kcols] + _dot_tn(v_h, kdec[:, kcols])
        _gla_finish(o, h, rows, r_ref, gain_ref, a_ref)


def _gla_kernel(x_ref, w_ref, wup_ref, bgk_ref, s0_ref, gain_ref, a_ref, st_ref,
                q_ref, k_ref, g_ref, v_ref, r_ref, st0_ref, *, chunk, n_chunks, wide_chunk):
    @pl.when(pl.program_id(1) == 0)
    def _():
        st_ref[...] = s0_ref[...]

    _gla_project(x_ref, w_ref, wup_ref, bgk_ref, q_ref, k_ref, g_ref, v_ref, r_ref)
    for s in range(x_ref.shape[0]):
        one = lambda ref: ref.at[pl.ds(s, 1)]
        _gla_sequence(one(q_ref), one(k_ref), one(g_ref), one(v_ref), one(r_ref), gain_ref, one(a_ref),
                      one(st_ref), one(st0_ref), chunk=chunk, n_chunks=n_chunks, wide_chunk=wide_chunk)


def _gla_sequence(q_ref, k_ref, g_ref, v_ref, r_ref, gain_ref, a_ref, st_ref, st0_ref, *, chunk, n_chunks,
                  wide_chunk):
    n = chunk * n_chunks

    def by_sub_blocks():
        _gla_chunks_by_sub_blocks(q_ref, k_ref, g_ref, v_ref, r_ref, gain_ref, a_ref, st_ref,
                                  chunk=chunk, n_chunks=n_chunks)

    if not wide_chunk:
        by_sub_blocks()
        return
    st0_ref[...] = st_ref[...]
    tri = (lax.broadcasted_iota(jnp.int32, (wide_chunk, wide_chunk), 1)
           <= lax.broadcasted_iota(jnp.int32, (wide_chunk, wide_chunk), 0))
    tri = jnp.where(tri, 1.0, 0.0).astype(BF16)
    min_log_decay = None
    for r0 in range(0, n, wide_chunk):
        rows = slice(r0, r0 + wide_chunk)
        b = sum(_dot(tri, gp) for gp in _split3(g_ref[0, rows, :]))
        _gla_rows_as_one_chunk(rows, b, q_ref, k_ref, v_ref, r_ref, gain_ref, a_ref, st_ref)
        total = b[wide_chunk - 1:wide_chunk, :]
        min_log_decay = total if min_log_decay is None else jnp.minimum(min_log_decay, total)

    @pl.when(jnp.logical_not(jnp.min(min_log_decay) >= -GLA_MAX_LOG_DECAY))
    def _():
        st_ref[...] = st0_ref[...]
        by_sub_blocks()


def _gla_chunks_by_sub_blocks(q_ref, k_ref, g_ref, v_ref, r_ref, gain_ref, a_ref, st_ref, *, chunk, n_chunks):
    nsb = chunk // SUB_BLOCK
    row = lax.broadcasted_iota(jnp.int32, (chunk, chunk), 0)
    col = lax.broadcasted_iota(jnp.int32, (chunk, chunk), 1)
    blk_start = (row // SUB_BLOCK) * SUB_BLOCK
    tri_local = jnp.where((col <= row) & (col >= blk_start), 1.0, 0.0).astype(BF16)
    tri_before = jnp.where(col < blk_start, 1.0, 0.0).astype(BF16)
    rcol = lax.broadcasted_iota(jnp.int32, (SUB_BLOCK, chunk), 1)
    rrow = lax.broadcasted_iota(jnp.int32, (SUB_BLOCK, chunk), 0)

    def chunk_body(c, carry):
        row0 = pl.multiple_of(c * chunk, chunk)
        rows = pl.ds(row0, chunk)
        q = q_ref[0, rows, :]
        k = k_ref[0, rows, :]
        g_parts = _split3(g_ref[0, rows, :])
        bl = sum(_dot(tri_local, gp) for gp in g_parts)
        if nsb > 1:
            rr = sum(_dot(tri_before, gp) for gp in g_parts)
            b = bl + rr
        else:
            rr = None
            b = bl
        b_last = b[chunk - 1:chunk, :]
        qt = q * jnp.exp(bl)
        qhat = qt * jnp.exp(rr) if nsb > 1 else qt
        kdec = (k * jnp.exp(b_last - b)).astype(BF16)
        decay = jnp.exp(b_last)
        kb = k.astype(BF16)

        for h in range(B_HEADS):
            kcols = slice(h * B_KEY_DIM, (h + 1) * B_KEY_DIM)
            vcols = slice(h * B_VAL_DIM, (h + 1) * B_VAL_DIM)
            q_h, k_h, bl_h, b_h, qt_h = q[:, kcols], k[:, kcols], bl[:, kcols], b[:, kcols], qt[:, kcols]
            kb_h = kb[:, kcols]
            a_rows = []
            for i in range(nsb):
                sb = slice(i * SUB_BLOCK, (i + 1) * SUB_BLOCK)
                q_s, bl_s = q_h[sb], bl_h[sb]
                y = jnp.concatenate(
                    [q_s * jnp.exp(jnp.minimum(bl_s - bl_s[j:j + 1], 0.0)) for j in range(SUB_BLOCK)], axis=0)
                res = _dot_nt(y.astype(BF16), kb_h)
                a_i = jnp.zeros((SUB_BLOCK, chunk), F32)
                for j in range(SUB_BLOCK):
                    a_i = jnp.where(rcol == i * SUB_BLOCK + j, res[j * SUB_BLOCK:(j + 1) * SUB_BLOCK], a_i)
                a_i = jnp.where(rcol <= i * SUB_BLOCK + rrow, a_i, 0.0)
                if i > 0:
                    r_i = rr[i * SUB_BLOCK:i * SUB_BLOCK + 1, kcols]
                    kt = (k_h * jnp.exp(jnp.minimum(r_i - b_h, 0.0))).astype(BF16)
                    a_off = _dot_nt(qt_h[sb].astype(BF16), kt)
                    a_i = jnp.where(rcol < i * SUB_BLOCK, a_off, a_i)
                a_rows.append(a_i)
            a_mat = (jnp.concatenate(a_rows, axis=0) if nsb > 1 else a_rows[0]).astype(BF16)
            v_h = v_ref[0, rows, vcols].astype(BF16)
            st = st_ref[0, h]
            o = _dot(a_mat, v_h) + _dot_nt(qhat[:, kcols].astype(BF16), st.astype(BF16))
            st_ref[0, h] = st * decay[:, kcols] + _dot_tn(v_h, kdec[:, kcols])
            _gla_finish(o, h, rows, r_ref, gain_ref, a_ref)
        return carry

    lax.fori_loop(0, n_chunks, chunk_body, 0)


def _gla(x, w_in, w_up, b_gk, s0t, gain, chunk, chunks_per_step, wide_chunk, seqs_per_step=1, name="gla"):
    bsz, t_len, d = x.shape
    rows, n_seq = chunk * chunks_per_step, seqs_per_step
    assert t_len % rows == 0 and bsz % n_seq == 0
    const = lambda b, s: (0, 0)
    resident = lambda w: pl.BlockSpec(w.shape, const, pipeline_mode=pl.Buffered(1))
    st_shape = (n_seq, B_HEADS, B_VAL_DIM, B_KEY_DIM)
    st_spec = pl.BlockSpec(st_shape, lambda b, s: (b, 0, 0, 0))
    out_spec = pl.BlockSpec((n_seq, rows, B_VD), lambda b, s: (b, s, 0))
    return pl.pallas_call(
        functools.partial(_gla_kernel, chunk=chunk, n_chunks=chunks_per_step, wide_chunk=wide_chunk),
        grid=(bsz // n_seq, t_len // rows),
        in_specs=[pl.BlockSpec((n_seq, rows, d), lambda b, s: (b, s, 0)),
                  resident(w_in), resident(w_up), pl.BlockSpec((1, B_QK), const),
                  st_spec, pl.BlockSpec((1, B_VD), const)],
        out_specs=[out_spec, st_spec],
        out_shape=[jax.ShapeDtypeStruct((bsz, t_len, B_VD), BF16),
                   jax.ShapeDtypeStruct(s0t.shape, F32)],
        scratch_shapes=[pltpu.VMEM((n_seq, rows, B_QK), F32)] * 3 + [pltpu.VMEM((n_seq, rows, B_VD), F32)] * 2
                       + [pltpu.VMEM(st_shape, F32)],
        compiler_params=_params("parallel", "arbitrary"),
        name=name,
    )(x, w_in, w_up, b_gk.reshape(1, B_QK), s0t, gain.reshape(1, B_VD))


def kernel(x_prompt, x_sample, cache_a_k, cache_a_v, state_b, w_in_a, rel_bias_a, w_out_a, w_in_b, w_gk_up_b,
           b_gk_b, gn_gain_b, w_out_b, w_ffn_in, w_ffn_out, ln1_g, ln1_b, ln2_g, ln2_b):
    bsz, t_len, d = x_prompt.shape
    dbsz, dt_len, _ = x_sample.shape
    n_cache = cache_a_k.shape[2]
    keep = min(BAND_PAST, t_len)
    xp = x_prompt.reshape(bsz * t_len, d)
    xs = x_sample.reshape(dbsz * dt_len, d)

    w_oa = w_out_a[0].astype(BF16)
    bias_p = _band_bias_vectors(rel_bias_a[0])
    bias_s = _sample_bias_vectors(rel_bias_a[0], n_cache, dt_len)

    q, k, v, k_tail, v_tail = _qkv(xp, w_in_a[0], BF16, t_len, tail_rows=keep, name="qkv_prompt")
    att = _attn_prompt(q.reshape(bsz, t_len, d), k.reshape(bsz, t_len, d), v.reshape(bsz, t_len, d), bias_p)

    qs, ks, vs = _qkv(xs, w_in_a[0], F32, dt_len, name="qkv_sample")
    heads_first = lambda a: jnp.transpose(a, (0, 2, 3, 1)).reshape(a.shape[0], d, a.shape[1])
    att_s = _attn_sample(qs.reshape(dbsz, dt_len, d), ks.reshape(dbsz, dt_len, d), vs.reshape(dbsz, dt_len, d),
                         heads_first(cache_a_k[0]), heads_first(cache_a_v[0]), bias_s)

    wi, wo = w_ffn_in.astype(BF16), w_ffn_out.astype(BF16)
    ln1, ln2 = (ln1_g[0], ln1_b[0]), (ln2_g[0], ln2_b[0])
    xp = _layer_tail(att.reshape(bsz * t_len, d), w_oa, xp, ln1, wi, wo, 0, ln2, name="tail0_prompt")
    xs = _layer_tail(att_s.reshape(dbsz * dt_len, d), w_oa, xs, ln1, wi, wo, 0, ln2, tm=128, name="tail0_sample")

    w_ob = w_out_b[0].astype(BF16)
    w_in_t = w_in_b[0].T

    s0 = jnp.zeros((bsz, B_HEADS, B_VAL_DIM, B_KEY_DIM), F32)
    a_p, st_p = _gla(xp.reshape(bsz, t_len, d), w_in_t, w_gk_up_b[0], b_gk_b[0], s0, gn_gain_b[0],
                     CHUNK, min(GLA_STEP_ROWS, t_len) // CHUNK, GLA_WIDE_CHUNK, name="gla_prompt")
    a_s, st_s = _gla(xs.reshape(dbsz, dt_len, d), w_in_t, w_gk_up_b[0], b_gk_b[0],
                     jnp.swapaxes(state_b[0], -1, -2), gn_gain_b[0], dt_len, 1, None, seqs_per_step=dbsz,
                     name="gla_sample")

    ln1, ln2 = (ln1_g[1], ln1_b[1]), (ln2_g[1], ln2_b[1])
    xp = _layer_tail(a_p.reshape(bsz * t_len, B_VD), w_ob, xp, ln1, wi, wo, 1, ln2, name="tail1_prompt")
    xs = _layer_tail(a_s.reshape(dbsz * dt_len, B_VD), w_ob, xs, ln1, wi, wo, 1, ln2, tm=128, name="tail1_sample")

    heads = lambda a, n, t: a.reshape(1, n, t, A_HEADS, A_HEAD_DIM)
    seq_first = lambda a: jnp.transpose(a.reshape(a.shape[0], A_HEADS, A_HEAD_DIM, a.shape[2]), (0, 3, 1, 2))[None]
    return (xp.reshape(bsz, t_len, d), xs.reshape(dbsz, dt_len, d),
            seq_first(k_tail), seq_first(v_tail), jnp.swapaxes(st_p, -1, -2)[None],
            heads(ks, dbsz, dt_len), heads(vs, dbsz, dt_len), jnp.swapaxes(st_s, -1, -2)[None])
```
